```python
import math
import jax
import jax.numpy as jnp
from jax import lax
import numpy as np


D_MODEL = 1024
BATCH = 8
SEQ = 4096
DEPTH = 4

GRID_W = 64
CTX_LEN = 256
HEAD_DIM = 64
ROPE_THETA = 10000.0
Q_BLOCK = 128
EPS = 1e-6
ADA_CHUNKS = 6

GQA_HEADS = 8
GQA_KV_HEADS = 2
DIFF_HEADS = 4
DIFF_V_DIM = 2 * HEAD_DIM
CONV_CH = 512
CONV_WIDTH = 31
HGRN_HEADS = 8
HGRN_DK = 64
HGRN_DV = 64
HGRN_CHUNK = 64
N_BRANCH = 4
BRANCH_W = 512

N_EXPERTS = 8
TOP_K = 2
D_FF = 3584
N_DENSE = (DEPTH + 1) // 2
N_MOE = DEPTH // 2

SPLIT_SIZES = (
    GQA_HEADS * HEAD_DIM,
    GQA_KV_HEADS * HEAD_DIM,
    GQA_KV_HEADS * HEAD_DIM,
    DIFF_HEADS * 2 * HEAD_DIM,
    DIFF_HEADS * 2 * HEAD_DIM,
    DIFF_HEADS * DIFF_V_DIM,
    2 * CONV_CH,
    HGRN_HEADS * HGRN_DK,
    HGRN_HEADS * HGRN_DK,
    HGRN_HEADS * HGRN_DK,
    HGRN_HEADS * HGRN_DV,
    HGRN_HEADS * HGRN_DV,
    N_BRANCH * D_MODEL,
)
SPLIT_POINTS = tuple(int(v) for v in np.cumsum(SPLIT_SIZES)[:-1])
D_IN = int(sum(SPLIT_SIZES))

kernel_name = 'hybrid_gated_branch_dit_moe'


def _rmsnorm(x, g):
    xf = x.astype(jnp.float32)
    y = xf * lax.rsqrt(jnp.mean(jnp.square(xf), axis=-1, keepdims=True) + EPS)
    return (y * g.astype(jnp.float32)).astype(x.dtype)


def _layernorm(x, g, b):
    xf = x.astype(jnp.float32)
    mu = jnp.mean(xf, axis=-1, keepdims=True)
    var = jnp.mean(jnp.square(xf - mu), axis=-1, keepdims=True)
    y = (xf - mu) * lax.rsqrt(var + EPS) * g.astype(jnp.float32) + b.astype(jnp.float32)
    return y.astype(x.dtype)


def _modulate(h, shift, scale):
    return h * (1 + scale) + shift


def _swiglu(h, w1, w3, w2):
    return (jax.nn.silu(h @ w1) * (h @ w3)) @ w2


def _axial_rope_tables(n_tokens, rows):
    row = jnp.repeat(jnp.arange(rows, dtype=jnp.float32), GRID_W)
    col = (jnp.arange(n_tokens) % GRID_W).astype(jnp.float32)
    axis_dim = HEAD_DIM // 2
    inv_freq = ROPE_THETA ** (-jnp.arange(0, axis_dim, 2, dtype=jnp.float32) / axis_dim)
    ang_r = row[:, None] * inv_freq
    ang_c = col[:, None] * inv_freq
    return (jnp.cos(ang_r), jnp.sin(ang_r), jnp.cos(ang_c), jnp.sin(ang_c))


def _rope_1d(x, cos, sin):
    x1, x2 = jnp.split(x, 2, axis=-1)
    cos = cos[:, None, :].astype(x.dtype)
    sin = sin[:, None, :].astype(x.dtype)
    return jnp.concatenate([x1 * cos - x2 * sin, x2 * cos + x1 * sin], axis=-1)


def _rope_axial(x, rope):
    cos_r, sin_r, cos_c, sin_c = rope
    xr, xc = jnp.split(x, 2, axis=-1)
    return jnp.concatenate([_rope_1d(xr, cos_r, sin_r), _rope_1d(xc, cos_c, sin_c)], axis=-1)


def _scores(q, k):
    return jnp.einsum('bhgqd,bhkd->bhgqk', q, k).astype(jnp.float32) * (q.shape[-1] ** -0.5)


def _dense_attend(q, k, v):
    p = jax.nn.softmax(_scores(q, k), axis=-1).astype(v.dtype)
    return jnp.einsum('bhgqk,bhkd->bhgqd', p, v)


def _block_attend(q, k_x, v_x, k_c, v_c):
    b, h, g, s, d = q.shape
    n_blk = s // Q_BLOCK
    q_blocks = jnp.moveaxis(q.reshape(b, h, g, n_blk, Q_BLOCK, d), 3, 0)
    n_x = k_x.shape[2]

    def one_block(qi):
        sc = jnp.concatenate([_scores(qi, k_x), _scores(qi, k_c)], axis=-1)
        p = jax.nn.softmax(sc, axis=-1).astype(v_x.dtype)
        return (jnp.einsum('bhgqk,bhkd->bhgqd', p[..., :n_x], v_x)
                + jnp.einsum('bhgqk,bhkd->bhgqd', p[..., n_x:], v_c))

    o = lax.map(one_block, q_blocks)
    return jnp.moveaxis(o, 0, 3).reshape(b, h, g, s, v_x.shape[-1])


def _gqa_branch(aq, ak, av, g_q, g_k, rope, n_ctx, need_ctx):
    b, t, _ = aq.shape
    q = _rmsnorm(aq.reshape(b, t, GQA_HEADS, HEAD_DIM), g_q)
    k = _rmsnorm(ak.reshape(b, t, GQA_KV_HEADS, HEAD_DIM), g_k)
    v = av.reshape(b, t, GQA_KV_HEADS, HEAD_DIM)
    q_x = _rope_axial(q[:, n_ctx:], rope)
    k_x = _rope_axial(k[:, n_ctx:], rope)
    grp = GQA_HEADS // GQA_KV_HEADS

    def q_heads(u):
        return u.reshape(b, u.shape[1], GQA_KV_HEADS, grp, HEAD_DIM).transpose(0, 2, 3, 1, 4)

    def kv_heads(u):
        return u.transpose(0, 2, 1, 3)

    k_c, v_c = kv_heads(k[:, :n_ctx]), kv_heads(v[:, :n_ctx])
    o = _block_attend(q_heads(q_x), kv_heads(k_x), kv_heads(v[:, n_ctx:]), k_c, v_c)
    if need_ctx:
        o = jnp.concatenate([_dense_attend(q_heads(q[:, :n_ctx]), k_c, v_c), o], axis=3)
    return o.transpose(0, 3, 1, 2, 4).reshape(b, o.shape[3], BRANCH_W)


def _diff_branch(bq, bk, bv, g_q, g_k, lam_p, subln_g, lam_init, rope, n_ctx, need_ctx):
    b, t, _ = bq.shape
    q = _rmsnorm(bq.reshape(b, t, 2 * DIFF_HEADS, HEAD_DIM), g_q)
    k = _rmsnorm(bk.reshape(b, t, 2 * DIFF_HEADS, HEAD_DIM), g_k)
    v = bv.reshape(b, t, DIFF_HEADS, DIFF_V_DIM).transpose(0, 2, 1, 3)
    q_x = _rope_axial(q[:, n_ctx:], rope)
    k_x = _rope_axial(k[:, n_ctx:], rope)
    q_c, k_c = q[:, :n_ctx], k[:, :n_ctx]
    v_c, v_x = v[:, :, :n_ctx], v[:, :, n_ctx:]

    def sub(u, i):
        return u.reshape(b, u.shape[1], DIFF_HEADS, 2, HEAD_DIM)[:, :, :, i].transpose(0, 2, 1, 3)

    lp = lam_p.astype(jnp.float32)
    lam = (jnp.exp(jnp.sum(lp[0] * lp[1])) - jnp.exp(jnp.sum(lp[2] * lp[3])) + lam_init).astype(bv.dtype)

    def latent_map(i):
        return _block_attend(sub(q_x, i)[:, :, None], sub(k_x, i), v_x, sub(k_c, i), v_c)

    o = latent_map(0) - lam * latent_map(1)
    if need_ctx:
        o_c = (_dense_attend(sub(q_c, 0)[:, :, None], sub(k_c, 0), v_c)
               - lam * _dense_attend(sub(q_c, 1)[:, :, None], sub(k_c, 1), v_c))
        o = jnp.concatenate([o_c, o], axis=3)
    o = o[:, :, 0].transpose(0, 2, 1, 3)
    o = _rmsnorm(o, subln_g) * (1.0 - lam_init)
    return o.reshape(b, o.shape[1], BRANCH_W)


def _depthwise_conv(u, w, bias):
    pad = CONV_WIDTH // 2
    y = lax.conv_general_dilated(u, w[:, None, :].astype(u.dtype), window_strides=(1,),
                                 padding=[(pad, pad)], dimension_numbers=('NWC', 'WIO', 'NWC'),
                                 feature_group_count=u.shape[-1])
    return y + bias


def _conv_branch(cglu, w, bias, ln_g, ln_b, n_ctx, need_ctx):
    a, gate = jnp.split(cglu, 2, axis=-1)
    u = a * jax.nn.sigmoid(gate)
    y = _depthwise_conv(u[:, n_ctx:], w, bias)
    if need_ctx:
        y = jnp.concatenate([_depthwise_conv(u[:, :n_ctx], w, bias), y], axis=1)
    return jax.nn.silu(_layernorm(y, ln_g, ln_b))


def _log_forget(z, lb):
    return jnp.logaddexp(jnp.log(lb), jnp.log1p(-lb) + jax.nn.log_sigmoid(z.astype(jnp.float32)))


def _gla_chunk_scan(q, k, logf, v, s0):
    b, h, t, _ = q.shape
    n_chunk = t // HGRN_CHUNK

    def chunks(u):
        return jnp.moveaxis(u.reshape(b, h, n_chunk, HGRN_CHUNK, u.shape[-1]), 2, 0)

    causal = jnp.tril(jnp.ones((HGRN_CHUNK, HGRN_CHUNK), dtype=bool))[:, :, None]

    def step(state, inp):
        qc, kc, gc, vc = inp
        cum = jnp.cumsum(gc, axis=2)
        rel = jnp.where(causal, cum[:, :, :, None, :] - cum[:, :, None, :, :], -jnp.inf)
        att = jnp.einsum('bhtd,bhsd,bhtsd->bhts', qc, kc, jnp.exp(rel))
        o = (jnp.einsum('bhts,bhse->bhte', att, vc)
             + jnp.einsum('bhtd,bhde->bhte', qc * jnp.exp(cum), state))
        end = cum[:, :, -1:, :]
        state = (jnp.exp(end)[:, :, 0, :, None] * state
                 + jnp.einsum('bhsd,bhse->bhde', kc * jnp.exp(end - cum), vc))
        return state, o

    s_final, o = lax.scan(step, s0, (chunks(q), chunks(k), chunks(logf), chunks(v)))
    return jnp.moveaxis(o, 0, 2).reshape(b, h, t, v.shape[-1]), s_final


def _gla_final_state(k, logf, v):
    cum = jnp.cumsum(logf, axis=2)
    return jnp.einsum('bhsd,bhse->bhde', k * jnp.exp(cum[:, :, -1:] - cum), v)


def _take(u, start, stop, rev):
    u = u[:, :, start:stop]
    return jnp.flip(u, axis=2) if rev else u


def _hgrn_branch(dq, df_fwd, df_bwd, di, dg, lb_l, norm_g, n_ctx, need_ctx):
    b, t, _ = dq.shape

    def heads(u):
        return u.reshape(b, t, HGRN_HEADS, -1).transpose(0, 2, 1, 3).astype(jnp.float32)

    q = heads(jax.nn.silu(dq)) * (HGRN_DK ** -0.5)
    v = heads(di)
    outs_c, outs_x = [], []
    for d_idx, fz in enumerate((df_fwd, df_bwd)):
        rev = d_idx == 1
        logf = heads(_log_forget(fz, lb_l[d_idx]))
        k = -jnp.expm1(logf)
        ctx_in = [_take(u, 0, n_ctx, rev) for u in (q, k, logf, v)]
        lat_in = [_take(u, n_ctx, None, rev) for u in (q, k, logf, v)]
        if need_ctx:
            o_c, s_c = _gla_chunk_scan(*ctx_in, jnp.zeros((b, HGRN_HEADS, HGRN_DK, HGRN_DV), jnp.float32))
            outs_c.append(jnp.flip(o_c, axis=2) if rev else o_c)
        else:
            s_c = _gla_final_state(*ctx_in[1:])
        o_x, _ = _gla_chunk_scan(*lat_in, s_c)
        outs_x.append(jnp.flip(o_x, axis=2) if rev else o_x)
    o = outs_x[0] + outs_x[1]
    if need_ctx:
        o = jnp.concatenate([outs_c[0] + outs_c[1], o], axis=2)
    o = o.transpose(0, 2, 1, 3)
    t_out = o.shape[1]
    gate = dg[:, t - t_out:].reshape(b, t_out, HGRN_HEADS, HGRN_DV).astype(jnp.float32)
    o = _rmsnorm(o, norm_g) * jax.nn.silu(gate)
    return o.reshape(b, t_out, BRANCH_W).astype(dq.dtype)


def _token_mixer(hc, hx, layer, w_in_l, qk_g, lam_p, subln_g, conv_w_l, conv_b_l, ln_g, ln_b,
                 lb_l, hgrn_g, w_branch_l, w_out_l, rope, need_ctx):
    n_ctx = hc.shape[1]
    p = jnp.concatenate([hc, hx], axis=1) @ w_in_l
    aq, ak, av, bq, bk, bv, cglu, dq, df_fwd, df_bwd, di, dg, gates = jnp.split(p, SPLIT_POINTS, axis=-1)
    lam_init = 0.8 - 0.6 * math.exp(-0.3 * layer)
    branches = (
        _gqa_branch(aq, ak, av, qk_g[0], qk_g[1], rope, n_ctx, need_ctx),
        _diff_branch(bq, bk, bv, qk_g[2], qk_g[3], lam_p, subln_g, lam_init, rope, n_ctx, need_ctx),
        _conv_branch(cglu, conv_w_l, conv_b_l, ln_g, ln_b, n_ctx, need_ctx),
        _hgrn_branch(dq, df_fwd, df_bwd, di, dg, lb_l, hgrn_g, n_ctx, need_ctx),
    )
    start = 0 if need_ctx else n_ctx
    gate_parts = jnp.split(jax.nn.sigmoid(gates[:, start:]), N_BRANCH, axis=-1)
    merged = sum(g * (br @ w_branch_l[i]) for i, (g, br) in enumerate(zip(gate_parts, branches)))
    return merged @ w_out_l


def _moe(h, router, w1, w3, w2):
    logits = (h @ router).astype(jnp.float32)
    top_val, top_idx = lax.top_k(logits, TOP_K)
    weights = jax.nn.softmax(top_val, axis=-1)
    combine = jnp.sum(jax.nn.one_hot(top_idx, N_EXPERTS, dtype=jnp.float32) * weights[..., None],
                      axis=-2).astype(h.dtype)
    out = jnp.zeros_like(h)
    for e in range(N_EXPERTS):
        out = out + combine[..., e:e + 1] * _swiglu(h, w1[e], w3[e], w2[e])
    return out


def setup_inputs(seed: int = 0) -> dict:
    key = jax.random.key(seed)
    ks = iter(jax.random.split(key, 32))

    def nrm(shape, scale):
        return scale * jax.random.normal(next(ks), shape, jnp.float32)

    def gain(shape):
        return 1.0 + nrm(shape, 0.05)

    return {
        'x': nrm((BATCH, SEQ, D_MODEL), 1.0),
        'c': nrm((BATCH, D_MODEL), 1.0),
        'ctx': nrm((BATCH, CTX_LEN, D_MODEL), 1.0),
        'c_ctx': nrm((D_MODEL,), 1.0),
        'ada_w': nrm((DEPTH, D_MODEL, ADA_CHUNKS * D_MODEL), 0.5 * D_MODEL ** -0.5),
        'ada_b': nrm((DEPTH, ADA_CHUNKS * D_MODEL), 0.02),
        'norm1_g': gain((DEPTH, D_MODEL)),
        'norm2_g': gain((DEPTH, D_MODEL)),
        'w_in': nrm((DEPTH, D_MODEL, D_IN), D_MODEL ** -0.5),
        'qk_norm_g': gain((DEPTH, 4, HEAD_DIM)),
        'diff_lambda': nrm((DEPTH, 4, HEAD_DIM), 0.1),
        'diff_subln_g': gain((DEPTH, DIFF_V_DIM)),
        'conv_w': nrm((DEPTH, CONV_WIDTH, CONV_CH), CONV_WIDTH ** -0.5),
        'conv_b': nrm((DEPTH, CONV_CH), 0.02),
        'conv_ln_g': gain((DEPTH, CONV_CH)),
        'conv_ln_b': nrm((DEPTH, CONV_CH), 0.02),
        'hgrn_lb_logits': nrm((DEPTH, 2, HGRN_HEADS * HGRN_DK), 0.1),
        'hgrn_norm_g': gain((DEPTH, HGRN_DV)),
        'w_branch': nrm((DEPTH, N_BRANCH, BRANCH_W, D_MODEL), BRANCH_W ** -0.5),
        'w_out': nrm((DEPTH, D_MODEL, D_MODEL), D_MODEL ** -0.5),
        'ffn_w1': nrm((N_DENSE, D_MODEL, D_FF), D_MODEL ** -0.5),
        'ffn_w3': nrm((N_DENSE, D_MODEL, D_FF), D_MODEL ** -0.5),
        'ffn_w2': nrm((N_DENSE, D_FF, D_MODEL), D_FF ** -0.5),
        'moe_router': nrm((N_MOE, D_MODEL, N_EXPERTS), D_MODEL ** -0.5),
        'moe_w1': nrm((N_MOE, N_EXPERTS, D_MODEL, D_FF), D_MODEL ** -0.5),
        'moe_w3': nrm((N_MOE, N_EXPERTS, D_MODEL, D_FF), D_MODEL ** -0.5),
        'moe_w2': nrm((N_MOE, N_EXPERTS, D_FF, D_MODEL), D_FF ** -0.5),
    }


def reference(x, c, ctx, c_ctx, ada_w, ada_b, norm1_g, norm2_g, w_in, qk_norm_g, diff_lambda,
              diff_subln_g, conv_w, conv_b, conv_ln_g, conv_ln_b, hgrn_lb_logits, hgrn_norm_g,
              w_branch, w_out, ffn_w1, ffn_w3, ffn_w2, moe_router, moe_w1, moe_w3, moe_w2):
    seq = x.shape[1]
    n_ctx = ctx.shape[1]
    rows = seq // GRID_W
    rope = _axial_rope_tables(seq, rows)
    lb = jnp.cumsum(jax.nn.softmax(hgrn_lb_logits.astype(jnp.float32), axis=0), axis=0)
    lb = lb - lb[:1]
    silu_c = jax.nn.silu(c)
    silu_cc = jax.nn.silu(c_ctx)
    for l in range(DEPTH):
        need_ctx = l < DEPTH - 1
        mod_x = (silu_c @ ada_w[l] + ada_b[l])[:, None, :]
        mod_c = silu_cc @ ada_w[l] + ada_b[l]
        sh1x, sc1x, gt1x, sh2x, sc2x, gt2x = jnp.split(mod_x, ADA_CHUNKS, axis=-1)
        sh1c, sc1c, gt1c, sh2c, sc2c, gt2c = jnp.split(mod_c, ADA_CHUNKS, axis=-1)
        hx = _modulate(_rmsnorm(x, norm1_g[l]), sh1x, sc1x)
        hc = _modulate(_rmsnorm(ctx, norm1_g[l]), sh1c, sc1c)
        mix = _token_mixer(hc, hx, l, w_in[l], qk_norm_g[l], diff_lambda[l], diff_subln_g[l],
                           conv_w[l], conv_b[l], conv_ln_g[l], conv_ln_b[l], lb[l], hgrn_norm_g[l],
                           w_branch[l], w_out[l], rope, need_ctx)
        x = x + gt1x * mix[:, mix.shape[1] - seq:]
        h2 = _modulate(_rmsnorm(x, norm2_g[l]), sh2x, sc2x)
        if need_ctx:
            ctx = ctx + gt1c * mix[:, :n_ctx]
            h2 = jnp.concatenate([_modulate(_rmsnorm(ctx, norm2_g[l]), sh2c, sc2c), h2], axis=1)
        if l % 2 == 0:
            f = _swiglu(h2, ffn_w1[l // 2], ffn_w3[l // 2], ffn_w2[l // 2])
        else:
            f = _moe(h2, moe_router[l // 2], moe_w1[l // 2], moe_w3[l // 2], moe_w2[l // 2])
        x = x + gt2x * f[:, f.shape[1] - seq:]
        if need_ctx:
            ctx = ctx + gt2c * f[:, :n_ctx]
    return x
```

```python
import functools
import math

import jax
import jax.numpy as jnp
import numpy as np
from jax import lax
from jax.experimental import pallas as pl
from jax.experimental.pallas import tpu as pltpu

F32 = jnp.float32
BF16 = jnp.bfloat16

GRID_W = 64
HEAD_DIM = 64
ROPE_THETA = 10000.0
EPS = 1e-6
GQA_HEADS = 8
GQA_KV_HEADS = 2
DIFF_HEADS = 4
CONV_CH = 512
CONV_WIDTH = 31
HGRN_HEADS = 8
HGRN_DK = 64
HGRN_CHUNK = 64
BRANCH_W = 512
N_EXPERTS = 8
LOG2E = 1.4426950408889634

COL_ATTN = (0, 2304)
COL_CONV = (2304, 3328)
COL_HQ = (3328, 3840)
COL_HF = (3840, 4864)
COL_HIG = (4864, 5888)
COL_GATE = (5888, 9984)

N_MAPS = 16
N_KMAPS = 10
N_VHEADS = 6

VMEM_LIMIT_BYTES = 50 * 1024 * 1024


def _params(sem):
    return pltpu.CompilerParams(dimension_semantics=sem, vmem_limit_bytes=VMEM_LIMIT_BYTES)


def _pick(n, target, mult):
    best = None
    for d in range(mult, min(n, target) + 1, mult):
        if n % d == 0:
            best = d
    assert best is not None, (n, target, mult)
    return best


def _silu(x):
    return x * jax.nn.sigmoid(x)


def _row_is_ctx(tile_idx, tr, n_ctx):
    row = tile_idx * tr + lax.broadcasted_iota(jnp.int32, (tr, 1), 0)
    return row < n_ctx


def _ada_kernel(c_ref, w_ref, b_ref, o_ref):
    s = _silu(c_ref[...])
    o_ref[0] = jnp.dot(s.astype(BF16), w_ref[0].astype(BF16),
                       preferred_element_type=F32) + b_ref[0]


def _ada(cvec, ada_w, ada_b):
    depth, d, n = ada_w.shape
    rows = cvec.shape[0]
    tn = _pick(n, 1536, 128)
    return pl.pallas_call(
        _ada_kernel,
        grid=(depth, n // tn),
        in_specs=[pl.BlockSpec((rows, d), lambda l, j: (0, 0)),
                  pl.BlockSpec((1, d, tn), lambda l, j: (l, 0, j)),
                  pl.BlockSpec((1, 1, tn), lambda l, j: (l, 0, j))],
        out_specs=pl.BlockSpec((1, rows, tn), lambda l, j: (l, 0, j)),
        out_shape=jax.ShapeDtypeStruct((depth, rows, n), F32),
        compiler_params=_params(("arbitrary", "arbitrary")),
        name="ada",
    )(cvec, ada_w, ada_b.reshape(depth, 1, n))


def _normmod_kernel(x_ref, g_ref, shc_ref, scc_ref, shx_ref, scx_ref, o_ref, *, tr, n_ctx):
    x = x_ref[0]
    ms = jnp.mean(x * x, axis=-1, keepdims=True)
    y = x * lax.rsqrt(ms + EPS) * g_ref[...]
    is_ctx = _row_is_ctx(pl.program_id(1), tr, n_ctx)
    sh = jnp.where(is_ctx, shc_ref[...], shx_ref[0])
    sc = jnp.where(is_ctx, scc_ref[...], scx_ref[0])
    o_ref[0] = (y * (1.0 + sc) + sh).astype(o_ref.dtype)


def _normmod(x, g, sh_c, sc_c, sh_x, sc_x, n_ctx, out_dtype=BF16):
    b, t, d = x.shape
    tr = _pick(t, 1088, 16)
    vec_c = pl.BlockSpec((1, d), lambda i, j: (0, 0))
    vec_x = pl.BlockSpec((1, 1, d), lambda i, j: (i, 0, 0))
    return pl.pallas_call(
        functools.partial(_normmod_kernel, tr=tr, n_ctx=n_ctx),
        grid=(b, t // tr),
        in_specs=[pl.BlockSpec((1, tr, d), lambda i, j: (i, j, 0)),
                  vec_c, vec_c, vec_c, vec_x, vec_x],
        out_specs=pl.BlockSpec((1, tr, d), lambda i, j: (i, j, 0)),
        out_shape=jax.ShapeDtypeStruct((b, t, d), out_dtype),
        compiler_params=_params(("arbitrary", "arbitrary")),
        name="normmod",
    )(x, g.reshape(1, d), sh_c, sc_c, sh_x, sc_x)


def _mm_kernel(a_ref, w_ref, o_ref):
    o_ref[...] = jnp.dot(a_ref[...], w_ref[...],
                         preferred_element_type=F32).astype(o_ref.dtype)


def _matmul(a, w, out_dtype):
    m, k = a.shape
    n = w.shape[1]
    tm = _pick(m, 2048, 256)
    tn = _pick(n, 1024, 128)
    return pl.pallas_call(
        _mm_kernel,
        grid=(m // tm, n // tn),
        in_specs=[pl.BlockSpec((tm, k), lambda i, j: (i, 0)),
                  pl.BlockSpec((k, tn), lambda i, j: (0, j))],
        out_specs=pl.BlockSpec((tm, tn), lambda i, j: (i, j)),
        out_shape=jax.ShapeDtypeStruct((m, n), out_dtype),
        compiler_params=_params(("arbitrary", "arbitrary")),
        name="proj",
    )(a, w)


def _prep_kernel(p_ref, cos_ref, sin_ref, g_ref, j_ref, q_ref, k_ref, v_ref):
    cos = cos_ref[...]
    sin = sin_ref[...]
    lane = lax.broadcasted_iota(jnp.int32, (1, 128), 1)
    first_half = (lane % 32) < 16
    low_head = lane < HEAD_DIM

    def norm_rope(col, gi):
        x = p_ref[0, :, col:col + 128].astype(F32)
        ms = jnp.dot((x * x).astype(BF16), j_ref[...], preferred_element_type=F32)
        xn = x * lax.rsqrt(ms + EPS) * g_ref[gi:gi + 1, :]
        swapped = jnp.where(first_half, pltpu.roll(xn, 112, 1), pltpu.roll(xn, 16, 1))
        return xn * cos + swapped * sin

    def put_heads(ref, first, y):
        ref[0, first] = y[:, :HEAD_DIM].astype(ref.dtype)
        ref[0, first + 1] = y[:, HEAD_DIM:].astype(ref.dtype)

    for g in range(4):
        put_heads(q_ref, 2 * g, norm_rope(128 * g, 0))
    put_heads(k_ref, 0, norm_rope(512, 1))
    for g in range(4):
        put_heads(q_ref, 8 + 2 * g, norm_rope(768 + 128 * g, 2))
    for g in range(4):
        put_heads(k_ref, 2 + 2 * g, norm_rope(1280 + 128 * g, 3))
    av = p_ref[0, :, 640:768].astype(F32)
    v_ref[0, 0] = jnp.where(low_head, av, 0.0).astype(v_ref.dtype)
    v_ref[0, 1] = jnp.where(low_head, pltpu.roll(av, 64, 1), 0.0).astype(v_ref.dtype)
    for h in range(DIFF_HEADS):
        v_ref[0, 2 + h] = p_ref[0, :, 1792 + 128 * h:1792 + 128 * (h + 1)]


def _attn_prep(p_attn, cos, sin, gains, jmat):
    b, t, w = p_attn.shape
    tr = _pick(t, 544, 16)
    return pl.pallas_call(
        _prep_kernel,
        grid=(b, t // tr),
        in_specs=[pl.BlockSpec((1, tr, w), lambda i, j: (i, j, 0)),
                  pl.BlockSpec((tr, 128), lambda i, j: (j, 0)),
                  pl.BlockSpec((tr, 128), lambda i, j: (j, 0)),
                  pl.BlockSpec((4, 128), lambda i, j: (0, 0)),
                  pl.BlockSpec((128, 128), lambda i, j: (0, 0))],
        out_specs=[pl.BlockSpec((1, N_MAPS, tr, HEAD_DIM), lambda i, j: (i, 0, j, 0)),
                   pl.BlockSpec((1, N_KMAPS, tr, HEAD_DIM), lambda i, j: (i, 0, j, 0)),
                   pl.BlockSpec((1, N_VHEADS, tr, 128), lambda i, j: (i, 0, j, 0))],
        out_shape=[jax.ShapeDtypeStruct((b, N_MAPS, t, HEAD_DIM), BF16),
                   jax.ShapeDtypeStruct((b, N_KMAPS, t, HEAD_DIM), BF16),
                   jax.ShapeDtypeStruct((b, N_VHEADS, t, 128), BF16)],
        compiler_params=_params(("arbitrary", "arbitrary")),
        name="attn_prep",
    )(p_attn, cos, sin, gains, jmat)


def _attn_kernel(q_ref, k_ref, v_ref, o_ref, s_scr, *, tq, tk, n_ctx, t):
    q = q_ref[0, 0]

    def attend(nk):
        nchunk = nk // tk
        m_part = jnp.full((tq, 128), -jnp.inf, F32)
        for c in range(nchunk):
            kc = k_ref[0, 0, c * tk:(c + 1) * tk, :]
            s = lax.dot_general(q, kc, (((1,), (1,)), ((), ())), preferred_element_type=F32)
            s_scr[:, c * tk:(c + 1) * tk] = s
            for u in range(tk // 128):
                m_part = jnp.maximum(m_part, s[:, u * 128:(u + 1) * 128])
        m = jnp.max(m_part, axis=-1, keepdims=True)
        l_part = jnp.zeros((tq, 128), F32)
        acc = jnp.zeros((tq, 128), F32)
        for c in range(nchunk):
            p = jnp.exp2(s_scr[:, c * tk:(c + 1) * tk] - m)
            for u in range(tk // 128):
                l_part = l_part + p[:, u * 128:(u + 1) * 128]
            acc = acc + jnp.dot(p.astype(BF16), v_ref[0, 0, c * tk:(c + 1) * tk, :],
                                preferred_element_type=F32)
        l = jnp.sum(l_part, axis=-1, keepdims=True)
        o_ref[0, 0] = acc / l

    is_ctx_tile = pl.program_id(2) * tq < n_ctx

    @pl.when(is_ctx_tile)
    def _():
        attend(n_ctx)

    @pl.when(jnp.logical_not(is_ctx_tile))
    def _():
        attend(t)


def _attention(q, k, v, n_ctx):
    b, _, t, _ = q.shape
    tq = 256
    tk = 256
    assert n_ctx % tq == 0 and t % tq == 0 and n_ctx % tk == 0 and t % tk == 0

    def kmap(i, u, j):
        return (i, jnp.where(u < GQA_HEADS, u // (GQA_HEADS // GQA_KV_HEADS), u - 6), 0, 0)

    def vmap(i, u, j):
        return (i, jnp.where(u < GQA_HEADS, u // (GQA_HEADS // GQA_KV_HEADS), 2 + (u - 8) // 2), 0, 0)

    return pl.pallas_call(
        functools.partial(_attn_kernel, tq=tq, tk=tk, n_ctx=n_ctx, t=t),
        grid=(b, N_MAPS, t // tq),
        in_specs=[pl.BlockSpec((1, 1, tq, HEAD_DIM), lambda i, u, j: (i, u, j, 0)),
                  pl.BlockSpec((1, 1, t, HEAD_DIM), kmap),
                  pl.BlockSpec((1, 1, t, 128), vmap)],
        out_specs=pl.BlockSpec((1, 1, tq, 128), lambda i, u, j: (i, u, j, 0)),
        out_shape=jax.ShapeDtypeStruct((b, N_MAPS, t, 128), F32),
        scratch_shapes=[pltpu.VMEM((tq, t), F32)],
        compiler_params=_params(("arbitrary", "arbitrary", "arbitrary")),
        name="attention",
    )(q, k, v)


CONV_TILE = 256
CONV_HALO = 16
CONV_ROWS = 32


def _conv_kernel(main_ref, left_ref, right_ref, w_ref, b_ref, lng_ref, lnb_ref, o_ref,
                 u_scr, *, n_ctx, t):
    j = pl.program_id(1)
    row0 = j * CONV_TILE
    has_left = jnp.logical_and(row0 != 0, row0 != n_ctx)
    has_right = jnp.logical_and(row0 + CONV_TILE != n_ctx, row0 + CONV_TILE != t)

    def glu(ref):
        x = ref[0].astype(F32)
        return x[:, :CONV_CH] * jax.nn.sigmoid(x[:, CONV_CH:])

    left = jnp.where(has_left, glu(left_ref), 0.0)
    right = jnp.where(has_right, glu(right_ref), 0.0)
    upad = jnp.concatenate([left, glu(main_ref), right], axis=0)
    n_pad = CONV_TILE + 2 * CONV_HALO
    for r in range(8):
        u_scr[r] = upad if r == 0 else pltpu.roll(upad, n_pad - r, 0)

    pad = CONV_WIDTH // 2

    def rows_body(i, carry):
        base = pl.multiple_of(i * CONV_ROWS, CONV_ROWS)
        acc = jnp.zeros((CONV_ROWS, CONV_CH), F32)
        for tap in range(CONV_WIDTH):
            off = tap + CONV_HALO - pad
            a, r = off // 8, off % 8
            acc = acc + w_ref[tap:tap + 1, :] * u_scr[r, pl.ds(base + 8 * a, CONV_ROWS), :]
        y = acc + b_ref[...]
        mu = jnp.mean(y, axis=-1, keepdims=True)
        yc = y - mu
        var = jnp.mean(yc * yc, axis=-1, keepdims=True)
        z = yc * lax.rsqrt(var + EPS) * lng_ref[...] + lnb_ref[...]
        o_ref[0, pl.ds(base, CONV_ROWS), :] = _silu(z).astype(o_ref.dtype)
        return carry

    lax.fori_loop(0, CONV_TILE // CONV_ROWS, rows_body, 0)


def _conv_branch(p_conv, w, bias, ln_g, ln_b, n_ctx):
    b, t, wd = p_conv.shape
    assert n_ctx % CONV_TILE == 0 and t % CONV_TILE == 0
    hb = CONV_TILE // CONV_HALO
    n_halo_blocks = t // CONV_HALO
    vec = pl.BlockSpec((1, CONV_CH), lambda i, j: (0, 0))
    return pl.pallas_call(
        functools.partial(_conv_kernel, n_ctx=n_ctx, t=t),
        grid=(b, t // CONV_TILE),
        in_specs=[pl.BlockSpec((1, CONV_TILE, wd), lambda i, j: (i, j, 0)),
                  pl.BlockSpec((1, CONV_HALO, wd),
                               lambda i, j: (i, jnp.maximum(j * hb - 1, 0), 0)),
                  pl.BlockSpec((1, CONV_HALO, wd),
                               lambda i, j: (i, jnp.minimum((j + 1) * hb, n_halo_blocks - 1), 0)),
                  pl.BlockSpec((CONV_WIDTH, CONV_CH), lambda i, j: (0, 0)),
                  vec, vec, vec],
        out_specs=pl.BlockSpec((1, CONV_TILE, CONV_CH), lambda i, j: (i, j, 0)),
        out_shape=jax.ShapeDtypeStruct((b, t, CONV_CH), BF16),
        scratch_shapes=[pltpu.VMEM((8, CONV_TILE + 2 * CONV_HALO, CONV_CH), F32)],
        compiler_params=_params(("arbitrary", "arbitrary")),
        name="conv",
    )(p_conv, p_conv, p_conv, w, bias.reshape(1, -1), ln_g.reshape(1, -1), ln_b.reshape(1, -1))


HG_TILE = 256
HG_W = HGRN_HEADS * HGRN_DK


def _dot3(a_bf16, x):
    x1 = x.astype(BF16)
    r1 = x - x1.astype(F32)
    x2 = r1.astype(BF16)
    x3 = (r1 - x2.astype(F32)).astype(BF16)
    out = jnp.dot(a_bf16, x1, preferred_element_type=F32)
    out = out + jnp.dot(a_bf16, x2, preferred_element_type=F32)
    return out + jnp.dot(a_bf16, x3, preferred_element_type=F32)


def _hgrn_kernel(q_ref, z_ref, v_ref, la_ref, lb0_ref, tri_ref, j_ref, bd_ref, o_ref,
                 st_scr, q_scr, k_scr, c_scr, v_scr, ks_scr, cs_scr, vs_scr, acc_scr, *, rev):
    c_sz = HGRN_CHUNK
    n_chunk = HG_TILE // c_sz

    @pl.when(pl.program_id(1) == 0)
    def _():
        st_scr[...] = jnp.zeros_like(st_scr)

    q = _silu(q_ref[0].astype(F32)) * (HGRN_DK ** -0.5)
    z = z_ref[0]
    v = v_ref[0].astype(F32)
    log_sig = jnp.minimum(z, 0.0) - jnp.log(1.0 + jnp.exp(-jnp.abs(z)))
    t1 = la_ref[...]
    t2 = lb0_ref[...] + log_sig
    logf = jnp.maximum(t1, t2) + jnp.log(1.0 + jnp.exp(-jnp.abs(t1 - t2)))
    k = 1.0 - jnp.exp(logf)
    c2 = _dot3(tri_ref[...], logf * LOG2E)
    q_scr[...] = q
    k_scr[...] = k
    c_scr[...] = c2
    v_scr[...] = v

    order = range(n_chunk - 1, -1, -1) if rev else range(n_chunk)
    for ci in order:
        lo = ci * c_sz
        cj = c_scr[lo:lo + c_sz, :]
        c_end = cj[0:1] if rev else cj[c_sz - 1:c_sz]
        qd = (q_scr[lo:lo + c_sz, :] * jnp.exp2(cj)).astype(BF16)
        st = st_scr[...]
        acc_scr[lo:lo + c_sz, :] = lax.dot_general(
            qd, st.astype(BF16), (((1,), (1,)), ((), ())), preferred_element_type=F32)
        kd = (k_scr[lo:lo + c_sz, :] * jnp.exp2(c_end - cj)).astype(BF16)
        vb = v_scr[lo:lo + c_sz, :].astype(BF16)
        upd = lax.dot_general(vb, kd, (((0,), (0,)), ((), ())),
                              preferred_element_type=F32)
        st_scr[...] = (st * jnp.exp2(c_end) + upd) * bd_ref[...]

    row_in_chunk = lax.broadcasted_iota(jnp.int32, (HG_TILE, 1), 0) % c_sz

    def shifted(ref, r):
        amt = (HG_TILE - r) % HG_TILE if rev else r
        y = pltpu.roll(ref[...], amt, 0)
        keep = (row_in_chunk < c_sz - r) if rev else (row_in_chunk >= r)
        return jnp.where(keep, y, 0.0)

    def r_body(r, carry):
        ks_scr[...] = shifted(k_scr, r)
        cs_scr[...] = shifted(c_scr, r)
        vs_scr[...] = shifted(v_scr, r)
        for a in range(c_sz // 8):
            n = c_sz - 8 * a
            d_lo = 0 if rev else 8 * a
            s_lo = 8 * a if rev else 0
            def rows(ref, lo_in_chunk):
                return jnp.concatenate(
                    [ref[ci * c_sz + lo_in_chunk:ci * c_sz + lo_in_chunk + n, :]
                     for ci in range(n_chunk)], axis=0)
            qa = rows(q_scr, d_lo)
            ca = rows(c_scr, d_lo)
            kk = rows(ks_scr, s_lo)
            cc = rows(cs_scr, s_lo)
            vv = rows(vs_scr, s_lo)
            pw = (qa * kk * jnp.exp2(ca - cc)).astype(BF16)
            att = jnp.dot(pw, j_ref[...], preferred_element_type=F32)
            contrib = att * vv
            for ci in range(n_chunk):
                dst = ci * c_sz + d_lo
                acc_scr[dst:dst + n, :] = acc_scr[dst:dst + n, :] + contrib[ci * n:(ci + 1) * n]
        return carry

    lax.fori_loop(0, 8, r_body, 0)
    o_ref[0] = acc_scr[...]


def _hgrn_dir(p_hq, p_hf, p_hig, la, lb0, tri, jmat, bd, n_ctx, rev):
    b, t, _ = p_hq.shape
    assert n_ctx % HG_TILE == 0 and t % HG_TILE == 0
    n_tiles = t // HG_TILE
    n_ctx_tiles = n_ctx // HG_TILE
    col = 1 if rev else 0

    def tile(j):
        if not rev:
            return j
        return jnp.where(j < n_ctx_tiles, n_ctx_tiles - 1 - j, n_tiles - 1 - (j - n_ctx_tiles))

    const = lambda shape: pl.BlockSpec(shape, lambda i, j: (0,) * len(shape))
    return pl.pallas_call(
        functools.partial(_hgrn_kernel, rev=rev),
        grid=(b, n_tiles),
        in_specs=[pl.BlockSpec((1, HG_TILE, HG_W), lambda i, j: (i, tile(j), 0)),
                  pl.BlockSpec((1, HG_TILE, HG_W), lambda i, j: (i, tile(j), col)),
                  pl.BlockSpec((1, HG_TILE, HG_W), lambda i, j: (i, tile(j), 0)),
                  const((1, HG_W)),
                  const((1, HG_W)),
                  const((HG_TILE, HG_TILE)), const((HG_W, HG_W)), const((HG_W, HG_W))],
        out_specs=pl.BlockSpec((1, HG_TILE, HG_W), lambda i, j: (i, tile(j), 0)),
        out_shape=jax.ShapeDtypeStruct((b, t, HG_W), F32),
        scratch_shapes=[pltpu.VMEM((HG_W, HG_W), F32)]
                       + [pltpu.VMEM((HG_TILE, HG_W), F32) for _ in range(8)],
        compiler_params=_params(("arbitrary", "arbitrary")),
        name="hgrn_bwd" if rev else "hgrn_fwd",
    )(p_hq, p_hf, p_hig, la[col:col + 1], lb0[col:col + 1], tri, jmat, bd)


def _merge_kernel(oa_ref, cv_ref, hf_ref, hb_ref, hg_ref, gt_ref, x_ref, wb_ref, wo_ref,
                  sub_ref, hng_ref, j_ref, lam_ref, gc_ref, gx_ref, o_ref, *, tr, n_ctx, lam_init):
    lam = lam_ref[0]
    gates = gt_ref[0]

    def gate(i):
        return jax.nn.sigmoid(gates[:, i * 1024:(i + 1) * 1024].astype(F32))

    pa = jnp.zeros((tr, 1024), F32)
    for h in range(GQA_HEADS):
        pa = pa + jnp.dot(oa_ref[0, h][:, :HEAD_DIM].astype(BF16),
                          wb_ref[0, h * HEAD_DIM:(h + 1) * HEAD_DIM, :],
                          preferred_element_type=F32)
    merged = gate(0) * pa
    pb = jnp.zeros((tr, 1024), F32)
    for h in range(DIFF_HEADS):
        od = oa_ref[0, 8 + 2 * h] - lam * oa_ref[0, 9 + 2 * h]
        ms = jnp.mean(od * od, axis=-1, keepdims=True)
        on = od * lax.rsqrt(ms + EPS) * sub_ref[...] * (1.0 - lam_init)
        pb = pb + jnp.dot(on.astype(BF16), wb_ref[1, h * 128:(h + 1) * 128, :],
                          preferred_element_type=F32)
    merged = merged + gate(1) * pb
    merged = merged + gate(2) * jnp.dot(cv_ref[0], wb_ref[2], preferred_element_type=F32)
    od = hf_ref[0] + hb_ref[0]
    ms = jnp.dot((od * od).astype(BF16), j_ref[...], preferred_element_type=F32)
    on = od * lax.rsqrt(ms + EPS) * hng_ref[...] * _silu(hg_ref[0].astype(F32))
    merged = merged + gate(3) * jnp.dot(on.astype(BF16), wb_ref[3], preferred_element_type=F32)

    mix = jnp.dot(merged.astype(BF16), wo_ref[...], preferred_element_type=F32)
    g1 = jnp.where(_row_is_ctx(pl.program_id(1), tr, n_ctx), gc_ref[...], gx_ref[0])
    o_ref[0] = x_ref[0] + g1 * mix


def _merge(o_attn, conv_o, hg_f, hg_b, p_hig, p_gate, x, wb, wo, subln_g, hgrn_g, jmat64, lam,
           g_c, g_x, n_ctx, lam_init):
    b, t, d = x.shape
    tr = _pick(t, 272, 16)
    const = lambda shape: pl.BlockSpec(shape, lambda i, j: (0,) * len(shape))
    return pl.pallas_call(
        functools.partial(_merge_kernel, tr=tr, n_ctx=n_ctx, lam_init=lam_init),
        grid=(b, t // tr),
        in_specs=[pl.BlockSpec((1, N_MAPS, tr, 128), lambda i, j: (i, 0, j, 0)),
                  pl.BlockSpec((1, tr, CONV_CH), lambda i, j: (i, j, 0)),
                  pl.BlockSpec((1, tr, HG_W), lambda i, j: (i, j, 0)),
                  pl.BlockSpec((1, tr, HG_W), lambda i, j: (i, j, 0)),
                  pl.BlockSpec((1, tr, HG_W), lambda i, j: (i, j, 1)),
                  pl.BlockSpec((1, tr, 4 * d), lambda i, j: (i, j, 0)),
                  pl.BlockSpec((1, tr, d), lambda i, j: (i, j, 0)),
                  const((4, BRANCH_W, d)), const((d, d)),
                  const((1, 128)), const((1, HG_W)), const((HG_W, HG_W)),
                  pl.BlockSpec(memory_space=pltpu.SMEM),
                  const((1, d)),
                  pl.BlockSpec((1, 1, d), lambda i, j: (i, 0, 0))],
        out_specs=pl.BlockSpec((1, tr, d), lambda i, j: (i, j, 0)),
        out_shape=jax.ShapeDtypeStruct((b, t, d), F32),
        compiler_params=_params(("arbitrary", "arbitrary")),
        name="merge",
    )(o_attn, conv_o, hg_f, hg_b, p_hig, p_gate, x, wb, wo, subln_g, hgrn_g, jmat64, lam,
      g_c, g_x)


def _ffn_kernel(*refs, tr, n_ctx, weighted):
    if weighted:
        h_ref, w1_ref, w3_ref, w2_ref, x_ref, gc_ref, gx_ref, cw_ref, o_ref, acc_scr = refs
    else:
        h_ref, w1_ref, w3_ref, w2_ref, x_ref, gc_ref, gx_ref, o_ref, acc_scr = refs
    kf = pl.program_id(2)

    @pl.when(kf == 0)
    def _():
        acc_scr[...] = jnp.zeros_like(acc_scr)

    h = h_ref[0]
    a = jnp.dot(h, w1_ref[...], preferred_element_type=F32)
    g = jnp.dot(h, w3_ref[...], preferred_element_type=F32)
    acc_scr[...] += jnp.dot((_silu(a) * g).astype(BF16), w2_ref[...],
                            preferred_element_type=F32)

    @pl.when(kf == pl.num_programs(2) - 1)
    def _():
        g2 = jnp.where(_row_is_ctx(pl.program_id(1), tr, n_ctx), gc_ref[...], gx_ref[0])
        f = acc_scr[...]
        if weighted:
            f = f * cw_ref[0]
        o_ref[0] = x_ref[0] + g2 * f


def _ffn(h, w1, w3, w2, x, g_c, g_x, n_ctx, cw=None):
    b, t, d = x.shape
    dff = w1.shape[1]
    tr = _pick(t, 1088, 16)
    tf = _pick(dff, 512, 128)
    weighted = cw is not None
    in_specs = [pl.BlockSpec((1, tr, d), lambda i, j, f: (i, j, 0)),
                pl.BlockSpec((d, tf), lambda i, j, f: (0, f)),
                pl.BlockSpec((d, tf), lambda i, j, f: (0, f)),
                pl.BlockSpec((tf, d), lambda i, j, f: (f, 0)),
                pl.BlockSpec((1, tr, d), lambda i, j, f: (i, j, 0)),
                pl.BlockSpec((1, d), lambda i, j, f: (0, 0)),
                pl.BlockSpec((1, 1, d), lambda i, j, f: (i, 0, 0))]
    args = [h, w1, w3, w2, x, g_c, g_x]
    if weighted:
        in_specs.append(pl.BlockSpec((1, tr, 1), lambda i, j, f: (i, j, 0)))
        args.append(cw)
    return pl.pallas_call(
        functools.partial(_ffn_kernel, tr=tr, n_ctx=n_ctx, weighted=weighted),
        grid=(b, t // tr, dff // tf),
        in_specs=in_specs,
        out_specs=pl.BlockSpec((1, tr, d), lambda i, j, f: (i, j, 0)),
        out_shape=jax.ShapeDtypeStruct((b, t, d), F32),
        scratch_shapes=[pltpu.VMEM((tr, d), F32)],
        input_output_aliases={4: 0},
        compiler_params=_params(("arbitrary", "arbitrary", "arbitrary")),
        name="ffn",
    )(*args)


def _router_kernel(h_ref, r_ref, o_ref):
    logits = jnp.dot(h_ref[0], r_ref[...], preferred_element_type=F32,
                     precision=lax.Precision.HIGHEST)
    lane = lax.broadcasted_iota(jnp.int32, logits.shape, 1).astype(F32)
    neg = -jnp.inf
    logits = jnp.where(lane < N_EXPERTS, logits, neg)
    m1 = jnp.max(logits, axis=-1, keepdims=True)
    i1 = jnp.min(jnp.where(logits == m1, lane, 128.0), axis=-1, keepdims=True)
    rest = jnp.where(lane == i1, neg, logits)
    m2 = jnp.max(rest, axis=-1, keepdims=True)
    i2 = jnp.min(jnp.where(rest == m2, lane, 128.0), axis=-1, keepdims=True)
    e2 = jnp.exp(m2 - m1)
    w1 = 1.0 / (1.0 + e2)
    w2 = e2 / (1.0 + e2)
    o_ref[0] = jnp.where(lane == i1, w1, 0.0) + jnp.where(lane == i2, w2, 0.0)


def _router(h_f32, router_pad):
    b, t, d = h_f32.shape
    tr = _pick(t, 1088, 16)
    return pl.pallas_call(
        _router_kernel,
        grid=(b, t // tr),
        in_specs=[pl.BlockSpec((1, tr, d), lambda i, j: (i, j, 0)),
                  pl.BlockSpec((d, 128), lambda i, j: (0, 0))],
        out_specs=pl.BlockSpec((1, tr, 128), lambda i, j: (i, j, 0)),
        out_shape=jax.ShapeDtypeStruct((b, t, 128), F32),
        compiler_params=_params(("arbitrary", "arbitrary")),
        name="router",
    )(h_f32, router_pad)


def _rope_tables(seq, n_ctx):
    rows = seq // GRID_W
    row = jnp.repeat(jnp.arange(rows, dtype=F32), GRID_W)
    col = (jnp.arange(seq) % GRID_W).astype(F32)
    axis_dim = HEAD_DIM // 2
    inv_freq = ROPE_THETA ** (-jnp.arange(0, axis_dim, 2, dtype=F32) / axis_dim)
    ang_r = row[:, None] * inv_freq
    ang_c = col[:, None] * inv_freq
    cos = jnp.concatenate([jnp.cos(ang_r), jnp.cos(ang_r), jnp.cos(ang_c), jnp.cos(ang_c)], axis=-1)
    sin = jnp.concatenate([-jnp.sin(ang_r), jnp.sin(ang_r), -jnp.sin(ang_c), jnp.sin(ang_c)], axis=-1)
    cos = jnp.concatenate([jnp.ones((n_ctx, HEAD_DIM), F32), cos], axis=0)
    sin = jnp.concatenate([jnp.zeros((n_ctx, HEAD_DIM), F32), sin], axis=0)
    return jnp.tile(cos, (1, 2)), jnp.tile(sin, (1, 2))


def _block_ones(n, blk, value=1.0, dtype=BF16):
    i = np.arange(n) // blk
    return jnp.asarray((i[:, None] == i[None, :]).astype(np.float32) * value, dtype)


def _chunk_tri(n, chunk, rev):
    i = np.arange(n)
    same = (i[:, None] // chunk) == (i[None, :] // chunk)
    tri = (i[None, :] >= i[:, None]) if rev else (i[None, :] <= i[:, None])
    return jnp.asarray((same & tri).astype(np.float32), BF16)


def kernel(x, c, ctx, c_ctx, ada_w, ada_b, norm1_g, norm2_g, w_in, qk_norm_g, diff_lambda,
           diff_subln_g, conv_w, conv_b, conv_ln_g, conv_ln_b, hgrn_lb_logits, hgrn_norm_g,
           w_branch, w_out, ffn_w1, ffn_w3, ffn_w2, moe_router, moe_w1, moe_w3, moe_w2):
    b, seq, d = x.shape
    n_ctx = ctx.shape[1]
    t = n_ctx + seq
    depth = ada_w.shape[0]

    cos, sin = _rope_tables(seq, n_ctx)
    j_head_mean = _block_ones(128, HEAD_DIM, 1.0 / HEAD_DIM)
    j_hg_sum = _block_ones(HG_W, HGRN_DK)
    j_hg_mean = _block_ones(HG_W, HGRN_DK, 1.0 / HGRN_DK)
    bd_mask = _block_ones(HG_W, HGRN_DK, 1.0, F32)
    tri_f = _chunk_tri(HG_TILE, HGRN_CHUNK, False)
    tri_b = _chunk_tri(HG_TILE, HGRN_CHUNK, True)

    lb = jnp.cumsum(jax.nn.softmax(hgrn_lb_logits.astype(F32), axis=0), axis=0)
    lb = lb - lb[:1]
    log_lb = jnp.log(lb)
    log_1m_lb = jnp.log1p(-lb)

    rows = ((b + 1 + 7) // 8) * 8
    cvec = jnp.zeros((rows, d), F32).at[:b].set(c).at[b].set(c_ctx)
    mod = _ada(cvec, ada_w, ada_b).reshape(depth, rows, 6, d)

    xc = jnp.concatenate([ctx, x], axis=1)
    q_scale = LOG2E * HEAD_DIM ** -0.5

    for l in range(depth):
        mx = mod[l, :b]
        mc = mod[l, b]
        vx = lambda i: mx[:, i:i + 1, :]
        vc = lambda i: mc[i:i + 1, :]
        w_l = w_in[l].astype(BF16)

        h = _normmod(xc, norm1_g[l], vc(0), vc(1), vx(0), vx(1), n_ctx).reshape(b * t, d)
        proj = lambda cols, dt: _matmul(h, w_l[:, cols[0]:cols[1]], dt).reshape(b, t, -1)
        p_attn = proj(COL_ATTN, BF16)
        p_conv = proj(COL_CONV, BF16)
        p_hq = proj(COL_HQ, BF16)
        p_hf = proj(COL_HF, F32)
        p_hig = proj(COL_HIG, BF16)
        p_gate = proj(COL_GATE, BF16)

        g = qk_norm_g[l].astype(F32)
        gains = jnp.tile(jnp.stack([g[0] * q_scale, g[1], g[2] * q_scale, g[3]]), (1, 2))
        q_h, k_h, v_h = _attn_prep(p_attn, cos, sin, gains, j_head_mean)
        o_attn = _attention(q_h, k_h, v_h, n_ctx)

        conv_o = _conv_branch(p_conv, conv_w[l], conv_b[l], conv_ln_g[l], conv_ln_b[l], n_ctx)

        hg_f = _hgrn_dir(p_hq, p_hf, p_hig, log_lb[l], log_1m_lb[l], tri_f, j_hg_sum, bd_mask,
                         n_ctx, False)
        hg_b = _hgrn_dir(p_hq, p_hf, p_hig, log_lb[l], log_1m_lb[l], tri_b, j_hg_sum, bd_mask,
                         n_ctx, True)

        lam_init = 0.8 - 0.6 * math.exp(-0.3 * l)
        lp = diff_lambda[l].astype(F32)
        lam = (jnp.exp(jnp.sum(lp[0] * lp[1])) - jnp.exp(jnp.sum(lp[2] * lp[3]))
               + lam_init).reshape(1)
        xc = _merge(o_attn, conv_o, hg_f, hg_b, p_hig, p_gate, xc,
                    w_branch[l].astype(BF16), w_out[l].astype(BF16),
                    diff_subln_g[l].reshape(1, -1), jnp.tile(hgrn_norm_g[l], HGRN_HEADS).reshape(1, -1),
                    j_hg_mean, lam, vc(2), vx(2), n_ctx, lam_init)

        if l % 2 == 0:
            h2 = _normmod(xc, norm2_g[l], vc(3), vc(4), vx(3), vx(4), n_ctx)
            i = l // 2
            xc = _ffn(h2, ffn_w1[i].astype(BF16), ffn_w3[i].astype(BF16), ffn_w2[i].astype(BF16),
                      xc, vc(5), vx(5), n_ctx)
        else:
            h2f = _normmod(xc, norm2_g[l], vc(3), vc(4), vx(3), vx(4), n_ctx, out_dtype=F32)
            i = l // 2
            router_pad = jnp.zeros((d, 128), F32).at[:, :N_EXPERTS].set(moe_router[i])
            combine = _router(h2f, router_pad)
            h2 = h2f.astype(BF16)
            for e in range(N_EXPERTS):
                xc = _ffn(h2, moe_w1[i, e].astype(BF16), moe_w3[i, e].astype(BF16),
                          moe_w2[i, e].astype(BF16), xc, vc(5), vx(5), n_ctx,
                          cw=combine[:, :, e:e + 1])
    return xc[:, n_ctx:]
```

```python
import functools
import math

import jax
import jax.numpy as jnp
import numpy as np
from jax import lax
from jax.experimental import pallas as pl
from jax.experimental.pallas import tpu as pltpu

F32 = jnp.float32
BF16 = jnp.bfloat16

GRID_W = 64
HEAD_DIM = 64
ROPE_THETA = 10000.0
EPS = 1e-6
GQA_HEADS = 8
GQA_KV_HEADS = 2
DIFF_HEADS = 4
CONV_CH = 512
CONV_WIDTH = 31
HGRN_HEADS = 8
HGRN_DK = 64
HGRN_CHUNK = 64
BRANCH_W = 512
N_EXPERTS = 8
LOG2E = 1.4426950408889634

COL_ATTN = (0, 2304)
COL_CONV = (2304, 3328)
COL_HQ = (3328, 3840)
COL_HF = (3840, 4864)
COL_HIG = (4864, 5888)
COL_GATE = (5888, 9984)

VMEM_LIMIT_BYTES = 50 * 1024 * 1024


def _params(sem):
    return pltpu.CompilerParams(dimension_semantics=sem, vmem_limit_bytes=VMEM_LIMIT_BYTES)


def _pick(n, target, mult):
    best = None
    for d in range(mult, min(n, target) + 1, mult):
        if n % d == 0:
            best = d
    assert best is not None, (n, target, mult)
    return best


def _silu(x):
    return x * jax.nn.sigmoid(x)


def _row_is_ctx(tile_idx, tr, n_ctx):
    row = tile_idx * tr + lax.broadcasted_iota(jnp.int32, (tr, 1), 0)
    return row < n_ctx


def _ada_kernel(c_ref, w_ref, b_ref, o_ref):
    s = _silu(c_ref[...])
    o_ref[0] = jnp.dot(s.astype(BF16), w_ref[0].astype(BF16),
                       preferred_element_type=F32) + b_ref[0]


def _ada(cvec, ada_w, ada_b):
    depth, d, n = ada_w.shape
    rows = cvec.shape[0]
    tn = _pick(n, 1536, 128)
    return pl.pallas_call(
        _ada_kernel,
        grid=(depth, n // tn),
        in_specs=[pl.BlockSpec((rows, d), lambda l, j: (0, 0)),
                  pl.BlockSpec((1, d, tn), lambda l, j: (l, 0, j)),
                  pl.BlockSpec((1, 1, tn), lambda l, j: (l, 0, j))],
        out_specs=pl.BlockSpec((1, rows, tn), lambda l, j: (l, 0, j)),
        out_shape=jax.ShapeDtypeStruct((depth, rows, n), F32),
        compiler_params=_params(("arbitrary", "arbitrary")),
        name="ada",
    )(cvec, ada_w, ada_b.reshape(depth, 1, n))


def _normmod_kernel(x_ref, g_ref, shc_ref, scc_ref, shx_ref, scx_ref, o_ref, *, tr, n_ctx):
    x = x_ref[0]
    ms = jnp.mean(x * x, axis=-1, keepdims=True)
    y = x * lax.rsqrt(ms + EPS) * g_ref[...]
    is_ctx = _row_is_ctx(pl.program_id(1), tr, n_ctx)
    sh = jnp.where(is_ctx, shc_ref[...], shx_ref[0])
    sc = jnp.where(is_ctx, scc_ref[...], scx_ref[0])
    o_ref[0] = (y * (1.0 + sc) + sh).astype(o_ref.dtype)


def _normmod(x, g, sh_c, sc_c, sh_x, sc_x, n_ctx, out_dtype=BF16):
    b, t, d = x.shape
    tr = _pick(t, 1088, 16)
    vec_c = pl.BlockSpec((1, d), lambda i, j: (0, 0))
    vec_x = pl.BlockSpec((1, 1, d), lambda i, j: (i, 0, 0))
    return pl.pallas_call(
        functools.partial(_normmod_kernel, tr=tr, n_ctx=n_ctx),
        grid=(b, t // tr),
        in_specs=[pl.BlockSpec((1, tr, d), lambda i, j: (i, j, 0)),
                  vec_c, vec_c, vec_c, vec_x, vec_x],
        out_specs=pl.BlockSpec((1, tr, d), lambda i, j: (i, j, 0)),
        out_shape=jax.ShapeDtypeStruct((b, t, d), out_dtype),
        compiler_params=_params(("arbitrary", "arbitrary")),
        name="normmod",
    )(x, g.reshape(1, d), sh_c, sc_c, sh_x, sc_x)


def _mm_kernel(a_ref, w_ref, o_ref):
    o_ref[...] = jnp.dot(a_ref[...], w_ref[...],
                         preferred_element_type=F32).astype(o_ref.dtype)


def _matmul(a, w, out_dtype):
    m, k = a.shape
    n = w.shape[1]
    tm = _pick(m, 2048, 256)
    tn = _pick(n, 1024, 128)
    return pl.pallas_call(
        _mm_kernel,
        grid=(m // tm, n // tn),
        in_specs=[pl.BlockSpec((tm, k), lambda i, j: (i, 0)),
                  pl.BlockSpec((k, tn), lambda i, j: (0, j))],
        out_specs=pl.BlockSpec((tm, tn), lambda i, j: (i, j)),
        out_shape=jax.ShapeDtypeStruct((m, n), out_dtype),
        compiler_params=_params(("arbitrary", "arbitrary")),
        name="proj",
    )(a, w)


def _prep_kernel(p_ref, cos_ref, sin_ref, g_ref, j_ref, qg_ref, kg_ref, vg_ref,
                 qd_ref, kd_ref, vd_ref):
    cos = cos_ref[...]
    sin = sin_ref[...]
    lane = lax.broadcasted_iota(jnp.int32, (1, 128), 1)
    first_half = (lane % 32) < 16
    low_head = lane < HEAD_DIM

    def norm_rope(col, gi):
        x = p_ref[0, :, col:col + 128].astype(F32)
        ms = jnp.dot((x * x).astype(BF16), j_ref[...], preferred_element_type=F32)
        xn = x * lax.rsqrt(ms + EPS) * g_ref[gi:gi + 1, :]
        swapped = jnp.where(first_half, pltpu.roll(xn, 112, 1), pltpu.roll(xn, 16, 1))
        return xn * cos + swapped * sin

    def put_heads(ref, first, y):
        ref[0, first] = y[:, :HEAD_DIM].astype(ref.dtype)
        ref[0, first + 1] = y[:, HEAD_DIM:].astype(ref.dtype)

    for g in range(4):
        put_heads(qg_ref, 2 * g, norm_rope(128 * g, 0))
    put_heads(kg_ref, 0, norm_rope(512, 1))
    for g in range(4):
        put_heads(qd_ref, 2 * g, norm_rope(768 + 128 * g, 2))
    for g in range(4):
        put_heads(kd_ref, 2 * g, norm_rope(1280 + 128 * g, 3))
    av = p_ref[0, :, 640:768].astype(F32)
    ones_col = (lane == HEAD_DIM).astype(F32)
    vg_ref[0, 0] = (jnp.where(low_head, av, 0.0) + ones_col).astype(vg_ref.dtype)
    vg_ref[0, 1] = (jnp.where(low_head, pltpu.roll(av, 64, 1), 0.0) + ones_col).astype(vg_ref.dtype)
    tail = jnp.broadcast_to((lane == 0).astype(vd_ref.dtype), (av.shape[0], 128))
    for h in range(DIFF_HEADS):
        vd_ref[0, h, :, 0:128] = p_ref[0, :, 1792 + 128 * h:1792 + 128 * (h + 1)]
        vd_ref[0, h, :, 128:256] = tail


def _attn_prep(p_attn, cos, sin, gains, jmat):
    b, t, w = p_attn.shape
    tr = _pick(t, 544, 16)
    heads = lambda n, width: pl.BlockSpec((1, n, tr, width), lambda i, j: (i, 0, j, 0))
    shape = lambda n, width: jax.ShapeDtypeStruct((b, n, t, width), BF16)
    return pl.pallas_call(
        _prep_kernel,
        grid=(b, t // tr),
        in_specs=[pl.BlockSpec((1, tr, w), lambda i, j: (i, j, 0)),
                  pl.BlockSpec((tr, 128), lambda i, j: (j, 0)),
                  pl.BlockSpec((tr, 128), lambda i, j: (j, 0)),
                  pl.BlockSpec((4, 128), lambda i, j: (0, 0)),
                  pl.BlockSpec((128, 128), lambda i, j: (0, 0))],
        out_specs=[heads(GQA_HEADS, HEAD_DIM), heads(GQA_KV_HEADS, HEAD_DIM),
                   heads(GQA_KV_HEADS, 128),
                   heads(2 * DIFF_HEADS, HEAD_DIM), heads(2 * DIFF_HEADS, HEAD_DIM),
                   heads(DIFF_HEADS, 256)],
        out_shape=[shape(GQA_HEADS, HEAD_DIM), shape(GQA_KV_HEADS, HEAD_DIM),
                   shape(GQA_KV_HEADS, 128),
                   shape(2 * DIFF_HEADS, HEAD_DIM), shape(2 * DIFF_HEADS, HEAD_DIM),
                   shape(DIFF_HEADS, 256)],
        compiler_params=_params(("arbitrary", "arbitrary")),
        name="attn_prep",
    )(p_attn, cos, sin, gains, jmat)


ATT_TQ = 256
ATT_TK = 256


def _softmax_pv(q, k_ref, k_idx, v_ref, v_idx, s_scr, slot, nk, l_col):
    tq, tk = ATT_TQ, ATT_TK
    nchunk = nk // tk
    m_part = jnp.full((tq, 128), -jnp.inf, F32)
    for c in range(nchunk):
        kc = k_ref[0, k_idx, c * tk:(c + 1) * tk, :]
        s = lax.dot_general(q, kc, (((1,), (1,)), ((), ())), preferred_element_type=F32)
        s_scr[slot, :, c * tk:(c + 1) * tk] = s
        for u in range(tk // 128):
            m_part = jnp.maximum(m_part, s[:, u * 128:(u + 1) * 128])
    m = jnp.max(m_part, axis=-1, keepdims=True)
    acc = jnp.zeros((tq, v_ref.shape[-1]), F32)
    for c in range(nchunk):
        p = jnp.exp2(s_scr[slot, :, c * tk:(c + 1) * tk] - m)
        acc = acc + jnp.dot(p.astype(BF16), v_ref[0, v_idx, c * tk:(c + 1) * tk, :],
                            preferred_element_type=F32)
    return acc / acc[:, l_col:l_col + 1]


def _ctx_or_full(n_ctx, t, fn):
    is_ctx_tile = pl.program_id(2) * ATT_TQ < n_ctx

    @pl.when(is_ctx_tile)
    def _():
        fn(n_ctx)

    @pl.when(jnp.logical_not(is_ctx_tile))
    def _():
        fn(t)


ATT_MAPS = 4


def _gqa_kernel(q_ref, k_ref, v_ref, o_ref, s_scr, *, n_ctx, t):
    def attend(nk):
        for u in range(ATT_MAPS):
            o = _softmax_pv(q_ref[0, u], k_ref, 0, v_ref, 0, s_scr, u, nk, HEAD_DIM)
            o_ref[0, u] = o.astype(o_ref.dtype)
    _ctx_or_full(n_ctx, t, attend)


def _diff_kernel(lam_ref, q_ref, k_ref, v_ref, o_ref, s_scr, *, n_ctx, t):
    def attend(nk):
        for h in range(ATT_MAPS // 2):
            o1 = _softmax_pv(q_ref[0, 2 * h], k_ref, 2 * h, v_ref, h, s_scr, 2 * h, nk, 128)
            o2 = _softmax_pv(q_ref[0, 2 * h + 1], k_ref, 2 * h + 1, v_ref, h, s_scr, 2 * h + 1,
                             nk, 128)
            o_ref[0, h] = (o1[:, :128] - lam_ref[0] * o2[:, :128]).astype(o_ref.dtype)
    _ctx_or_full(n_ctx, t, attend)


def _attention_call(body, name, q, k, v, n_out, k_block, v_block, extra_in, extra_args, n_ctx):
    b, n_q, t, _ = q.shape
    assert n_ctx % ATT_TQ == 0 and t % ATT_TQ == 0 and n_ctx % ATT_TK == 0 and t % ATT_TK == 0
    vw = v.shape[-1]
    return pl.pallas_call(
        functools.partial(body, n_ctx=n_ctx, t=t),
        grid=(b, n_q // ATT_MAPS, t // ATT_TQ),
        in_specs=extra_in + [
            pl.BlockSpec((1, ATT_MAPS, ATT_TQ, HEAD_DIM), lambda i, u, j: (i, u, j, 0)),
            pl.BlockSpec((1, k_block, t, HEAD_DIM), lambda i, u, j: (i, u, 0, 0)),
            pl.BlockSpec((1, v_block, t, vw), lambda i, u, j: (i, u, 0, 0))],
        out_specs=pl.BlockSpec((1, n_out, ATT_TQ, 128), lambda i, u, j: (i, u, j, 0)),
        out_shape=jax.ShapeDtypeStruct((b, n_q // ATT_MAPS * n_out, t, 128), BF16),
        scratch_shapes=[pltpu.VMEM((ATT_MAPS, ATT_TQ, t), F32)],
        compiler_params=_params(("arbitrary", "arbitrary", "arbitrary")),
        name=name,
    )(*extra_args, q, k, v)


def _gqa_attention(q, k, v, n_ctx):
    assert GQA_HEADS // GQA_KV_HEADS == ATT_MAPS
    return _attention_call(_gqa_kernel, "attn_gqa", q, k, v, ATT_MAPS, 1, 1, [], [], n_ctx)


def _diff_attention(lam, q, k, v, n_ctx):
    return _attention_call(_diff_kernel, "attn_diff", q, k, v, ATT_MAPS // 2, ATT_MAPS,
                           ATT_MAPS // 2, [pl.BlockSpec(memory_space=pltpu.SMEM)], [lam], n_ctx)


CONV_TILE = 256
CONV_HALO = 16
CONV_ROWS = 32


def _conv_kernel(main_ref, left_ref, right_ref, w_ref, b_ref, lng_ref, lnb_ref, o_ref,
                 u_scr, *, n_ctx, t):
    j = pl.program_id(1)
    row0 = j * CONV_TILE
    has_left = jnp.logical_and(row0 != 0, row0 != n_ctx)
    has_right = jnp.logical_and(row0 + CONV_TILE != n_ctx, row0 + CONV_TILE != t)

    def glu(ref):
        x = ref[0].astype(F32)
        return x[:, :CONV_CH] * jax.nn.sigmoid(x[:, CONV_CH:])

    left = jnp.where(has_left, glu(left_ref), 0.0)
    right = jnp.where(has_right, glu(right_ref), 0.0)
    upad = jnp.concatenate([left, glu(main_ref), right], axis=0)
    n_pad = CONV_TILE + 2 * CONV_HALO
    for r in range(8):
        u_scr[r] = upad if r == 0 else pltpu.roll(upad, n_pad - r, 0)

    pad = CONV_WIDTH // 2

    def rows_body(i, carry):
        base = pl.multiple_of(i * CONV_ROWS, CONV_ROWS)
        acc = jnp.zeros((CONV_ROWS, CONV_CH), F32)
        for tap in range(CONV_WIDTH):
            off = tap + CONV_HALO - pad
            a, r = off // 8, off % 8
            acc = acc + w_ref[tap:tap + 1, :] * u_scr[r, pl.ds(base + 8 * a, CONV_ROWS), :]
        y = acc + b_ref[...]
        mu = jnp.mean(y, axis=-1, keepdims=True)
        yc = y - mu
        var = jnp.mean(yc * yc, axis=-1, keepdims=True)
        z = yc * lax.rsqrt(var + EPS) * lng_ref[...] + lnb_ref[...]
        o_ref[0, pl.ds(base, CONV_ROWS), :] = _silu(z).astype(o_ref.dtype)
        return carry

    lax.fori_loop(0, CONV_TILE // CONV_ROWS, rows_body, 0)


def _conv_branch(p_conv, w, bias, ln_g, ln_b, n_ctx):
    b, t, wd = p_conv.shape
    assert n_ctx % CONV_TILE == 0 and t % CONV_TILE == 0
    hb = CONV_TILE // CONV_HALO
    n_halo_blocks = t // CONV_HALO
    vec = pl.BlockSpec((1, CONV_CH), lambda i, j: (0, 0))
    return pl.pallas_call(
        functools.partial(_conv_kernel, n_ctx=n_ctx, t=t),
        grid=(b, t // CONV_TILE),
        in_specs=[pl.BlockSpec((1, CONV_TILE, wd), lambda i, j: (i, j, 0)),
                  pl.BlockSpec((1, CONV_HALO, wd),
                               lambda i, j: (i, jnp.maximum(j * hb - 1, 0), 0)),
                  pl.BlockSpec((1, CONV_HALO, wd),
                               lambda i, j: (i, jnp.minimum((j + 1) * hb, n_halo_blocks - 1), 0)),
                  pl.BlockSpec((CONV_WIDTH, CONV_CH), lambda i, j: (0, 0)),
                  vec, vec, vec],
        out_specs=pl.BlockSpec((1, CONV_TILE, CONV_CH), lambda i, j: (i, j, 0)),
        out_shape=jax.ShapeDtypeStruct((b, t, CONV_CH), BF16),
        scratch_shapes=[pltpu.VMEM((8, CONV_TILE + 2 * CONV_HALO, CONV_CH), F32)],
        compiler_params=_params(("arbitrary", "arbitrary")),
        name="conv",
    )(p_conv, p_conv, p_conv, w, bias.reshape(1, -1), ln_g.reshape(1, -1), ln_b.reshape(1, -1))


HG_TILE = 256
HG_W = HGRN_HEADS * HGRN_DK


def _dot3(a_bf16, x):
    x1 = x.astype(BF16)
    r1 = x - x1.astype(F32)
    x2 = r1.astype(BF16)
    x3 = (r1 - x2.astype(F32)).astype(BF16)
    out = jnp.dot(a_bf16, x1, preferred_element_type=F32)
    out = out + jnp.dot(a_bf16, x2, preferred_element_type=F32)
    return out + jnp.dot(a_bf16, x3, preferred_element_type=F32)


def _hgrn_kernel(q_ref, z_ref, v_ref, la_ref, lb0_ref, tri_ref, j_ref, bd_ref, o_ref,
                 st_scr, q_scr, k_scr, c_scr, v_scr, ks_scr, cs_scr, vs_scr, acc_scr, *, rev):
    c_sz = HGRN_CHUNK
    n_chunk = HG_TILE // c_sz

    @pl.when(pl.program_id(1) == 0)
    def _():
        st_scr[...] = jnp.zeros_like(st_scr)

    q = _silu(q_ref[0].astype(F32)) * (HGRN_DK ** -0.5)
    z = z_ref[0]
    v = v_ref[0].astype(F32)
    log_sig = jnp.minimum(z, 0.0) - jnp.log(1.0 + jnp.exp(-jnp.abs(z)))
    t1 = la_ref[...]
    t2 = lb0_ref[...] + log_sig
    logf = jnp.maximum(t1, t2) + jnp.log(1.0 + jnp.exp(-jnp.abs(t1 - t2)))
    k = 1.0 - jnp.exp(logf)
    c2 = _dot3(tri_ref[...], logf * LOG2E)
    q_scr[...] = q
    k_scr[...] = k
    c_scr[...] = c2
    v_scr[...] = v

    order = range(n_chunk - 1, -1, -1) if rev else range(n_chunk)
    for ci in order:
        lo = ci * c_sz
        cj = c_scr[lo:lo + c_sz, :]
        c_end = cj[0:1] if rev else cj[c_sz - 1:c_sz]
        qd = (q_scr[lo:lo + c_sz, :] * jnp.exp2(cj)).astype(BF16)
        st = st_scr[...]
        acc_scr[lo:lo + c_sz, :] = lax.dot_general(
            qd, st.astype(BF16), (((1,), (1,)), ((), ())), preferred_element_type=F32)
        kd = (k_scr[lo:lo + c_sz, :] * jnp.exp2(c_end - cj)).astype(BF16)
        vb = v_scr[lo:lo + c_sz, :].astype(BF16)
        upd = lax.dot_general(vb, kd, (((0,), (0,)), ((), ())),
                              preferred_element_type=F32)
        st_scr[...] = (st * jnp.exp2(c_end) + upd) * bd_ref[...]

    row_in_chunk = lax.broadcasted_iota(jnp.int32, (HG_TILE, 1), 0) % c_sz

    def shifted(ref, r):
        amt = (HG_TILE - r) % HG_TILE if rev else r
        y = pltpu.roll(ref[...], amt, 0)
        keep = (row_in_chunk < c_sz - r) if rev else (row_in_chunk >= r)
        return jnp.where(keep, y, 0.0)

    def r_body(r, carry):
        ks_scr[...] = shifted(k_scr, r)
        cs_scr[...] = shifted(c_scr, r)
        vs_scr[...] = shifted(v_scr, r)
        for a in range(c_sz // 8):
            n = c_sz - 8 * a
            d_lo = 0 if rev else 8 * a
            s_lo = 8 * a if rev else 0
            def rows(ref, lo_in_chunk):
                return jnp.concatenate(
                    [ref[ci * c_sz + lo_in_chunk:ci * c_sz + lo_in_chunk + n, :]
                     for ci in range(n_chunk)], axis=0)
            qa = rows(q_scr, d_lo)
            ca = rows(c_scr, d_lo)
            kk = rows(ks_scr, s_lo)
            cc = rows(cs_scr, s_lo)
            vv = rows(vs_scr, s_lo)
            pw = (qa * kk * jnp.exp2(ca - cc)).astype(BF16)
            att = jnp.dot(pw, j_ref[...], preferred_element_type=F32)
            contrib = att * vv
            for ci in range(n_chunk):
                dst = ci * c_sz + d_lo
                acc_scr[dst:dst + n, :] = acc_scr[dst:dst + n, :] + contrib[ci * n:(ci + 1) * n]
        return carry

    lax.fori_loop(0, 8, r_body, 0)
    o_ref[0] = acc_scr[...]


def _hgrn_dir(p_hq, p_hf, p_hig, la, lb0, tri, jmat, bd, n_ctx, rev):
    b, t, _ = p_hq.shape
    assert n_ctx % HG_TILE == 0 and t % HG_TILE == 0
    n_tiles = t // HG_TILE
    n_ctx_tiles = n_ctx // HG_TILE
    col = 1 if rev else 0

    def tile(j):
        if not rev:
            return j
        return jnp.where(j < n_ctx_tiles, n_ctx_tiles - 1 - j, n_tiles - 1 - (j - n_ctx_tiles))

    const = lambda shape: pl.BlockSpec(shape, lambda i, j: (0,) * len(shape))
    return pl.pallas_call(
        functools.partial(_hgrn_kernel, rev=rev),
        grid=(b, n_tiles),
        in_specs=[pl.BlockSpec((1, HG_TILE, HG_W), lambda i, j: (i, tile(j), 0)),
                  pl.BlockSpec((1, HG_TILE, HG_W), lambda i, j: (i, tile(j), col)),
                  pl.BlockSpec((1, HG_TILE, HG_W), lambda i, j: (i, tile(j), 0)),
                  const((1, HG_W)),
                  const((1, HG_W)),
                  const((HG_TILE, HG_TILE)), const((HG_W, HG_W)), const((HG_W, HG_W))],
        out_specs=pl.BlockSpec((1, HG_TILE, HG_W), lambda i, j: (i, tile(j), 0)),
        out_shape=jax.ShapeDtypeStruct((b, t, HG_W), F32),
        scratch_shapes=[pltpu.VMEM((HG_W, HG_W), F32)]
                       + [pltpu.VMEM((HG_TILE, HG_W), F32) for _ in range(8)],
        compiler_params=_params(("arbitrary", "arbitrary")),
        name="hgrn_bwd" if rev else "hgrn_fwd",
    )(p_hq, p_hf, p_hig, la[col:col + 1], lb0[col:col + 1], tri, jmat, bd)


def _merge_kernel(og_ref, od_ref, cv_ref, hf_ref, hb_ref, hg_ref, gt_ref, x_ref, wb_ref, wo_ref,
                  sub_ref, hng_ref, j_ref, gc_ref, gx_ref, o_ref, *, tr, n_ctx, lam_init):
    gates = gt_ref[0]

    def gate(i):
        return jax.nn.sigmoid(gates[:, i * 1024:(i + 1) * 1024].astype(F32))

    pa = jnp.zeros((tr, 1024), F32)
    for h in range(GQA_HEADS):
        pa = pa + jnp.dot(og_ref[0, h][:, :HEAD_DIM],
                          wb_ref[0, h * HEAD_DIM:(h + 1) * HEAD_DIM, :],
                          preferred_element_type=F32)
    merged = gate(0) * pa
    pb = jnp.zeros((tr, 1024), F32)
    for h in range(DIFF_HEADS):
        od = od_ref[0, h].astype(F32)
        ms = jnp.mean(od * od, axis=-1, keepdims=True)
        on = od * lax.rsqrt(ms + EPS) * sub_ref[...] * (1.0 - lam_init)
        pb = pb + jnp.dot(on.astype(BF16), wb_ref[1, h * 128:(h + 1) * 128, :],
                          preferred_element_type=F32)
    merged = merged + gate(1) * pb
    merged = merged + gate(2) * jnp.dot(cv_ref[0], wb_ref[2], preferred_element_type=F32)
    od = hf_ref[0] + hb_ref[0]
    ms = jnp.dot((od * od).astype(BF16), j_ref[...], preferred_element_type=F32)
    on = od * lax.rsqrt(ms + EPS) * hng_ref[...] * _silu(hg_ref[0].astype(F32))
    merged = merged + gate(3) * jnp.dot(on.astype(BF16), wb_ref[3], preferred_element_type=F32)

    mix = jnp.dot(merged.astype(BF16), wo_ref[...], preferred_element_type=F32)
    g1 = jnp.where(_row_is_ctx(pl.program_id(1), tr, n_ctx), gc_ref[...], gx_ref[0])
    o_ref[0] = x_ref[0] + g1 * mix


def _merge(o_gqa, o_diff, conv_o, hg_f, hg_b, p_hig, p_gate, x, wb, wo, subln_g, hgrn_g, jmat64,
           g_c, g_x, n_ctx, lam_init):
    b, t, d = x.shape
    tr = _pick(t, 272, 16)
    const = lambda shape: pl.BlockSpec(shape, lambda i, j: (0,) * len(shape))
    return pl.pallas_call(
        functools.partial(_merge_kernel, tr=tr, n_ctx=n_ctx, lam_init=lam_init),
        grid=(b, t // tr),
        in_specs=[pl.BlockSpec((1, GQA_HEADS, tr, 128), lambda i, j: (i, 0, j, 0)),
                  pl.BlockSpec((1, DIFF_HEADS, tr, 128), lambda i, j: (i, 0, j, 0)),
                  pl.BlockSpec((1, tr, CONV_CH), lambda i, j: (i, j, 0)),
                  pl.BlockSpec((1, tr, HG_W), lambda i, j: (i, j, 0)),
                  pl.BlockSpec((1, tr, HG_W), lambda i, j: (i, j, 0)),
                  pl.BlockSpec((1, tr, HG_W), lambda i, j: (i, j, 1)),
                  pl.BlockSpec((1, tr, 4 * d), lambda i, j: (i, j, 0)),
                  pl.BlockSpec((1, tr, d), lambda i, j: (i, j, 0)),
                  const((4, BRANCH_W, d)), const((d, d)),
                  const((1, 128)), const((1, HG_W)), const((HG_W, HG_W)),
                  const((1, d)),
                  pl.BlockSpec((1, 1, d), lambda i, j: (i, 0, 0))],
        out_specs=pl.BlockSpec((1, tr, d), lambda i, j: (i, j, 0)),
        out_shape=jax.ShapeDtypeStruct((b, t, d), F32),
        compiler_params=_params(("arbitrary", "arbitrary")),
        name="merge",
    )(o_gqa, o_diff, conv_o, hg_f, hg_b, p_hig, p_gate, x, wb, wo, subln_g, hgrn_g, jmat64,
      g_c, g_x)


def _swiglu_step(h, w1, w3, w2):
    a = jnp.dot(h, w1, preferred_element_type=F32)
    g = jnp.dot(h, w3, preferred_element_type=F32)
    return jnp.dot((_silu(a) * g).astype(BF16), w2, preferred_element_type=F32)


def _ffn_kernel(h_ref, w1_ref, w3_ref, w2_ref, x_ref, gc_ref, gx_ref, o_ref, acc_scr, *, tr, n_ctx):
    kf = pl.program_id(2)

    @pl.when(kf == 0)
    def _():
        acc_scr[...] = jnp.zeros_like(acc_scr)

    acc_scr[...] += _swiglu_step(h_ref[0], w1_ref[...], w3_ref[...], w2_ref[...])

    @pl.when(kf == pl.num_programs(2) - 1)
    def _():
        g2 = jnp.where(_row_is_ctx(pl.program_id(1), tr, n_ctx), gc_ref[...], gx_ref[0])
        o_ref[0] = x_ref[0] + g2 * acc_scr[...]


def _ffn(h, w1, w3, w2, x, g_c, g_x, n_ctx):
    b, t, d = x.shape
    dff = w1.shape[1]
    tr = _pick(t, 1088, 16)
    tf = _pick(dff, 512, 128)
    return pl.pallas_call(
        functools.partial(_ffn_kernel, tr=tr, n_ctx=n_ctx),
        grid=(b, t // tr, dff // tf),
        in_specs=[pl.BlockSpec((1, tr, d), lambda i, j, f: (i, j, 0)),
                  pl.BlockSpec((d, tf), lambda i, j, f: (0, f)),
                  pl.BlockSpec((d, tf), lambda i, j, f: (0, f)),
                  pl.BlockSpec((tf, d), lambda i, j, f: (f, 0)),
                  pl.BlockSpec((1, tr, d), lambda i, j, f: (i, j, 0)),
                  pl.BlockSpec((1, d), lambda i, j, f: (0, 0)),
                  pl.BlockSpec((1, 1, d), lambda i, j, f: (i, 0, 0))],
        out_specs=pl.BlockSpec((1, tr, d), lambda i, j, f: (i, j, 0)),
        out_shape=jax.ShapeDtypeStruct((b, t, d), F32),
        scratch_shapes=[pltpu.VMEM((tr, d), F32)],
        input_output_aliases={4: 0},
        compiler_params=_params(("arbitrary", "arbitrary", "arbitrary")),
        name="ffn",
    )(h, w1, w3, w2, x, g_c, g_x)


MOE_TILE = 512
GATHER_ROWS = 256
COMBINE_ROWS = 544


def _row_gather(idx_ref, n_rows, idx_base, src_ref, dst_ref, sem, phase):
    def step(r, carry):
        copy = pltpu.make_async_copy(src_ref.at[pl.ds(idx_ref[0, 0, idx_base + r], 1), :],
                                     dst_ref.at[pl.ds(r, 1), :], sem)
        if phase == "start":
            copy.start()
        else:
            copy.wait()
        return carry

    lax.fori_loop(0, n_rows, step, 0)


def _dispatch_kernel(idx_ref, src_ref, o_ref, sem):
    for phase in ("start", "wait"):
        _row_gather(idx_ref, GATHER_ROWS, 0, src_ref, o_ref, sem, phase)


def _dispatch(src_rows, tokens):
    n_sorted = tokens.shape[0]
    d = src_rows.shape[1]
    steps = n_sorted // GATHER_ROWS
    return pl.pallas_call(
        _dispatch_kernel,
        grid=(steps,),
        in_specs=[pl.BlockSpec((1, 1, GATHER_ROWS), lambda i: (i, 0, 0), memory_space=pltpu.SMEM),
                  pl.BlockSpec(memory_space=pl.ANY)],
        out_specs=pl.BlockSpec((GATHER_ROWS, d), lambda i: (i, 0)),
        out_shape=jax.ShapeDtypeStruct((n_sorted, d), src_rows.dtype),
        scratch_shapes=[pltpu.SemaphoreType.DMA(())],
        compiler_params=_params(("arbitrary",)),
        name="moe_dispatch",
    )(tokens.reshape(steps, 1, GATHER_ROWS), src_rows)


def _expert_kernel(te_ref, nu_ref, x_ref, w1_ref, w3_ref, w2_ref, o_ref, acc_scr):
    i = pl.program_id(0)
    kf = pl.program_id(1)
    last = kf == pl.num_programs(1) - 1
    used = i < nu_ref[0]

    @pl.when(jnp.logical_and(used, kf == 0))
    def _():
        acc_scr[...] = jnp.zeros_like(acc_scr)

    @pl.when(used)
    def _():
        acc_scr[...] += _swiglu_step(x_ref[...].astype(BF16), w1_ref[0], w3_ref[0], w2_ref[0])

    @pl.when(jnp.logical_and(used, last))
    def _():
        o_ref[...] = acc_scr[...]

    @pl.when(jnp.logical_and(jnp.logical_not(used), last))
    def _():
        o_ref[...] = jnp.zeros_like(o_ref)


def _expert_ffn(xs, w1, w3, w2, tile_expert, n_used):
    n_sorted, d = xs.shape
    dff = w1.shape[2]
    tf = _pick(dff, 896, 128)
    grid_spec = pltpu.PrefetchScalarGridSpec(
        num_scalar_prefetch=2,
        grid=(n_sorted // MOE_TILE, dff // tf),
        in_specs=[pl.BlockSpec((MOE_TILE, d), lambda i, f, te, nu: (i, 0)),
                  pl.BlockSpec((1, d, tf), lambda i, f, te, nu: (te[i], 0, f)),
                  pl.BlockSpec((1, d, tf), lambda i, f, te, nu: (te[i], 0, f)),
                  pl.BlockSpec((1, tf, d), lambda i, f, te, nu: (te[i], f, 0))],
        out_specs=pl.BlockSpec((MOE_TILE, d), lambda i, f, te, nu: (i, 0)),
        scratch_shapes=[pltpu.VMEM((MOE_TILE, d), F32)])
    return pl.pallas_call(
        _expert_kernel,
        grid_spec=grid_spec,
        out_shape=jax.ShapeDtypeStruct((n_sorted, d), F32),
        compiler_params=_params(("arbitrary", "arbitrary")),
        name="moe_experts",
    )(tile_expert, n_used, xs, w1, w3, w2)


def _combine_kernel(pos_ref, rw_ref, ys_ref, x_ref, gc_ref, gx_ref, o_ref, buf, sem, *, tr, n_ctx):
    for phase in ("start", "wait"):
        for k in range(2):
            _row_gather(pos_ref, tr, k * tr, ys_ref, buf.at[k], sem, phase)
    rw = rw_ref[0]
    f = rw[:, 2:3] * buf[0] + rw[:, 3:4] * buf[1]
    g2 = jnp.where(_row_is_ctx(pl.program_id(1), tr, n_ctx), gc_ref[...], gx_ref[0])
    o_ref[0] = x_ref[0] + g2 * f


def _combine(pos, route, ys, x, g_c, g_x, n_ctx):
    b, t, d = x.shape
    tr = pos.shape[-1] // 2
    n_t = t // tr
    return pl.pallas_call(
        functools.partial(_combine_kernel, tr=tr, n_ctx=n_ctx),
        grid=(b, n_t),
        in_specs=[pl.BlockSpec((1, 1, 2 * tr), lambda i, j: (i * n_t + j, 0, 0),
                               memory_space=pltpu.SMEM),
                  pl.BlockSpec((1, tr, 128), lambda i, j: (i, j, 0)),
                  pl.BlockSpec(memory_space=pl.ANY),
                  pl.BlockSpec((1, tr, d), lambda i, j: (i, j, 0)),
                  pl.BlockSpec((1, d), lambda i, j: (0, 0)),
                  pl.BlockSpec((1, 1, d), lambda i, j: (i, 0, 0))],
        out_specs=pl.BlockSpec((1, tr, d), lambda i, j: (i, j, 0)),
        out_shape=jax.ShapeDtypeStruct((b, t, d), F32),
        scratch_shapes=[pltpu.VMEM((2, tr, d), F32), pltpu.SemaphoreType.DMA(())],
        input_output_aliases={3: 0},
        compiler_params=_params(("arbitrary", "arbitrary")),
        name="moe_combine",
    )(pos.reshape(b * n_t, 1, 2 * tr), route, ys, x, g_c, g_x)


def _moe_plan(e1, e2):
    n = e1.shape[0]
    e = jnp.concatenate([e1, e2])
    onehot = (e[:, None] == jnp.arange(N_EXPERTS, dtype=jnp.int32)[None, :]).astype(jnp.int32)
    counts = jnp.sum(onehot, axis=0)
    rank = jnp.sum((jnp.cumsum(onehot, axis=0) - onehot) * onehot, axis=1)
    padded = ((counts + MOE_TILE - 1) // MOE_TILE) * MOE_TILE
    ends = jnp.cumsum(padded)
    slot = jnp.sum((ends - padded)[None, :] * onehot, axis=1) + rank
    n_tiles = (2 * n + N_EXPERTS * (MOE_TILE - 1)) // MOE_TILE
    token = jnp.tile(jnp.arange(n, dtype=jnp.int32), 2)
    tokens_sorted = jnp.zeros((n_tiles * MOE_TILE,), jnp.int32).at[slot].set(token)
    tile_start = jnp.arange(n_tiles, dtype=jnp.int32) * MOE_TILE
    tile_expert = jnp.minimum(jnp.sum((ends[None, :] <= tile_start[:, None]).astype(jnp.int32), axis=1),
                              N_EXPERTS - 1)
    n_used = (ends[-1] // MOE_TILE).reshape(1)
    return slot[:n], slot[n:], tokens_sorted, tile_expert, n_used


def _moe(h_f32, route, w1, w3, w2, x, g_c, g_x, n_ctx):
    b, t, d = x.shape
    e1 = route[:, :, 0].reshape(-1).astype(jnp.int32)
    e2 = route[:, :, 1].reshape(-1).astype(jnp.int32)
    pos1, pos2, tokens_sorted, tile_expert, n_used = _moe_plan(e1, e2)
    xs = _dispatch(h_f32.reshape(b * t, d), tokens_sorted)
    ys = _expert_ffn(xs, w1, w3, w2, tile_expert, n_used)
    tr = _pick(t, COMBINE_ROWS, 16)
    pos = jnp.concatenate([pos1.reshape(b, t // tr, 1, tr),
                           pos2.reshape(b, t // tr, 1, tr)], axis=-1)
    return _combine(pos, route, ys, x, g_c, g_x, n_ctx)


def _router_kernel(h_ref, r_ref, o_ref):
    logits = jnp.dot(h_ref[0], r_ref[...], preferred_element_type=F32,
                     precision=lax.Precision.HIGHEST)
    lane = lax.broadcasted_iota(jnp.int32, logits.shape, 1).astype(F32)
    neg = -jnp.inf
    logits = jnp.where(lane < N_EXPERTS, logits, neg)
    m1 = jnp.max(logits, axis=-1, keepdims=True)
    i1 = jnp.min(jnp.where(logits == m1, lane, 128.0), axis=-1, keepdims=True)
    rest = jnp.where(lane == i1, neg, logits)
    m2 = jnp.max(rest, axis=-1, keepdims=True)
    i2 = jnp.min(jnp.where(rest == m2, lane, 128.0), axis=-1, keepdims=True)
    e2 = jnp.exp(m2 - m1)
    w1 = 1.0 / (1.0 + e2)
    w2 = e2 / (1.0 + e2)
    o_ref[0] = (jnp.where(lane == 0.0, i1, 0.0) + jnp.where(lane == 1.0, i2, 0.0)
                + jnp.where(lane == 2.0, w1, 0.0) + jnp.where(lane == 3.0, w2, 0.0))


def _router(h_f32, router_pad):
    b, t, d = h_f32.shape
    tr = _pick(t, 1088, 16)
    return pl.pallas_call(
        _router_kernel,
        grid=(b, t // tr),
        in_specs=[pl.BlockSpec((1, tr, d), lambda i, j: (i, j, 0)),
                  pl.BlockSpec((d, 128), lambda i, j: (0, 0))],
        out_specs=pl.BlockSpec((1, tr, 128), lambda i, j: (i, j, 0)),
        out_shape=jax.ShapeDtypeStruct((b, t, 128), F32),
        compiler_params=_params(("arbitrary", "arbitrary")),
        name="router",
    )(h_f32, router_pad)


def _rope_tables(seq, n_ctx):
    rows = seq // GRID_W
    row = jnp.repeat(jnp.arange(rows, dtype=F32), GRID_W)
    col = (jnp.arange(seq) % GRID_W).astype(F32)
    axis_dim = HEAD_DIM // 2
    inv_freq = ROPE_THETA ** (-jnp.arange(0, axis_dim, 2, dtype=F32) / axis_dim)
    ang_r = row[:, None] * inv_freq
    ang_c = col[:, None] * inv_freq
    cos = jnp.concatenate([jnp.cos(ang_r), jnp.cos(ang_r), jnp.cos(ang_c), jnp.cos(ang_c)], axis=-1)
    sin = jnp.concatenate([-jnp.sin(ang_r), jnp.sin(ang_r), -jnp.sin(ang_c), jnp.sin(ang_c)], axis=-1)
    cos = jnp.concatenate([jnp.ones((n_ctx, HEAD_DIM), F32), cos], axis=0)
    sin = jnp.concatenate([jnp.zeros((n_ctx, HEAD_DIM), F32), sin], axis=0)
    return jnp.tile(cos, (1, 2)), jnp.tile(sin, (1, 2))


def _block_ones(n, blk, value=1.0, dtype=BF16):
    i = np.arange(n) // blk
    return jnp.asarray((i[:, None] == i[None, :]).astype(np.float32) * value, dtype)


def _chunk_tri(n, chunk, rev):
    i = np.arange(n)
    same = (i[:, None] // chunk) == (i[None, :] // chunk)
    tri = (i[None, :] >= i[:, None]) if rev else (i[None, :] <= i[:, None])
    return jnp.asarray((same & tri).astype(np.float32), BF16)


def kernel(x, c, ctx, c_ctx, ada_w, ada_b, norm1_g, norm2_g, w_in, qk_norm_g, diff_lambda,
           diff_subln_g, conv_w, conv_b, conv_ln_g, conv_ln_b, hgrn_lb_logits, hgrn_norm_g,
           w_branch, w_out, ffn_w1, ffn_w3, ffn_w2, moe_router, moe_w1, moe_w3, moe_w2):
    b, seq, d = x.shape
    n_ctx = ctx.shape[1]
    t = n_ctx + seq
    depth = ada_w.shape[0]

    cos, sin = _rope_tables(seq, n_ctx)
    j_head_mean = _block_ones(128, HEAD_DIM, 1.0 / HEAD_DIM)
    j_hg_sum = _block_ones(HG_W, HGRN_DK)
    j_hg_mean = _block_ones(HG_W, HGRN_DK, 1.0 / HGRN_DK)
    bd_mask = _block_ones(HG_W, HGRN_DK, 1.0, F32)
    tri_f = _chunk_tri(HG_TILE, HGRN_CHUNK, False)
    tri_b = _chunk_tri(HG_TILE, HGRN_CHUNK, True)

    lb = jnp.cumsum(jax.nn.softmax(hgrn_lb_logits.astype(F32), axis=0), axis=0)
    lb = lb - lb[:1]
    log_lb = jnp.log(lb)
    log_1m_lb = jnp.log1p(-lb)

    rows = ((b + 1 + 7) // 8) * 8
    cvec = jnp.zeros((rows, d), F32).at[:b].set(c).at[b].set(c_ctx)
    mod = _ada(cvec, ada_w, ada_b).reshape(depth, rows, 6, d)

    xc = jnp.concatenate([ctx, x], axis=1)
    q_scale = LOG2E * HEAD_DIM ** -0.5

    for l in range(depth):
        mx = mod[l, :b]
        mc = mod[l, b]
        vx = lambda i: mx[:, i:i + 1, :]
        vc = lambda i: mc[i:i + 1, :]
        w_l = w_in[l].astype(BF16)

        h = _normmod(xc, norm1_g[l], vc(0), vc(1), vx(0), vx(1), n_ctx).reshape(b * t, d)
        proj = lambda cols, dt: _matmul(h, w_l[:, cols[0]:cols[1]], dt).reshape(b, t, -1)
        p_attn = proj(COL_ATTN, BF16)
        p_conv = proj(COL_CONV, BF16)
        p_hq = proj(COL_HQ, BF16)
        p_hf = proj(COL_HF, F32)
        p_hig = proj(COL_HIG, BF16)
        p_gate = proj(COL_GATE, BF16)

        g = qk_norm_g[l].astype(F32)
        gains = jnp.tile(jnp.stack([g[0] * q_scale, g[1], g[2] * q_scale, g[3]]), (1, 2))
        lam_init = 0.8 - 0.6 * math.exp(-0.3 * l)
        lp = diff_lambda[l].astype(F32)
        lam = (jnp.exp(jnp.sum(lp[0] * lp[1])) - jnp.exp(jnp.sum(lp[2] * lp[3]))
               + lam_init).reshape(1)
        qg, kg, vg, qd, kd, vd = _attn_prep(p_attn, cos, sin, gains, j_head_mean)
        o_gqa = _gqa_attention(qg, kg, vg, n_ctx)
        o_diff = _diff_attention(lam, qd, kd, vd, n_ctx)

        conv_o = _conv_branch(p_conv, conv_w[l], conv_b[l], conv_ln_g[l], conv_ln_b[l], n_ctx)

        hg_f = _hgrn_dir(p_hq, p_hf, p_hig, log_lb[l], log_1m_lb[l], tri_f, j_hg_sum, bd_mask,
                         n_ctx, False)
        hg_b = _hgrn_dir(p_hq, p_hf, p_hig, log_lb[l], log_1m_lb[l], tri_b, j_hg_sum, bd_mask,
                         n_ctx, True)

        xc = _merge(o_gqa, o_diff, conv_o, hg_f, hg_b, p_hig, p_gate, xc,
                    w_branch[l].astype(BF16), w_out[l].astype(BF16),
                    diff_subln_g[l].reshape(1, -1), jnp.tile(hgrn_norm_g[l], HGRN_HEADS).reshape(1, -1),
                    j_hg_mean, vc(2), vx(2), n_ctx, lam_init)

        if l % 2 == 0:
            h2 = _normmod(xc, norm2_g[l], vc(3), vc(4), vx(3), vx(4), n_ctx)
            i = l // 2
            xc = _ffn(h2, ffn_w1[i].astype(BF16), ffn_w3[i].astype(BF16), ffn_w2[i].astype(BF16),
                      xc, vc(5), vx(5), n_ctx)
        else:
            h2f = _normmod(xc, norm2_g[l], vc(3), vc(4), vx(3), vx(4), n_ctx, out_dtype=F32)
            i = l // 2
            router_pad = jnp.zeros((d, 128), F32).at[:, :N_EXPERTS].set(moe_router[i])
            route = _router(h2f, router_pad)
            xc = _moe(h2f, route, moe_w1[i].astype(BF16), moe_w3[i].astype(BF16),
                      moe_w2[i].astype(BF16), xc, vc(5), vx(5), n_ctx)
    return xc[:, n_ctx:]
```

```python
import functools
import math

import jax
import jax.numpy as jnp
import numpy as np
from jax import lax
from jax.experimental import pallas as pl
from jax.experimental.pallas import tpu as pltpu

F32 = jnp.float32
BF16 = jnp.bfloat16

GRID_W = 64
HEAD_DIM = 64
ROPE_THETA = 10000.0
EPS = 1e-6
GQA_HEADS = 8
GQA_KV_HEADS = 2
DIFF_HEADS = 4
CONV_CH = 512
CONV_WIDTH = 31
HGRN_HEADS = 8
HGRN_DK = 64
HGRN_CHUNK = 64
BRANCH_W = 512
N_EXPERTS = 8
LOG2E = 1.4426950408889634

COL_ATTN = (0, 2304)
COL_CONV = (2304, 3328)
COL_HQ = (3328, 3840)
COL_HF = (3840, 4864)
COL_HIG = (4864, 5888)
COL_GATE = (5888, 9984)

VMEM_LIMIT_BYTES = 50 * 1024 * 1024


def _params(sem):
    return pltpu.CompilerParams(dimension_semantics=sem, vmem_limit_bytes=VMEM_LIMIT_BYTES)


def _pick(n, target, mult):
    best = None
    for d in range(mult, min(n, target) + 1, mult):
        if n % d == 0:
            best = d
    assert best is not None, (n, target, mult)
    return best


def _silu(x):
    return x * jax.nn.sigmoid(x)


def _row_is_ctx(tile_idx, tr, n_ctx):
    row = tile_idx * tr + lax.broadcasted_iota(jnp.int32, (tr, 1), 0)
    return row < n_ctx


def _ada_kernel(c_ref, w_ref, b_ref, o_ref):
    s = _silu(c_ref[...])
    o_ref[0] = jnp.dot(s.astype(BF16), w_ref[0].astype(BF16),
                       preferred_element_type=F32) + b_ref[0]


def _ada(cvec, ada_w, ada_b):
    depth, d, n = ada_w.shape
    rows = cvec.shape[0]
    tn = _pick(n, 1536, 128)
    return pl.pallas_call(
        _ada_kernel,
        grid=(depth, n // tn),
        in_specs=[pl.BlockSpec((rows, d), lambda l, j: (0, 0)),
                  pl.BlockSpec((1, d, tn), lambda l, j: (l, 0, j)),
                  pl.BlockSpec((1, 1, tn), lambda l, j: (l, 0, j))],
        out_specs=pl.BlockSpec((1, rows, tn), lambda l, j: (l, 0, j)),
        out_shape=jax.ShapeDtypeStruct((depth, rows, n), F32),
        compiler_params=_params(("arbitrary", "arbitrary")),
        name="ada",
    )(cvec, ada_w, ada_b.reshape(depth, 1, n))


def _normmod_kernel(x_ref, g_ref, shc_ref, scc_ref, shx_ref, scx_ref, o_ref, *, tr, n_ctx):
    x = x_ref[0]
    ms = jnp.mean(x * x, axis=-1, keepdims=True)
    y = x * lax.rsqrt(ms + EPS) * g_ref[...]
    is_ctx = _row_is_ctx(pl.program_id(1), tr, n_ctx)
    sh = jnp.where(is_ctx, shc_ref[...], shx_ref[0])
    sc = jnp.where(is_ctx, scc_ref[...], scx_ref[0])
    o_ref[0] = (y * (1.0 + sc) + sh).astype(o_ref.dtype)


def _normmod(x, g, sh_c, sc_c, sh_x, sc_x, n_ctx, out_dtype=BF16):
    b, t, d = x.shape
    tr = _pick(t, 1088, 16)
    vec_c = pl.BlockSpec((1, d), lambda i, j: (0, 0))
    vec_x = pl.BlockSpec((1, 1, d), lambda i, j: (i, 0, 0))
    return pl.pallas_call(
        functools.partial(_normmod_kernel, tr=tr, n_ctx=n_ctx),
        grid=(b, t // tr),
        in_specs=[pl.BlockSpec((1, tr, d), lambda i, j: (i, j, 0)),
                  vec_c, vec_c, vec_c, vec_x, vec_x],
        out_specs=pl.BlockSpec((1, tr, d), lambda i, j: (i, j, 0)),
        out_shape=jax.ShapeDtypeStruct((b, t, d), out_dtype),
        compiler_params=_params(("arbitrary", "arbitrary")),
        name="normmod",
    )(x, g.reshape(1, d), sh_c, sc_c, sh_x, sc_x)


def _mm_kernel(a_ref, w_ref, o_ref):
    o_ref[...] = jnp.dot(a_ref[...], w_ref[...],
                         preferred_element_type=F32).astype(o_ref.dtype)


def _matmul(a, w, out_dtype):
    m, k = a.shape
    n = w.shape[1]
    tm = _pick(m, 2048, 256)
    tn = _pick(n, 1024, 128)
    return pl.pallas_call(
        _mm_kernel,
        grid=(m // tm, n // tn),
        in_specs=[pl.BlockSpec((tm, k), lambda i, j: (i, 0)),
                  pl.BlockSpec((k, tn), lambda i, j: (0, j))],
        out_specs=pl.BlockSpec((tm, tn), lambda i, j: (i, j)),
        out_shape=jax.ShapeDtypeStruct((m, n), out_dtype),
        compiler_params=_params(("arbitrary", "arbitrary")),
        name="proj",
    )(a, w)


def _prep_kernel(p_ref, cos_ref, sin_ref, g_ref, j_ref, qg_ref, kg_ref, vg_ref,
                 qd_ref, kd_ref, vd_ref):
    cos = cos_ref[...]
    sin = sin_ref[...]
    lane = lax.broadcasted_iota(jnp.int32, (1, 128), 1)
    first_half = (lane % 32) < 16
    low_head = lane < HEAD_DIM

    def norm_rope(col, gi):
        x = p_ref[0, :, col:col + 128].astype(F32)
        ms = jnp.dot((x * x).astype(BF16), j_ref[...], preferred_element_type=F32)
        xn = x * lax.rsqrt(ms + EPS) * g_ref[gi:gi + 1, :]
        swapped = jnp.where(first_half, pltpu.roll(xn, 112, 1), pltpu.roll(xn, 16, 1))
        return xn * cos + swapped * sin

    def put_heads(ref, first, y):
        ref[0, first] = y[:, :HEAD_DIM].astype(ref.dtype)
        ref[0, first + 1] = y[:, HEAD_DIM:].astype(ref.dtype)

    for g in range(4):
        put_heads(qg_ref, 2 * g, norm_rope(128 * g, 0))
    put_heads(kg_ref, 0, norm_rope(512, 1))
    for g in range(4):
        put_heads(qd_ref, 2 * g, norm_rope(768 + 128 * g, 2))
    for g in range(4):
        put_heads(kd_ref, 2 * g, norm_rope(1280 + 128 * g, 3))
    av = p_ref[0, :, 640:768].astype(F32)
    ones_col = (lane == HEAD_DIM).astype(F32)
    vg_ref[0, 0] = (jnp.where(low_head, av, 0.0) + ones_col).astype(vg_ref.dtype)
    vg_ref[0, 1] = (jnp.where(low_head, pltpu.roll(av, 64, 1), 0.0) + ones_col).astype(vg_ref.dtype)
    tail = jnp.broadcast_to((lane == 0).astype(vd_ref.dtype), (av.shape[0], 128))
    for h in range(DIFF_HEADS):
        vd_ref[0, h, :, 0:128] = p_ref[0, :, 1792 + 128 * h:1792 + 128 * (h + 1)]
        vd_ref[0, h, :, 128:256] = tail


def _attn_prep(p_attn, cos, sin, gains, jmat):
    b, t, w = p_attn.shape
    tr = _pick(t, 544, 16)
    heads = lambda n, width: pl.BlockSpec((1, n, tr, width), lambda i, j: (i, 0, j, 0))
    shape = lambda n, width: jax.ShapeDtypeStruct((b, n, t, width), BF16)
    return pl.pallas_call(
        _prep_kernel,
        grid=(b, t // tr),
        in_specs=[pl.BlockSpec((1, tr, w), lambda i, j: (i, j, 0)),
                  pl.BlockSpec((tr, 128), lambda i, j: (j, 0)),
                  pl.BlockSpec((tr, 128), lambda i, j: (j, 0)),
                  pl.BlockSpec((4, 128), lambda i, j: (0, 0)),
                  pl.BlockSpec((128, 128), lambda i, j: (0, 0))],
        out_specs=[heads(GQA_HEADS, HEAD_DIM), heads(GQA_KV_HEADS, HEAD_DIM),
                   heads(GQA_KV_HEADS, 128),
                   heads(2 * DIFF_HEADS, HEAD_DIM), heads(2 * DIFF_HEADS, HEAD_DIM),
                   heads(DIFF_HEADS, 256)],
        out_shape=[shape(GQA_HEADS, HEAD_DIM), shape(GQA_KV_HEADS, HEAD_DIM),
                   shape(GQA_KV_HEADS, 128),
                   shape(2 * DIFF_HEADS, HEAD_DIM), shape(2 * DIFF_HEADS, HEAD_DIM),
                   shape(DIFF_HEADS, 256)],
        compiler_params=_params(("arbitrary", "arbitrary")),
        name="attn_prep",
    )(p_attn, cos, sin, gains, jmat)


ATT_TQ = 256
ATT_TK = 256


def _softmax_pv(q, k_ref, k_idx, v_ref, v_idx, s_scr, slot, nk, l_col):
    tq, tk = ATT_TQ, ATT_TK
    nchunk = nk // tk
    m_part = jnp.full((tq, 128), -jnp.inf, F32)
    for c in range(nchunk):
        kc = k_ref[0, k_idx, c * tk:(c + 1) * tk, :]
        s = lax.dot_general(q, kc, (((1,), (1,)), ((), ())), preferred_element_type=F32)
        s_scr[slot, :, c * tk:(c + 1) * tk] = s
        for u in range(tk // 128):
            m_part = jnp.maximum(m_part, s[:, u * 128:(u + 1) * 128])
    m = jnp.max(m_part, axis=-1, keepdims=True)
    acc = jnp.zeros((tq, v_ref.shape[-1]), F32)
    for c in range(nchunk):
        p = jnp.exp2(s_scr[slot, :, c * tk:(c + 1) * tk] - m)
        acc = acc + jnp.dot(p.astype(BF16), v_ref[0, v_idx, c * tk:(c + 1) * tk, :],
                            preferred_element_type=F32)
    return acc / acc[:, l_col:l_col + 1]


def _ctx_or_full(n_ctx, t, fn):
    is_ctx_tile = pl.program_id(2) * ATT_TQ < n_ctx

    @pl.when(is_ctx_tile)
    def _():
        fn(n_ctx)

    @pl.when(jnp.logical_not(is_ctx_tile))
    def _():
        fn(t)


ATT_MAPS = 4


def _gqa_kernel(q_ref, k_ref, v_ref, o_ref, s_scr, *, n_ctx, t):
    def attend(nk):
        for u in range(ATT_MAPS):
            o = _softmax_pv(q_ref[0, u], k_ref, 0, v_ref, 0, s_scr, u, nk, HEAD_DIM)
            o_ref[0, u] = o.astype(o_ref.dtype)
    _ctx_or_full(n_ctx, t, attend)


def _diff_kernel(lam_ref, q_ref, k_ref, v_ref, o_ref, s_scr, *, n_ctx, t):
    def attend(nk):
        for h in range(ATT_MAPS // 2):
            o1 = _softmax_pv(q_ref[0, 2 * h], k_ref, 2 * h, v_ref, h, s_scr, 2 * h, nk, 128)
            o2 = _softmax_pv(q_ref[0, 2 * h + 1], k_ref, 2 * h + 1, v_ref, h, s_scr, 2 * h + 1,
                             nk, 128)
            o_ref[0, h] = (o1[:, :128] - lam_ref[0] * o2[:, :128]).astype(o_ref.dtype)
    _ctx_or_full(n_ctx, t, attend)


def _attention_call(body, name, q, k, v, n_out, k_block, v_block, extra_in, extra_args, n_ctx):
    b, n_q, t, _ = q.shape
    assert n_ctx % ATT_TQ == 0 and t % ATT_TQ == 0 and n_ctx % ATT_TK == 0 and t % ATT_TK == 0
    vw = v.shape[-1]
    return pl.pallas_call(
        functools.partial(body, n_ctx=n_ctx, t=t),
        grid=(b, n_q // ATT_MAPS, t // ATT_TQ),
        in_specs=extra_in + [
            pl.BlockSpec((1, ATT_MAPS, ATT_TQ, HEAD_DIM), lambda i, u, j: (i, u, j, 0)),
            pl.BlockSpec((1, k_block, t, HEAD_DIM), lambda i, u, j: (i, u, 0, 0)),
            pl.BlockSpec((1, v_block, t, vw), lambda i, u, j: (i, u, 0, 0))],
        out_specs=pl.BlockSpec((1, n_out, ATT_TQ, 128), lambda i, u, j: (i, u, j, 0)),
        out_shape=jax.ShapeDtypeStruct((b, n_q // ATT_MAPS * n_out, t, 128), BF16),
        scratch_shapes=[pltpu.VMEM((ATT_MAPS, ATT_TQ, t), F32)],
        compiler_params=_params(("arbitrary", "arbitrary", "arbitrary")),
        name=name,
    )(*extra_args, q, k, v)


def _gqa_attention(q, k, v, n_ctx):
    assert GQA_HEADS // GQA_KV_HEADS == ATT_MAPS
    return _attention_call(_gqa_kernel, "attn_gqa", q, k, v, ATT_MAPS, 1, 1, [], [], n_ctx)


def _diff_attention(lam, q, k, v, n_ctx):
    return _attention_call(_diff_kernel, "attn_diff", q, k, v, ATT_MAPS // 2, ATT_MAPS,
                           ATT_MAPS // 2, [pl.BlockSpec(memory_space=pltpu.SMEM)], [lam], n_ctx)


CONV_TILE = 256
CONV_HALO = 16
CONV_ROWS = 32


def _conv_kernel(main_ref, left_ref, right_ref, w_ref, b_ref, lng_ref, lnb_ref, o_ref,
                 u_scr, *, n_ctx, t):
    j = pl.program_id(1)
    row0 = j * CONV_TILE
    has_left = jnp.logical_and(row0 != 0, row0 != n_ctx)
    has_right = jnp.logical_and(row0 + CONV_TILE != n_ctx, row0 + CONV_TILE != t)

    def glu(ref):
        x = ref[0].astype(F32)
        return x[:, :CONV_CH] * jax.nn.sigmoid(x[:, CONV_CH:])

    left = jnp.where(has_left, glu(left_ref), 0.0)
    right = jnp.where(has_right, glu(right_ref), 0.0)
    upad = jnp.concatenate([left, glu(main_ref), right], axis=0)
    n_pad = CONV_TILE + 2 * CONV_HALO
    for r in range(8):
        u_scr[r] = upad if r == 0 else pltpu.roll(upad, n_pad - r, 0)

    pad = CONV_WIDTH // 2

    def rows_body(i, carry):
        base = pl.multiple_of(i * CONV_ROWS, CONV_ROWS)
        acc = jnp.zeros((CONV_ROWS, CONV_CH), F32)
        for tap in range(CONV_WIDTH):
            off = tap + CONV_HALO - pad
            a, r = off // 8, off % 8
            acc = acc + w_ref[tap:tap + 1, :] * u_scr[r, pl.ds(base + 8 * a, CONV_ROWS), :]
        y = acc + b_ref[...]
        mu = jnp.mean(y, axis=-1, keepdims=True)
        yc = y - mu
        var = jnp.mean(yc * yc, axis=-1, keepdims=True)
        z = yc * lax.rsqrt(var + EPS) * lng_ref[...] + lnb_ref[...]
        o_ref[0, pl.ds(base, CONV_ROWS), :] = _silu(z).astype(o_ref.dtype)
        return carry

    lax.fori_loop(0, CONV_TILE // CONV_ROWS, rows_body, 0)


def _conv_branch(p_conv, w, bias, ln_g, ln_b, n_ctx):
    b, t, wd = p_conv.shape
    assert n_ctx % CONV_TILE == 0 and t % CONV_TILE == 0
    hb = CONV_TILE // CONV_HALO
    n_halo_blocks = t // CONV_HALO
    vec = pl.BlockSpec((1, CONV_CH), lambda i, j: (0, 0))
    return pl.pallas_call(
        functools.partial(_conv_kernel, n_ctx=n_ctx, t=t),
        grid=(b, t // CONV_TILE),
        in_specs=[pl.BlockSpec((1, CONV_TILE, wd), lambda i, j: (i, j, 0)),
                  pl.BlockSpec((1, CONV_HALO, wd),
                               lambda i, j: (i, jnp.maximum(j * hb - 1, 0), 0)),
                  pl.BlockSpec((1, CONV_HALO, wd),
                               lambda i, j: (i, jnp.minimum((j + 1) * hb, n_halo_blocks - 1), 0)),
                  pl.BlockSpec((CONV_WIDTH, CONV_CH), lambda i, j: (0, 0)),
                  vec, vec, vec],
        out_specs=pl.BlockSpec((1, CONV_TILE, CONV_CH), lambda i, j: (i, j, 0)),
        out_shape=jax.ShapeDtypeStruct((b, t, CONV_CH), BF16),
        scratch_shapes=[pltpu.VMEM((8, CONV_TILE + 2 * CONV_HALO, CONV_CH), F32)],
        compiler_params=_params(("arbitrary", "arbitrary")),
        name="conv",
    )(p_conv, p_conv, p_conv, w, bias.reshape(1, -1), ln_g.reshape(1, -1), ln_b.reshape(1, -1))


HG_TILE = 256
HG_SUB = 16
HG_ROWS = 64
HG_PAIRS = HGRN_HEADS // 2
HG_PAD = 8
HG_W = HGRN_HEADS * HGRN_DK


def _dot3(a_bf16, x):
    x1 = x.astype(BF16)
    r1 = x - x1.astype(F32)
    x2 = r1.astype(BF16)
    x3 = (r1 - x2.astype(F32)).astype(BF16)
    out = jnp.dot(a_bf16, x1, preferred_element_type=F32)
    out = out + jnp.dot(a_bf16, x2, preferred_element_type=F32)
    return out + jnp.dot(a_bf16, x3, preferred_element_type=F32)


def _hgrn_kernel(q_ref, z_ref, v_ref, la_ref, lb0_ref, tri_ref, j_ref, hm_ref, o_ref,
                 st_scr, q_scr, k_scr, c_scr, v_scr, kp_scr, cp_scr, vp_scr,
                 ks_scr, cs_scr, vs_scr, acc_scr, *, rev):
    c_sz = HG_SUB
    n_chunk = HG_TILE // c_sz

    @pl.when(pl.program_id(1) == 0)
    def _():
        st_scr[...] = jnp.zeros_like(st_scr)

    q = _silu(q_ref[0].astype(F32)) * (HGRN_DK ** -0.5)
    z = z_ref[0]
    v = v_ref[0].astype(F32)
    log_sig = jnp.minimum(z, 0.0) - jnp.log(1.0 + jnp.exp(-jnp.abs(z)))
    t1 = la_ref[...]
    t2 = lb0_ref[...] + log_sig
    logf = jnp.maximum(t1, t2) + jnp.log(1.0 + jnp.exp(-jnp.abs(t1 - t2)))
    k = 1.0 - jnp.exp(logf)
    c2 = _dot3(tri_ref[...], logf * LOG2E)
    q_scr[...] = q
    k_scr[...] = k
    c_scr[...] = c2
    v_scr[...] = v
    for ref, val in ((kp_scr, k), (cp_scr, c2), (vp_scr, v)):
        ref[0:HG_PAD, :] = jnp.zeros((HG_PAD, HG_W), F32)
        ref[HG_PAD:HG_PAD + HG_TILE, :] = val
        ref[HG_PAD + HG_TILE:HG_TILE + 2 * HG_PAD, :] = jnp.zeros((HG_PAD, HG_W), F32)

    order = range(n_chunk - 1, -1, -1) if rev else range(n_chunk)
    for ci in order:
        lo = ci * c_sz
        cj = c_scr[lo:lo + c_sz, :]
        c_end = cj[0:1] if rev else cj[c_sz - 1:c_sz]
        qd = (q_scr[lo:lo + c_sz, :] * jnp.exp2(cj)).astype(BF16)
        kd = (k_scr[lo:lo + c_sz, :] * jnp.exp2(c_end - cj)).astype(BF16)
        vb = v_scr[lo:lo + c_sz, :].astype(BF16)
        dec = jnp.exp2(c_end)
        outs = []
        for p in range(HG_PAIRS):
            sl = slice(128 * p, 128 * (p + 1))
            st = st_scr[p]
            outs.append(lax.dot_general(qd[:, sl], st.astype(BF16), (((1,), (1,)), ((), ())),
                                        preferred_element_type=F32))
            upd = lax.dot_general(vb[:, sl], kd[:, sl], (((0,), (0,)), ((), ())),
                                  preferred_element_type=F32)
            st_scr[p] = st * dec[:, sl] + upd * hm_ref[...]
        acc_scr[lo:lo + c_sz, :] = jnp.concatenate(outs, axis=1)

    row_in_chunk = lax.broadcasted_iota(jnp.int32, (HG_TILE, 1), 0) % c_sz

    def shifted(ref, r, mask):
        start = HG_PAD + r if rev else HG_PAD - r
        y = ref[start:start + HG_TILE, :]
        if not mask or r == 0:
            return y
        keep = (row_in_chunk < c_sz - r) if rev else (row_in_chunk >= r)
        return jnp.where(keep, y, 0.0)

    for r in range(8):
        ks_scr[...] = shifted(kp_scr, r, True)
        cs_scr[...] = shifted(cp_scr, r, True)
        vs_scr[...] = shifted(vp_scr, r, False)
        half = HG_W // 2
        for blk in range(HG_TILE // HG_ROWS):
            base = blk * HG_ROWS
            for a in range(c_sz // 8):
                n = c_sz - 8 * a
                d_lo = 0 if rev else 8 * a
                s_lo = 8 * a if rev else 0
                starts = [base + ci * c_sz for ci in range(HG_ROWS // c_sz)]

                def rows(ref, lo_in_chunk):
                    if n == c_sz:
                        return ref[base:base + HG_ROWS, :]
                    return jnp.concatenate(
                        [ref[s + lo_in_chunk:s + lo_in_chunk + n, :] for s in starts], axis=0)

                qa = rows(q_scr, d_lo)
                ca = rows(c_scr, d_lo)
                kk = rows(ks_scr, s_lo)
                cc = rows(cs_scr, s_lo)
                vv = rows(vs_scr, s_lo)
                pw = (qa * kk * jnp.exp2(ca - cc)).astype(BF16)
                m = pw.shape[0]
                att2 = jnp.dot(jnp.concatenate([pw[:, :half], pw[:, half:]], axis=0), j_ref[...],
                               preferred_element_type=F32)
                contrib = jnp.concatenate([att2[:m], att2[m:]], axis=1) * vv
                if n == c_sz:
                    acc_scr[base:base + HG_ROWS, :] = acc_scr[base:base + HG_ROWS, :] + contrib
                else:
                    for i, s in enumerate(starts):
                        dst = s + d_lo
                        acc_scr[dst:dst + n, :] = acc_scr[dst:dst + n, :] + contrib[i * n:(i + 1) * n]

    o_ref[0] = acc_scr[...]


def _hgrn_dir(p_hq, p_hf, p_hig, la, lb0, tri, jmat, head_mask, n_ctx, rev):
    b, t, _ = p_hq.shape
    assert n_ctx % HG_TILE == 0 and t % HG_TILE == 0
    n_tiles = t // HG_TILE
    n_ctx_tiles = n_ctx // HG_TILE
    col = 1 if rev else 0

    def tile(j):
        if not rev:
            return j
        return jnp.where(j < n_ctx_tiles, n_ctx_tiles - 1 - j, n_tiles - 1 - (j - n_ctx_tiles))

    const = lambda shape: pl.BlockSpec(shape, lambda i, j: (0,) * len(shape))
    return pl.pallas_call(
        functools.partial(_hgrn_kernel, rev=rev),
        grid=(b, n_tiles),
        in_specs=[pl.BlockSpec((1, HG_TILE, HG_W), lambda i, j: (i, tile(j), 0)),
                  pl.BlockSpec((1, HG_TILE, HG_W), lambda i, j: (i, tile(j), col)),
                  pl.BlockSpec((1, HG_TILE, HG_W), lambda i, j: (i, tile(j), 0)),
                  const((1, HG_W)),
                  const((1, HG_W)),
                  const((HG_TILE, HG_TILE)), const((HG_W // 2, HG_W // 2)), const((128, 128))],
        out_specs=pl.BlockSpec((1, HG_TILE, HG_W), lambda i, j: (i, tile(j), 0)),
        out_shape=jax.ShapeDtypeStruct((b, t, HG_W), F32),
        scratch_shapes=[pltpu.VMEM((HG_PAIRS, 128, 128), F32)]
                       + [pltpu.VMEM((HG_TILE, HG_W), F32) for _ in range(4)]
                       + [pltpu.VMEM((HG_TILE + 2 * HG_PAD, HG_W), F32) for _ in range(3)]
                       + [pltpu.VMEM((HG_TILE, HG_W), F32) for _ in range(4)],
        compiler_params=_params(("arbitrary", "arbitrary")),
        name="hgrn_bwd" if rev else "hgrn_fwd",
    )(p_hq, p_hf, p_hig, la[col:col + 1], lb0[col:col + 1], tri, jmat, head_mask)


def _merge_kernel(og_ref, od_ref, cv_ref, hf_ref, hb_ref, hg_ref, gt_ref, x_ref, wb_ref, wo_ref,
                  sub_ref, hng_ref, j_ref, gc_ref, gx_ref, o_ref, *, tr, n_ctx, lam_init):
    gates = gt_ref[0]

    def gate(i):
        return jax.nn.sigmoid(gates[:, i * 1024:(i + 1) * 1024].astype(F32))

    pa = jnp.zeros((tr, 1024), F32)
    for h in range(GQA_HEADS):
        pa = pa + jnp.dot(og_ref[0, h][:, :HEAD_DIM],
                          wb_ref[0, h * HEAD_DIM:(h + 1) * HEAD_DIM, :],
                          preferred_element_type=F32)
    merged = gate(0) * pa
    pb = jnp.zeros((tr, 1024), F32)
    for h in range(DIFF_HEADS):
        od = od_ref[0, h].astype(F32)
        ms = jnp.mean(od * od, axis=-1, keepdims=True)
        on = od * lax.rsqrt(ms + EPS) * sub_ref[...] * (1.0 - lam_init)
        pb = pb + jnp.dot(on.astype(BF16), wb_ref[1, h * 128:(h + 1) * 128, :],
                          preferred_element_type=F32)
    merged = merged + gate(1) * pb
    merged = merged + gate(2) * jnp.dot(cv_ref[0], wb_ref[2], preferred_element_type=F32)
    od = hf_ref[0] + hb_ref[0]
    ms = jnp.dot((od * od).astype(BF16), j_ref[...], preferred_element_type=F32)
    on = od * lax.rsqrt(ms + EPS) * hng_ref[...] * _silu(hg_ref[0].astype(F32))
    merged = merged + gate(3) * jnp.dot(on.astype(BF16), wb_ref[3], preferred_element_type=F32)

    mix = jnp.dot(merged.astype(BF16), wo_ref[...], preferred_element_type=F32)
    g1 = jnp.where(_row_is_ctx(pl.program_id(1), tr, n_ctx), gc_ref[...], gx_ref[0])
    o_ref[0] = x_ref[0] + g1 * mix


def _merge(o_gqa, o_diff, conv_o, hg_f, hg_b, p_hig, p_gate, x, wb, wo, subln_g, hgrn_g, jmat64,
           g_c, g_x, n_ctx, lam_init):
    b, t, d = x.shape
    tr = _pick(t, 272, 16)
    const = lambda shape: pl.BlockSpec(shape, lambda i, j: (0,) * len(shape))
    return pl.pallas_call(
        functools.partial(_merge_kernel, tr=tr, n_ctx=n_ctx, lam_init=lam_init),
        grid=(b, t // tr),
        in_specs=[pl.BlockSpec((1, GQA_HEADS, tr, 128), lambda i, j: (i, 0, j, 0)),
                  pl.BlockSpec((1, DIFF_HEADS, tr, 128), lambda i, j: (i, 0, j, 0)),
                  pl.BlockSpec((1, tr, CONV_CH), lambda i, j: (i, j, 0)),
                  pl.BlockSpec((1, tr, HG_W), lambda i, j: (i, j, 0)),
                  pl.BlockSpec((1, tr, HG_W), lambda i, j: (i, j, 0)),
                  pl.BlockSpec((1, tr, HG_W), lambda i, j: (i, j, 1)),
                  pl.BlockSpec((1, tr, 4 * d), lambda i, j: (i, j, 0)),
                  pl.BlockSpec((1, tr, d), lambda i, j: (i, j, 0)),
                  const((4, BRANCH_W, d)), const((d, d)),
                  const((1, 128)), const((1, HG_W)), const((HG_W, HG_W)),
                  const((1, d)),
                  pl.BlockSpec((1, 1, d), lambda i, j: (i, 0, 0))],
        out_specs=pl.BlockSpec((1, tr, d), lambda i, j: (i, j, 0)),
        out_shape=jax.ShapeDtypeStruct((b, t, d), F32),
        compiler_params=_params(("arbitrary", "arbitrary")),
        name="merge",
    )(o_gqa, o_diff, conv_o, hg_f, hg_b, p_hig, p_gate, x, wb, wo, subln_g, hgrn_g, jmat64,
      g_c, g_x)


def _swiglu_step(h, w1, w3, w2):
    a = jnp.dot(h, w1, preferred_element_type=F32)
    g = jnp.dot(h, w3, preferred_element_type=F32)
    return jnp.dot((_silu(a) * g).astype(BF16), w2, preferred_element_type=F32)


def _ffn_kernel(h_ref, w1_ref, w3_ref, w2_ref, x_ref, gc_ref, gx_ref, o_ref, acc_scr, *, tr, n_ctx):
    kf = pl.program_id(2)

    @pl.when(kf == 0)
    def _():
        acc_scr[...] = jnp.zeros_like(acc_scr)

    acc_scr[...] += _swiglu_step(h_ref[0], w1_ref[...], w3_ref[...], w2_ref[...])

    @pl.when(kf == pl.num_programs(2) - 1)
    def _():
        g2 = jnp.where(_row_is_ctx(pl.program_id(1), tr, n_ctx), gc_ref[...], gx_ref[0])
        o_ref[0] = x_ref[0] + g2 * acc_scr[...]


def _ffn(h, w1, w3, w2, x, g_c, g_x, n_ctx):
    b, t, d = x.shape
    dff = w1.shape[1]
    tr = _pick(t, 1088, 16)
    tf = _pick(dff, 512, 128)
    return pl.pallas_call(
        functools.partial(_ffn_kernel, tr=tr, n_ctx=n_ctx),
        grid=(b, t // tr, dff // tf),
        in_specs=[pl.BlockSpec((1, tr, d), lambda i, j, f: (i, j, 0)),
                  pl.BlockSpec((d, tf), lambda i, j, f: (0, f)),
                  pl.BlockSpec((d, tf), lambda i, j, f: (0, f)),
                  pl.BlockSpec((tf, d), lambda i, j, f: (f, 0)),
                  pl.BlockSpec((1, tr, d), lambda i, j, f: (i, j, 0)),
                  pl.BlockSpec((1, d), lambda i, j, f: (0, 0)),
                  pl.BlockSpec((1, 1, d), lambda i, j, f: (i, 0, 0))],
        out_specs=pl.BlockSpec((1, tr, d), lambda i, j, f: (i, j, 0)),
        out_shape=jax.ShapeDtypeStruct((b, t, d), F32),
        scratch_shapes=[pltpu.VMEM((tr, d), F32)],
        input_output_aliases={4: 0},
        compiler_params=_params(("arbitrary", "arbitrary", "arbitrary")),
        name="ffn",
    )(h, w1, w3, w2, x, g_c, g_x)


MOE_TILE = 512
GATHER_ROWS = 2048
COMBINE_ROWS = 544


def _gather_kernel(idx_ref, src_ref, o_ref, sems, *, rows):
    i = pl.program_id(0)
    slot = i % 2

    def start(r, carry):
        pltpu.make_async_copy(src_ref.at[pl.ds(idx_ref[0, 0, r], 1), :],
                              o_ref.at[pl.ds(i * rows + r, 1), :], sems.at[slot]).start()
        return carry

    def wait_all(which):
        def wait(r, carry):
            pltpu.make_async_copy(src_ref.at[pl.ds(0, 1), :], o_ref.at[pl.ds(0, 1), :],
                                  sems.at[which]).wait()
            return carry
        lax.fori_loop(0, rows, wait, 0)

    lax.fori_loop(0, rows, start, 0)

    @pl.when(i > 0)
    def _():
        wait_all(1 - slot)

    @pl.when(i == pl.num_programs(0) - 1)
    def _():
        wait_all(slot)


def _gather_rows(src_rows, idx):
    n_out = idx.shape[0]
    d = src_rows.shape[1]
    rows = _pick(n_out, GATHER_ROWS, 8)
    steps = n_out // rows
    return pl.pallas_call(
        functools.partial(_gather_kernel, rows=rows),
        grid=(steps,),
        in_specs=[pl.BlockSpec((1, 1, rows), lambda i: (i, 0, 0), memory_space=pltpu.SMEM),
                  pl.BlockSpec(memory_space=pl.ANY)],
        out_specs=pl.BlockSpec(memory_space=pl.ANY),
        out_shape=jax.ShapeDtypeStruct((n_out, d), src_rows.dtype),
        scratch_shapes=[pltpu.SemaphoreType.DMA((2,))],
        compiler_params=_params(("arbitrary",)),
        name="moe_gather",
    )(idx.reshape(steps, 1, rows), src_rows)


def _expert_kernel(te_ref, nu_ref, x_ref, w1_ref, w3_ref, w2_ref, o_ref, acc_scr):
    i = pl.program_id(0)
    kf = pl.program_id(1)
    last = kf == pl.num_programs(1) - 1
    used = i < nu_ref[0]

    @pl.when(jnp.logical_and(used, kf == 0))
    def _():
        acc_scr[...] = jnp.zeros_like(acc_scr)

    @pl.when(used)
    def _():
        acc_scr[...] += _swiglu_step(x_ref[...].astype(BF16), w1_ref[0], w3_ref[0], w2_ref[0])

    @pl.when(jnp.logical_and(used, last))
    def _():
        o_ref[...] = acc_scr[...]

    @pl.when(jnp.logical_and(jnp.logical_not(used), last))
    def _():
        o_ref[...] = jnp.zeros_like(o_ref)


def _expert_ffn(xs, w1, w3, w2, tile_expert, n_used):
    n_sorted, d = xs.shape
    dff = w1.shape[2]
    tf = _pick(dff, 896, 128)
    grid_spec = pltpu.PrefetchScalarGridSpec(
        num_scalar_prefetch=2,
        grid=(n_sorted // MOE_TILE, dff // tf),
        in_specs=[pl.BlockSpec((MOE_TILE, d), lambda i, f, te, nu: (i, 0)),
                  pl.BlockSpec((1, d, tf), lambda i, f, te, nu: (te[i], 0, f)),
                  pl.BlockSpec((1, d, tf), lambda i, f, te, nu: (te[i], 0, f)),
                  pl.BlockSpec((1, tf, d), lambda i, f, te, nu: (te[i], f, 0))],
        out_specs=pl.BlockSpec((MOE_TILE, d), lambda i, f, te, nu: (i, 0)),
        scratch_shapes=[pltpu.VMEM((MOE_TILE, d), F32)])
    return pl.pallas_call(
        _expert_kernel,
        grid_spec=grid_spec,
        out_shape=jax.ShapeDtypeStruct((n_sorted, d), F32),
        compiler_params=_params(("arbitrary", "arbitrary")),
        name="moe_experts",
    )(tile_expert, n_used, xs, w1, w3, w2)


def _combine_kernel(rw_ref, y1_ref, y2_ref, x_ref, gc_ref, gx_ref, o_ref, *, tr, n_ctx):
    rw = rw_ref[0]
    f = rw[:, 2:3] * y1_ref[0, 0] + rw[:, 3:4] * y2_ref[0, 0]
    g2 = jnp.where(_row_is_ctx(pl.program_id(1), tr, n_ctx), gc_ref[...], gx_ref[0])
    o_ref[0] = x_ref[0] + g2 * f


def _combine(route, y_tok, x, g_c, g_x, n_ctx):
    b, t, d = x.shape
    tr = _pick(t, COMBINE_ROWS, 16)
    return pl.pallas_call(
        functools.partial(_combine_kernel, tr=tr, n_ctx=n_ctx),
        grid=(b, t // tr),
        in_specs=[pl.BlockSpec((1, tr, 128), lambda i, j: (i, j, 0)),
                  pl.BlockSpec((1, 1, tr, d), lambda i, j: (0, i, j, 0)),
                  pl.BlockSpec((1, 1, tr, d), lambda i, j: (1, i, j, 0)),
                  pl.BlockSpec((1, tr, d), lambda i, j: (i, j, 0)),
                  pl.BlockSpec((1, d), lambda i, j: (0, 0)),
                  pl.BlockSpec((1, 1, d), lambda i, j: (i, 0, 0))],
        out_specs=pl.BlockSpec((1, tr, d), lambda i, j: (i, j, 0)),
        out_shape=jax.ShapeDtypeStruct((b, t, d), F32),
        input_output_aliases={3: 0},
        compiler_params=_params(("arbitrary", "arbitrary")),
        name="moe_combine",
    )(route, y_tok, y_tok, x, g_c, g_x)


def _moe_plan(e1, e2):
    n = e1.shape[0]
    e = jnp.concatenate([e1, e2])
    onehot = (e[:, None] == jnp.arange(N_EXPERTS, dtype=jnp.int32)[None, :]).astype(jnp.int32)
    counts = jnp.sum(onehot, axis=0)
    rank = jnp.sum((jnp.cumsum(onehot, axis=0) - onehot) * onehot, axis=1)
    padded = ((counts + MOE_TILE - 1) // MOE_TILE) * MOE_TILE
    ends = jnp.cumsum(padded)
    slot = jnp.sum((ends - padded)[None, :] * onehot, axis=1) + rank
    n_tiles = (2 * n + N_EXPERTS * (MOE_TILE - 1)) // MOE_TILE
    token = jnp.tile(jnp.arange(n, dtype=jnp.int32), 2)
    tokens_sorted = jnp.zeros((n_tiles * MOE_TILE,), jnp.int32).at[slot].set(token)
    tile_start = jnp.arange(n_tiles, dtype=jnp.int32) * MOE_TILE
    tile_expert = jnp.minimum(jnp.sum((ends[None, :] <= tile_start[:, None]).astype(jnp.int32), axis=1),
                              N_EXPERTS - 1)
    n_used = (ends[-1] // MOE_TILE).reshape(1)
    return slot[:n], slot[n:], tokens_sorted, tile_expert, n_used


def _moe(h_f32, route, w1, w3, w2, x, g_c, g_x, n_ctx):
    b, t, d = x.shape
    e1 = route[:, :, 0].reshape(-1).astype(jnp.int32)
    e2 = route[:, :, 1].reshape(-1).astype(jnp.int32)
    pos1, pos2, tokens_sorted, tile_expert, n_used = _moe_plan(e1, e2)
    xs = _gather_rows(h_f32.reshape(b * t, d), tokens_sorted)
    ys = _expert_ffn(xs, w1, w3, w2, tile_expert, n_used)
    y_tok = _gather_rows(ys, jnp.concatenate([pos1, pos2])).reshape(2, b, t, d)
    return _combine(route, y_tok, x, g_c, g_x, n_ctx)


def _router_kernel(h_ref, r_ref, o_ref):
    logits = jnp.dot(h_ref[0], r_ref[...], preferred_element_type=F32,
                     precision=lax.Precision.HIGHEST)
    lane = lax.broadcasted_iota(jnp.int32, logits.shape, 1).astype(F32)
    neg = -jnp.inf
    logits = jnp.where(lane < N_EXPERTS, logits, neg)
    m1 = jnp.max(logits, axis=-1, keepdims=True)
    i1 = jnp.min(jnp.where(logits == m1, lane, 128.0), axis=-1, keepdims=True)
    rest = jnp.where(lane == i1, neg, logits)
    m2 = jnp.max(rest, axis=-1, keepdims=True)
    i2 = jnp.min(jnp.where(rest == m2, lane, 128.0), axis=-1, keepdims=True)
    e2 = jnp.exp(m2 - m1)
    w1 = 1.0 / (1.0 + e2)
    w2 = e2 / (1.0 + e2)
    o_ref[0] = (jnp.where(lane == 0.0, i1, 0.0) + jnp.where(lane == 1.0, i2, 0.0)
                + jnp.where(lane == 2.0, w1, 0.0) + jnp.where(lane == 3.0, w2, 0.0))


def _router(h_f32, router_pad):
    b, t, d = h_f32.shape
    tr = _pick(t, 1088, 16)
    return pl.pallas_call(
        _router_kernel,
        grid=(b, t // tr),
        in_specs=[pl.BlockSpec((1, tr, d), lambda i, j: (i, j, 0)),
                  pl.BlockSpec((d, 128), lambda i, j: (0, 0))],
        out_specs=pl.BlockSpec((1, tr, 128), lambda i, j: (i, j, 0)),
        out_shape=jax.ShapeDtypeStruct((b, t, 128), F32),
        compiler_params=_params(("arbitrary", "arbitrary")),
        name="router",
    )(h_f32, router_pad)


def _rope_tables(seq, n_ctx):
    rows = seq // GRID_W
    row = jnp.repeat(jnp.arange(rows, dtype=F32), GRID_W)
    col = (jnp.arange(seq) % GRID_W).astype(F32)
    axis_dim = HEAD_DIM // 2
    inv_freq = ROPE_THETA ** (-jnp.arange(0, axis_dim, 2, dtype=F32) / axis_dim)
    ang_r = row[:, None] * inv_freq
    ang_c = col[:, None] * inv_freq
    cos = jnp.concatenate([jnp.cos(ang_r), jnp.cos(ang_r), jnp.cos(ang_c), jnp.cos(ang_c)], axis=-1)
    sin = jnp.concatenate([-jnp.sin(ang_r), jnp.sin(ang_r), -jnp.sin(ang_c), jnp.sin(ang_c)], axis=-1)
    cos = jnp.concatenate([jnp.ones((n_ctx, HEAD_DIM), F32), cos], axis=0)
    sin = jnp.concatenate([jnp.zeros((n_ctx, HEAD_DIM), F32), sin], axis=0)
    return jnp.tile(cos, (1, 2)), jnp.tile(sin, (1, 2))


def _block_ones(n, blk, value=1.0, dtype=BF16):
    i = np.arange(n) // blk
    return jnp.asarray((i[:, None] == i[None, :]).astype(np.float32) * value, dtype)


def _chunk_tri(n, chunk, rev):
    i = np.arange(n)
    same = (i[:, None] // chunk) == (i[None, :] // chunk)
    tri = (i[None, :] >= i[:, None]) if rev else (i[None, :] <= i[:, None])
    return jnp.asarray((same & tri).astype(np.float32), BF16)


def kernel(x, c, ctx, c_ctx, ada_w, ada_b, norm1_g, norm2_g, w_in, qk_norm_g, diff_lambda,
           diff_subln_g, conv_w, conv_b, conv_ln_g, conv_ln_b, hgrn_lb_logits, hgrn_norm_g,
           w_branch, w_out, ffn_w1, ffn_w3, ffn_w2, moe_router, moe_w1, moe_w3, moe_w2):
    b, seq, d = x.shape
    n_ctx = ctx.shape[1]
    t = n_ctx + seq
    depth = ada_w.shape[0]

    cos, sin = _rope_tables(seq, n_ctx)
    j_head_mean = _block_ones(128, HEAD_DIM, 1.0 / HEAD_DIM)
    j_hg_sum = _block_ones(HG_W // 2, HGRN_DK)
    j_hg_mean = _block_ones(HG_W, HGRN_DK, 1.0 / HGRN_DK)
    head_mask = _block_ones(128, HGRN_DK, 1.0, F32)
    tri_f = _chunk_tri(HG_TILE, HG_SUB, False)
    tri_b = _chunk_tri(HG_TILE, HG_SUB, True)

    lb = jnp.cumsum(jax.nn.softmax(hgrn_lb_logits.astype(F32), axis=0), axis=0)
    lb = lb - lb[:1]
    log_lb = jnp.log(lb)
    log_1m_lb = jnp.log1p(-lb)

    rows = ((b + 1 + 7) // 8) * 8
    cvec = jnp.zeros((rows, d), F32).at[:b].set(c).at[b].set(c_ctx)
    mod = _ada(cvec, ada_w, ada_b).reshape(depth, rows, 6, d)

    xc = jnp.concatenate([ctx, x], axis=1)
    q_scale = LOG2E * HEAD_DIM ** -0.5

    for l in range(depth):
        mx = mod[l, :b]
        mc = mod[l, b]
        vx = lambda i: mx[:, i:i + 1, :]
        vc = lambda i: mc[i:i + 1, :]
        w_l = w_in[l].astype(BF16)

        h = _normmod(xc, norm1_g[l], vc(0), vc(1), vx(0), vx(1), n_ctx).reshape(b * t, d)
        proj = lambda cols, dt: _matmul(h, w_l[:, cols[0]:cols[1]], dt).reshape(b, t, -1)
        p_attn = proj(COL_ATTN, BF16)
        p_conv = proj(COL_CONV, BF16)
        p_hq = proj(COL_HQ, BF16)
        p_hf = proj(COL_HF, F32)
        p_hig = proj(COL_HIG, BF16)
        p_gate = proj(COL_GATE, BF16)

        g = qk_norm_g[l].astype(F32)
        gains = jnp.tile(jnp.stack([g[0] * q_scale, g[1], g[2] * q_scale, g[3]]), (1, 2))
        lam_init = 0.8 - 0.6 * math.exp(-0.3 * l)
        lp = diff_lambda[l].astype(F32)
        lam = (jnp.exp(jnp.sum(lp[0] * lp[1])) - jnp.exp(jnp.sum(lp[2] * lp[3]))
               + lam_init).reshape(1)
        qg, kg, vg, qd, kd, vd = _attn_prep(p_attn, cos, sin, gains, j_head_mean)
        o_gqa = _gqa_attention(qg, kg, vg, n_ctx)
        o_diff = _diff_attention(lam, qd, kd, vd, n_ctx)

        conv_o = _conv_branch(p_conv, conv_w[l], conv_b[l], conv_ln_g[l], conv_ln_b[l], n_ctx)

        hg_f = _hgrn_dir(p_hq, p_hf, p_hig, log_lb[l], log_1m_lb[l], tri_f, j_hg_sum, head_mask,
                         n_ctx, False)
        hg_b = _hgrn_dir(p_hq, p_hf, p_hig, log_lb[l], log_1m_lb[l], tri_b, j_hg_sum, head_mask,
                         n_ctx, True)

        xc = _merge(o_gqa, o_diff, conv_o, hg_f, hg_b, p_hig, p_gate, xc,
                    w_branch[l].astype(BF16), w_out[l].astype(BF16),
                    diff_subln_g[l].reshape(1, -1), jnp.tile(hgrn_norm_g[l], HGRN_HEADS).reshape(1, -1),
                    j_hg_mean, vc(2), vx(2), n_ctx, lam_init)

        if l % 2 == 0:
            h2 = _normmod(xc, norm2_g[l], vc(3), vc(4), vx(3), vx(4), n_ctx)
            i = l // 2
            xc = _ffn(h2, ffn_w1[i].astype(BF16), ffn_w3[i].astype(BF16), ffn_w2[i].astype(BF16),
                      xc, vc(5), vx(5), n_ctx)
        else:
            h2f = _normmod(xc, norm2_g[l], vc(3), vc(4), vx(3), vx(4), n_ctx, out_dtype=F32)
            i = l // 2
            router_pad = jnp.zeros((d, 128), F32).at[:, :N_EXPERTS].set(moe_router[i])
            route = _router(h2f, router_pad)
            xc = _moe(h2f, route, moe_w1[i].astype(BF16), moe_w3[i].astype(BF16),
                      moe_w2[i].astype(BF16), xc, vc(5), vx(5), n_ctx)
    return xc[:, n_ctx:]
```

```python
import functools
import math

import jax
import jax.numpy as jnp
import numpy as np
from jax import lax
from jax.experimental import pallas as pl
from jax.experimental.pallas import tpu as pltpu

F32 = jnp.float32
BF16 = jnp.bfloat16

GRID_W = 64
HEAD_DIM = 64
ROPE_THETA = 10000.0
EPS = 1e-6
GQA_HEADS = 8
GQA_KV_HEADS = 2
DIFF_HEADS = 4
CONV_CH = 512
CONV_WIDTH = 31
HGRN_HEADS = 8
HGRN_DK = 64
HGRN_CHUNK = 64
BRANCH_W = 512
N_EXPERTS = 8
LOG2E = 1.4426950408889634

COL_ATTN = (0, 2304)
COL_CONV = (2304, 3328)
COL_HQ = (3328, 3840)
COL_HF = (3840, 4864)
COL_HIG = (4864, 5888)
COL_GATE = (5888, 9984)

VMEM_LIMIT_BYTES = 50 * 1024 * 1024


def _params(sem):
    return pltpu.CompilerParams(dimension_semantics=sem, vmem_limit_bytes=VMEM_LIMIT_BYTES)


def _pick(n, target, mult):
    best = None
    for d in range(mult, min(n, target) + 1, mult):
        if n % d == 0:
            best = d
    assert best is not None, (n, target, mult)
    return best


def _silu(x):
    return x * jax.nn.sigmoid(x)


def _row_is_ctx(tile_idx, tr, n_ctx):
    row = tile_idx * tr + lax.broadcasted_iota(jnp.int32, (tr, 1), 0)
    return row < n_ctx


def _ada_kernel(c_ref, w_ref, b_ref, o_ref):
    s = _silu(c_ref[...])
    o_ref[0] = jnp.dot(s.astype(BF16), w_ref[0].astype(BF16),
                       preferred_element_type=F32) + b_ref[0]


def _ada(cvec, ada_w, ada_b):
    depth, d, n = ada_w.shape
    rows = cvec.shape[0]
    tn = _pick(n, 1536, 128)
    return pl.pallas_call(
        _ada_kernel,
        grid=(depth, n // tn),
        in_specs=[pl.BlockSpec((rows, d), lambda l, j: (0, 0)),
                  pl.BlockSpec((1, d, tn), lambda l, j: (l, 0, j)),
                  pl.BlockSpec((1, 1, tn), lambda l, j: (l, 0, j))],
        out_specs=pl.BlockSpec((1, rows, tn), lambda l, j: (l, 0, j)),
        out_shape=jax.ShapeDtypeStruct((depth, rows, n), F32),
        compiler_params=_params(("arbitrary", "arbitrary")),
        name="ada",
    )(cvec, ada_w, ada_b.reshape(depth, 1, n))


def _normmod_kernel(x_ref, g_ref, shc_ref, scc_ref, shx_ref, scx_ref, o_ref, *, tr, n_ctx):
    x = x_ref[0]
    ms = jnp.mean(x * x, axis=-1, keepdims=True)
    y = x * lax.rsqrt(ms + EPS) * g_ref[...]
    is_ctx = _row_is_ctx(pl.program_id(1), tr, n_ctx)
    sh = jnp.where(is_ctx, shc_ref[...], shx_ref[0])
    sc = jnp.where(is_ctx, scc_ref[...], scx_ref[0])
    o_ref[0] = (y * (1.0 + sc) + sh).astype(o_ref.dtype)


def _normmod(x, g, sh_c, sc_c, sh_x, sc_x, n_ctx, out_dtype=BF16):
    b, t, d = x.shape
    tr = _pick(t, 1088, 16)
    vec_c = pl.BlockSpec((1, d), lambda i, j: (0, 0))
    vec_x = pl.BlockSpec((1, 1, d), lambda i, j: (i, 0, 0))
    return pl.pallas_call(
        functools.partial(_normmod_kernel, tr=tr, n_ctx=n_ctx),
        grid=(b, t // tr),
        in_specs=[pl.BlockSpec((1, tr, d), lambda i, j: (i, j, 0)),
                  vec_c, vec_c, vec_c, vec_x, vec_x],
        out_specs=pl.BlockSpec((1, tr, d), lambda i, j: (i, j, 0)),
        out_shape=jax.ShapeDtypeStruct((b, t, d), out_dtype),
        compiler_params=_params(("arbitrary", "arbitrary")),
        name="normmod",
    )(x, g.reshape(1, d), sh_c, sc_c, sh_x, sc_x)


def _mm_kernel(a_ref, w_ref, o_ref):
    o_ref[...] = jnp.dot(a_ref[...], w_ref[...],
                         preferred_element_type=F32).astype(o_ref.dtype)


def _matmul(a, w, out_dtype):
    m, k = a.shape
    n = w.shape[1]
    tm = _pick(m, 2048, 256)
    tn = _pick(n, 1024, 128)
    return pl.pallas_call(
        _mm_kernel,
        grid=(m // tm, n // tn),
        in_specs=[pl.BlockSpec((tm, k), lambda i, j: (i, 0)),
                  pl.BlockSpec((k, tn), lambda i, j: (0, j))],
        out_specs=pl.BlockSpec((tm, tn), lambda i, j: (i, j)),
        out_shape=jax.ShapeDtypeStruct((m, n), out_dtype),
        compiler_params=_params(("arbitrary", "arbitrary")),
        name="proj",
    )(a, w)


def _prep_kernel(p_ref, cos_ref, sin_ref, g_ref, j_ref, qg_ref, kg_ref, vg_ref,
                 qd_ref, kd_ref, vd_ref):
    cos = cos_ref[...]
    sin = sin_ref[...]
    lane = lax.broadcasted_iota(jnp.int32, (1, 128), 1)
    first_half = (lane % 32) < 16
    low_head = lane < HEAD_DIM

    def norm_rope(col, gi):
        x = p_ref[0, :, col:col + 128].astype(F32)
        ms = jnp.dot((x * x).astype(BF16), j_ref[...], preferred_element_type=F32)
        xn = x * lax.rsqrt(ms + EPS) * g_ref[gi:gi + 1, :]
        swapped = jnp.where(first_half, pltpu.roll(xn, 112, 1), pltpu.roll(xn, 16, 1))
        return xn * cos + swapped * sin

    def put_heads(ref, first, y):
        ref[0, first] = y[:, :HEAD_DIM].astype(ref.dtype)
        ref[0, first + 1] = y[:, HEAD_DIM:].astype(ref.dtype)

    for g in range(4):
        put_heads(qg_ref, 2 * g, norm_rope(128 * g, 0))
    put_heads(kg_ref, 0, norm_rope(512, 1))
    for g in range(4):
        put_heads(qd_ref, 2 * g, norm_rope(768 + 128 * g, 2))
    for g in range(4):
        put_heads(kd_ref, 2 * g, norm_rope(1280 + 128 * g, 3))
    av = p_ref[0, :, 640:768].astype(F32)
    ones_col = (lane == HEAD_DIM).astype(F32)
    vg_ref[0, 0] = (jnp.where(low_head, av, 0.0) + ones_col).astype(vg_ref.dtype)
    vg_ref[0, 1] = (jnp.where(low_head, pltpu.roll(av, 64, 1), 0.0) + ones_col).astype(vg_ref.dtype)
    tail = jnp.broadcast_to((lane == 0).astype(vd_ref.dtype), (av.shape[0], 128))
    for h in range(DIFF_HEADS):
        vd_ref[0, h, :, 0:128] = p_ref[0, :, 1792 + 128 * h:1792 + 128 * (h + 1)]
        vd_ref[0, h, :, 128:256] = tail


def _attn_prep(p_attn, cos, sin, gains, jmat):
    b, t, w = p_attn.shape
    tr = _pick(t, 544, 16)
    heads = lambda n, width: pl.BlockSpec((1, n, tr, width), lambda i, j: (i, 0, j, 0))
    shape = lambda n, width: jax.ShapeDtypeStruct((b, n, t, width), BF16)
    return pl.pallas_call(
        _prep_kernel,
        grid=(b, t // tr),
        in_specs=[pl.BlockSpec((1, tr, w), lambda i, j: (i, j, 0)),
                  pl.BlockSpec((tr, 128), lambda i, j: (j, 0)),
                  pl.BlockSpec((tr, 128), lambda i, j: (j, 0)),
                  pl.BlockSpec((4, 128), lambda i, j: (0, 0)),
                  pl.BlockSpec((128, 128), lambda i, j: (0, 0))],
        out_specs=[heads(GQA_HEADS, HEAD_DIM), heads(GQA_KV_HEADS, HEAD_DIM),
                   heads(GQA_KV_HEADS, 128),
                   heads(2 * DIFF_HEADS, HEAD_DIM), heads(2 * DIFF_HEADS, HEAD_DIM),
                   heads(DIFF_HEADS, 256)],
        out_shape=[shape(GQA_HEADS, HEAD_DIM), shape(GQA_KV_HEADS, HEAD_DIM),
                   shape(GQA_KV_HEADS, 128),
                   shape(2 * DIFF_HEADS, HEAD_DIM), shape(2 * DIFF_HEADS, HEAD_DIM),
                   shape(DIFF_HEADS, 256)],
        compiler_params=_params(("arbitrary", "arbitrary")),
        name="attn_prep",
    )(p_attn, cos, sin, gains, jmat)


ATT_TQ = 256
ATT_TK = 256


def _softmax_pv(q, k_ref, k_idx, v_ref, v_idx, s_scr, slot, nk, l_col):
    tq, tk = ATT_TQ, ATT_TK
    nchunk = nk // tk
    m_part = jnp.full((tq, 128), -jnp.inf, F32)
    for c in range(nchunk):
        kc = k_ref[0, k_idx, c * tk:(c + 1) * tk, :]
        s = lax.dot_general(q, kc, (((1,), (1,)), ((), ())), preferred_element_type=F32)
        s_scr[slot, :, c * tk:(c + 1) * tk] = s
        for u in range(tk // 128):
            m_part = jnp.maximum(m_part, s[:, u * 128:(u + 1) * 128])
    m = jnp.max(m_part, axis=-1, keepdims=True)
    acc = jnp.zeros((tq, v_ref.shape[-1]), F32)
    for c in range(nchunk):
        p = jnp.exp2(s_scr[slot, :, c * tk:(c + 1) * tk] - m)
        acc = acc + jnp.dot(p.astype(BF16), v_ref[0, v_idx, c * tk:(c + 1) * tk, :],
                            preferred_element_type=F32)
    return acc / acc[:, l_col:l_col + 1]


def _ctx_or_full(n_ctx, t, fn):
    is_ctx_tile = pl.program_id(2) * ATT_TQ < n_ctx

    @pl.when(is_ctx_tile)
    def _():
        fn(n_ctx)

    @pl.when(jnp.logical_not(is_ctx_tile))
    def _():
        fn(t)


ATT_MAPS = 4


def _gqa_kernel(q_ref, k_ref, v_ref, o_ref, s_scr, *, n_ctx, t):
    def attend(nk):
        for u in range(ATT_MAPS):
            o = _softmax_pv(q_ref[0, u], k_ref, 0, v_ref, 0, s_scr, u, nk, HEAD_DIM)
            o_ref[0, u] = o.astype(o_ref.dtype)
    _ctx_or_full(n_ctx, t, attend)


def _diff_kernel(lam_ref, q_ref, k_ref, v_ref, o_ref, s_scr, *, n_ctx, t):
    def attend(nk):
        for h in range(ATT_MAPS // 2):
            o1 = _softmax_pv(q_ref[0, 2 * h], k_ref, 2 * h, v_ref, h, s_scr, 2 * h, nk, 128)
            o2 = _softmax_pv(q_ref[0, 2 * h + 1], k_ref, 2 * h + 1, v_ref, h, s_scr, 2 * h + 1,
                             nk, 128)
            o_ref[0, h] = (o1[:, :128] - lam_ref[0] * o2[:, :128]).astype(o_ref.dtype)
    _ctx_or_full(n_ctx, t, attend)


def _attention_call(body, name, q, k, v, n_out, k_block, v_block, extra_in, extra_args, n_ctx):
    b, n_q, t, _ = q.shape
    assert n_ctx % ATT_TQ == 0 and t % ATT_TQ == 0 and n_ctx % ATT_TK == 0 and t % ATT_TK == 0
    vw = v.shape[-1]
    return pl.pallas_call(
        functools.partial(body, n_ctx=n_ctx, t=t),
        grid=(b, n_q // ATT_MAPS, t // ATT_TQ),
        in_specs=extra_in + [
            pl.BlockSpec((1, ATT_MAPS, ATT_TQ, HEAD_DIM), lambda i, u, j: (i, u, j, 0)),
            pl.BlockSpec((1, k_block, t, HEAD_DIM), lambda i, u, j: (i, u, 0, 0)),
            pl.BlockSpec((1, v_block, t, vw), lambda i, u, j: (i, u, 0, 0))],
        out_specs=pl.BlockSpec((1, n_out, ATT_TQ, 128), lambda i, u, j: (i, u, j, 0)),
        out_shape=jax.ShapeDtypeStruct((b, n_q // ATT_MAPS * n_out, t, 128), BF16),
        scratch_shapes=[pltpu.VMEM((ATT_MAPS, ATT_TQ, t), F32)],
        compiler_params=_params(("arbitrary", "arbitrary", "arbitrary")),
        name=name,
    )(*extra_args, q, k, v)


def _gqa_attention(q, k, v, n_ctx):
    assert GQA_HEADS // GQA_KV_HEADS == ATT_MAPS
    return _attention_call(_gqa_kernel, "attn_gqa", q, k, v, ATT_MAPS, 1, 1, [], [], n_ctx)


def _diff_attention(lam, q, k, v, n_ctx):
    return _attention_call(_diff_kernel, "attn_diff", q, k, v, ATT_MAPS // 2, ATT_MAPS,
                           ATT_MAPS // 2, [pl.BlockSpec(memory_space=pltpu.SMEM)], [lam], n_ctx)


CONV_TILE = 256
CONV_HALO = 16
CONV_ROWS = 32


def _conv_kernel(main_ref, left_ref, right_ref, w_ref, b_ref, lng_ref, lnb_ref, o_ref,
                 u_scr, *, n_ctx, t):
    j = pl.program_id(1)
    row0 = j * CONV_TILE
    has_left = jnp.logical_and(row0 != 0, row0 != n_ctx)
    has_right = jnp.logical_and(row0 + CONV_TILE != n_ctx, row0 + CONV_TILE != t)

    def glu(ref):
        x = ref[0].astype(F32)
        return x[:, :CONV_CH] * jax.nn.sigmoid(x[:, CONV_CH:])

    left = jnp.where(has_left, glu(left_ref), 0.0)
    right = jnp.where(has_right, glu(right_ref), 0.0)
    upad = jnp.concatenate([left, glu(main_ref), right], axis=0)
    n_pad = CONV_TILE + 2 * CONV_HALO
    for r in range(8):
        u_scr[r] = upad if r == 0 else pltpu.roll(upad, n_pad - r, 0)

    pad = CONV_WIDTH // 2

    def rows_body(i, carry):
        base = pl.multiple_of(i * CONV_ROWS, CONV_ROWS)
        acc = jnp.zeros((CONV_ROWS, CONV_CH), F32)
        for tap in range(CONV_WIDTH):
            off = tap + CONV_HALO - pad
            a, r = off // 8, off % 8
            acc = acc + w_ref[tap:tap + 1, :] * u_scr[r, pl.ds(base + 8 * a, CONV_ROWS), :]
        y = acc + b_ref[...]
        mu = jnp.mean(y, axis=-1, keepdims=True)
        yc = y - mu
        var = jnp.mean(yc * yc, axis=-1, keepdims=True)
        z = yc * lax.rsqrt(var + EPS) * lng_ref[...] + lnb_ref[...]
        o_ref[0, pl.ds(base, CONV_ROWS), :] = _silu(z).astype(o_ref.dtype)
        return carry

    lax.fori_loop(0, CONV_TILE // CONV_ROWS, rows_body, 0)


def _conv_branch(p_conv, w, bias, ln_g, ln_b, n_ctx):
    b, t, wd = p_conv.shape
    assert n_ctx % CONV_TILE == 0 and t % CONV_TILE == 0
    hb = CONV_TILE // CONV_HALO
    n_halo_blocks = t // CONV_HALO
    vec = pl.BlockSpec((1, CONV_CH), lambda i, j: (0, 0))
    return pl.pallas_call(
        functools.partial(_conv_kernel, n_ctx=n_ctx, t=t),
        grid=(b, t // CONV_TILE),
        in_specs=[pl.BlockSpec((1, CONV_TILE, wd), lambda i, j: (i, j, 0)),
                  pl.BlockSpec((1, CONV_HALO, wd),
                               lambda i, j: (i, jnp.maximum(j * hb - 1, 0), 0)),
                  pl.BlockSpec((1, CONV_HALO, wd),
                               lambda i, j: (i, jnp.minimum((j + 1) * hb, n_halo_blocks - 1), 0)),
                  pl.BlockSpec((CONV_WIDTH, CONV_CH), lambda i, j: (0, 0)),
                  vec, vec, vec],
        out_specs=pl.BlockSpec((1, CONV_TILE, CONV_CH), lambda i, j: (i, j, 0)),
        out_shape=jax.ShapeDtypeStruct((b, t, CONV_CH), BF16),
        scratch_shapes=[pltpu.VMEM((8, CONV_TILE + 2 * CONV_HALO, CONV_CH), F32)],
        compiler_params=_params(("arbitrary", "arbitrary")),
        name="conv",
    )(p_conv, p_conv, p_conv, w, bias.reshape(1, -1), ln_g.reshape(1, -1), ln_b.reshape(1, -1))


HG_TILE = 256
HG_SUB = 16
HG_ROWS = 64
HG_PAIRS = HGRN_HEADS // 2
HG_PAD = 8
HG_W = HGRN_HEADS * HGRN_DK


def _dot3(a_bf16, x):
    x1 = x.astype(BF16)
    r1 = x - x1.astype(F32)
    x2 = r1.astype(BF16)
    x3 = (r1 - x2.astype(F32)).astype(BF16)
    out = jnp.dot(a_bf16, x1, preferred_element_type=F32)
    out = out + jnp.dot(a_bf16, x2, preferred_element_type=F32)
    return out + jnp.dot(a_bf16, x3, preferred_element_type=F32)


def _hgrn_kernel(q_ref, z_ref, v_ref, la_ref, lb0_ref, tri_ref, j_ref, hm_ref, o_ref,
                 st_scr, q_scr, k_scr, c_scr, v_scr, kp_scr, cp_scr, vp_scr,
                 ks_scr, cs_scr, vs_scr, acc_scr, *, rev):
    c_sz = HG_SUB
    n_chunk = HG_TILE // c_sz

    @pl.when(pl.program_id(1) == 0)
    def _():
        st_scr[...] = jnp.zeros_like(st_scr)

    q = _silu(q_ref[0].astype(F32)) * (HGRN_DK ** -0.5)
    z = z_ref[0]
    v = v_ref[0].astype(F32)
    log_sig = jnp.minimum(z, 0.0) - jnp.log(1.0 + jnp.exp(-jnp.abs(z)))
    t1 = la_ref[...]
    t2 = lb0_ref[...] + log_sig
    logf = jnp.maximum(t1, t2) + jnp.log(1.0 + jnp.exp(-jnp.abs(t1 - t2)))
    k = 1.0 - jnp.exp(logf)
    c2 = _dot3(tri_ref[...], logf * LOG2E)
    q_scr[...] = q
    k_scr[...] = k
    c_scr[...] = c2
    v_scr[...] = v
    for ref, val in ((kp_scr, k), (cp_scr, c2), (vp_scr, v)):
        ref[0:HG_PAD, :] = jnp.zeros((HG_PAD, HG_W), F32)
        ref[HG_PAD:HG_PAD + HG_TILE, :] = val
        ref[HG_PAD + HG_TILE:HG_TILE + 2 * HG_PAD, :] = jnp.zeros((HG_PAD, HG_W), F32)

    order = range(n_chunk - 1, -1, -1) if rev else range(n_chunk)
    for ci in order:
        lo = ci * c_sz
        cj = c_scr[lo:lo + c_sz, :]
        c_end = cj[0:1] if rev else cj[c_sz - 1:c_sz]
        qd = (q_scr[lo:lo + c_sz, :] * jnp.exp2(cj)).astype(BF16)
        kd = (k_scr[lo:lo + c_sz, :] * jnp.exp2(c_end - cj)).astype(BF16)
        vb = v_scr[lo:lo + c_sz, :].astype(BF16)
        dec = jnp.exp2(c_end)
        outs = []
        for p in range(HG_PAIRS):
            sl = slice(128 * p, 128 * (p + 1))
            st = st_scr[p]
            outs.append(lax.dot_general(qd[:, sl], st.astype(BF16), (((1,), (1,)), ((), ())),
                                        preferred_element_type=F32))
            upd = lax.dot_general(vb[:, sl], kd[:, sl], (((0,), (0,)), ((), ())),
                                  preferred_element_type=F32)
            st_scr[p] = st * dec[:, sl] + upd * hm_ref[...]
        acc_scr[lo:lo + c_sz, :] = jnp.concatenate(outs, axis=1)

    row_in_chunk = lax.broadcasted_iota(jnp.int32, (HG_TILE, 1), 0) % c_sz

    def shifted(ref, r, mask):
        start = HG_PAD + r if rev else HG_PAD - r
        y = ref[start:start + HG_TILE, :]
        if not mask or r == 0:
            return y
        keep = (row_in_chunk < c_sz - r) if rev else (row_in_chunk >= r)
        return jnp.where(keep, y, 0.0)

    for r in range(8):
        ks_scr[...] = shifted(kp_scr, r, True)
        cs_scr[...] = shifted(cp_scr, r, True)
        vs_scr[...] = shifted(vp_scr, r, False)
        half = HG_W // 2
        for blk in range(HG_TILE // HG_ROWS):
            base = blk * HG_ROWS
            for a in range(c_sz // 8):
                n = c_sz - 8 * a
                d_lo = 0 if rev else 8 * a
                s_lo = 8 * a if rev else 0
                starts = [base + ci * c_sz for ci in range(HG_ROWS // c_sz)]

                def rows(ref, lo_in_chunk):
                    if n == c_sz:
                        return ref[base:base + HG_ROWS, :]
                    return jnp.concatenate(
                        [ref[s + lo_in_chunk:s + lo_in_chunk + n, :] for s in starts], axis=0)

                qa = rows(q_scr, d_lo)
                ca = rows(c_scr, d_lo)
                kk = rows(ks_scr, s_lo)
                cc = rows(cs_scr, s_lo)
                vv = rows(vs_scr, s_lo)
                pw = (qa * kk * jnp.exp2(ca - cc)).astype(BF16)
                m = pw.shape[0]
                att2 = jnp.dot(jnp.concatenate([pw[:, :half], pw[:, half:]], axis=0), j_ref[...],
                               preferred_element_type=F32)
                contrib = jnp.concatenate([att2[:m], att2[m:]], axis=1) * vv
                if n == c_sz:
                    acc_scr[base:base + HG_ROWS, :] = acc_scr[base:base + HG_ROWS, :] + contrib
                else:
                    for i, s in enumerate(starts):
                        dst = s + d_lo
                        acc_scr[dst:dst + n, :] = acc_scr[dst:dst + n, :] + contrib[i * n:(i + 1) * n]

    o_ref[0] = acc_scr[...]


def _hgrn_dir(p_hq, p_hf, p_hig, la, lb0, tri, jmat, head_mask, n_ctx, rev):
    b, t, _ = p_hq.shape
    assert n_ctx % HG_TILE == 0 and t % HG_TILE == 0
    n_tiles = t // HG_TILE
    n_ctx_tiles = n_ctx // HG_TILE
    col = 1 if rev else 0

    def tile(j):
        if not rev:
            return j
        return jnp.where(j < n_ctx_tiles, n_ctx_tiles - 1 - j, n_tiles - 1 - (j - n_ctx_tiles))

    const = lambda shape: pl.BlockSpec(shape, lambda i, j: (0,) * len(shape))
    return pl.pallas_call(
        functools.partial(_hgrn_kernel, rev=rev),
        grid=(b, n_tiles),
        in_specs=[pl.BlockSpec((1, HG_TILE, HG_W), lambda i, j: (i, tile(j), 0)),
                  pl.BlockSpec((1, HG_TILE, HG_W), lambda i, j: (i, tile(j), col)),
                  pl.BlockSpec((1, HG_TILE, HG_W), lambda i, j: (i, tile(j), 0)),
                  const((1, HG_W)),
                  const((1, HG_W)),
                  const((HG_TILE, HG_TILE)), const((HG_W // 2, HG_W // 2)), const((128, 128))],
        out_specs=pl.BlockSpec((1, HG_TILE, HG_W), lambda i, j: (i, tile(j), 0)),
        out_shape=jax.ShapeDtypeStruct((b, t, HG_W), F32),
        scratch_shapes=[pltpu.VMEM((HG_PAIRS, 128, 128), F32)]
                       + [pltpu.VMEM((HG_TILE, HG_W), F32) for _ in range(4)]
                       + [pltpu.VMEM((HG_TILE + 2 * HG_PAD, HG_W), F32) for _ in range(3)]
                       + [pltpu.VMEM((HG_TILE, HG_W), F32) for _ in range(4)],
        compiler_params=_params(("arbitrary", "arbitrary")),
        name="hgrn_bwd" if rev else "hgrn_fwd",
    )(p_hq, p_hf, p_hig, la[col:col + 1], lb0[col:col + 1], tri, jmat, head_mask)


def _merge_kernel(og_ref, od_ref, cv_ref, hf_ref, hb_ref, hg_ref, gt_ref, x_ref, wb_ref, wo_ref,
                  sub_ref, hng_ref, j_ref, gc_ref, gx_ref, o_ref, *, tr, n_ctx, lam_init):
    gates = gt_ref[0]

    def gate(i):
        return jax.nn.sigmoid(gates[:, i * 1024:(i + 1) * 1024].astype(F32))

    pa = jnp.zeros((tr, 1024), F32)
    for h in range(GQA_HEADS):
        pa = pa + jnp.dot(og_ref[0, h][:, :HEAD_DIM],
                          wb_ref[0, h * HEAD_DIM:(h + 1) * HEAD_DIM, :],
                          preferred_element_type=F32)
    merged = gate(0) * pa
    pb = jnp.zeros((tr, 1024), F32)
    for h in range(DIFF_HEADS):
        od = od_ref[0, h].astype(F32)
        ms = jnp.mean(od * od, axis=-1, keepdims=True)
        on = od * lax.rsqrt(ms + EPS) * sub_ref[...] * (1.0 - lam_init)
        pb = pb + jnp.dot(on.astype(BF16), wb_ref[1, h * 128:(h + 1) * 128, :],
                          preferred_element_type=F32)
    merged = merged + gate(1) * pb
    merged = merged + gate(2) * jnp.dot(cv_ref[0], wb_ref[2], preferred_element_type=F32)
    od = hf_ref[0] + hb_ref[0]
    ms = jnp.dot((od * od).astype(BF16), j_ref[...], preferred_element_type=F32)
    on = od * lax.rsqrt(ms + EPS) * hng_ref[...] * _silu(hg_ref[0].astype(F32))
    merged = merged + gate(3) * jnp.dot(on.astype(BF16), wb_ref[3], preferred_element_type=F32)

    mix = jnp.dot(merged.astype(BF16), wo_ref[...], preferred_element_type=F32)
    g1 = jnp.where(_row_is_ctx(pl.program_id(1), tr, n_ctx), gc_ref[...], gx_ref[0])
    o_ref[0] = x_ref[0] + g1 * mix


def _merge(o_gqa, o_diff, conv_o, hg_f, hg_b, p_hig, p_gate, x, wb, wo, subln_g, hgrn_g, jmat64,
           g_c, g_x, n_ctx, lam_init):
    b, t, d = x.shape
    tr = _pick(t, 272, 16)
    const = lambda shape: pl.BlockSpec(shape, lambda i, j: (0,) * len(shape))
    return pl.pallas_call(
        functools.partial(_merge_kernel, tr=tr, n_ctx=n_ctx, lam_init=lam_init),
        grid=(b, t // tr),
        in_specs=[pl.BlockSpec((1, GQA_HEADS, tr, 128), lambda i, j: (i, 0, j, 0)),
                  pl.BlockSpec((1, DIFF_HEADS, tr, 128), lambda i, j: (i, 0, j, 0)),
                  pl.BlockSpec((1, tr, CONV_CH), lambda i, j: (i, j, 0)),
                  pl.BlockSpec((1, tr, HG_W), lambda i, j: (i, j, 0)),
                  pl.BlockSpec((1, tr, HG_W), lambda i, j: (i, j, 0)),
                  pl.BlockSpec((1, tr, HG_W), lambda i, j: (i, j, 1)),
                  pl.BlockSpec((1, tr, 4 * d), lambda i, j: (i, j, 0)),
                  pl.BlockSpec((1, tr, d), lambda i, j: (i, j, 0)),
                  const((4, BRANCH_W, d)), const((d, d)),
                  const((1, 128)), const((1, HG_W)), const((HG_W, HG_W)),
                  const((1, d)),
                  pl.BlockSpec((1, 1, d), lambda i, j: (i, 0, 0))],
        out_specs=pl.BlockSpec((1, tr, d), lambda i, j: (i, j, 0)),
        out_shape=jax.ShapeDtypeStruct((b, t, d), F32),
        compiler_params=_params(("arbitrary", "arbitrary")),
        name="merge",
    )(o_gqa, o_diff, conv_o, hg_f, hg_b, p_hig, p_gate, x, wb, wo, subln_g, hgrn_g, jmat64,
      g_c, g_x)


def _swiglu_step(h, w1, w3, w2):
    a = jnp.dot(h, w1, preferred_element_type=F32)
    g = jnp.dot(h, w3, preferred_element_type=F32)
    return jnp.dot((_silu(a) * g).astype(BF16), w2, preferred_element_type=F32)


def _ffn_kernel(h_ref, w1_ref, w3_ref, w2_ref, x_ref, gc_ref, gx_ref, o_ref, acc_scr, *, tr, n_ctx):
    kf = pl.program_id(2)

    @pl.when(kf == 0)
    def _():
        acc_scr[...] = jnp.zeros_like(acc_scr)

    acc_scr[...] += _swiglu_step(h_ref[0], w1_ref[...], w3_ref[...], w2_ref[...])

    @pl.when(kf == pl.num_programs(2) - 1)
    def _():
        g2 = jnp.where(_row_is_ctx(pl.program_id(1), tr, n_ctx), gc_ref[...], gx_ref[0])
        o_ref[0] = x_ref[0] + g2 * acc_scr[...]


def _ffn(h, w1, w3, w2, x, g_c, g_x, n_ctx):
    b, t, d = x.shape
    dff = w1.shape[1]
    tr = _pick(t, 1088, 16)
    tf = _pick(dff, 512, 128)
    return pl.pallas_call(
        functools.partial(_ffn_kernel, tr=tr, n_ctx=n_ctx),
        grid=(b, t // tr, dff // tf),
        in_specs=[pl.BlockSpec((1, tr, d), lambda i, j, f: (i, j, 0)),
                  pl.BlockSpec((d, tf), lambda i, j, f: (0, f)),
                  pl.BlockSpec((d, tf), lambda i, j, f: (0, f)),
                  pl.BlockSpec((tf, d), lambda i, j, f: (f, 0)),
                  pl.BlockSpec((1, tr, d), lambda i, j, f: (i, j, 0)),
                  pl.BlockSpec((1, d), lambda i, j, f: (0, 0)),
                  pl.BlockSpec((1, 1, d), lambda i, j, f: (i, 0, 0))],
        out_specs=pl.BlockSpec((1, tr, d), lambda i, j, f: (i, j, 0)),
        out_shape=jax.ShapeDtypeStruct((b, t, d), F32),
        scratch_shapes=[pltpu.VMEM((tr, d), F32)],
        input_output_aliases={4: 0},
        compiler_params=_params(("arbitrary", "arbitrary", "arbitrary")),
        name="ffn",
    )(h, w1, w3, w2, x, g_c, g_x)


MOE_TILE = 512
GATHER_ROWS = 512
COMBINE_ROWS = 544


def _gather_kernel(idx_ref, src_ref, o_ref, sem, *, rows):
    def copy(r):
        return pltpu.make_async_copy(src_ref.at[idx_ref[0, 0, r]], o_ref.at[r], sem)

    def start(r2, carry):
        copy(2 * r2).start(priority=0)
        copy(2 * r2 + 1).start(priority=1)
        return carry

    def wait(r, carry):
        copy(r).wait()
        return carry

    lax.fori_loop(0, rows // 2, start, 0)
    lax.fori_loop(0, rows, wait, 0)


def _gather_rows(src_rows, idx):
    n_out = idx.shape[0]
    n_src, d = src_rows.shape
    assert d % 128 == 0
    rows = _pick(n_out, GATHER_ROWS, 16)
    steps = n_out // rows
    tiles = pl.pallas_call(
        functools.partial(_gather_kernel, rows=rows),
        grid=(steps,),
        in_specs=[pl.BlockSpec((1, 1, rows), lambda i: (i, 0, 0), memory_space=pltpu.SMEM),
                  pl.BlockSpec(memory_space=pl.ANY)],
        out_specs=pl.BlockSpec((rows, d // 128, 128), lambda i: (i, 0, 0)),
        out_shape=jax.ShapeDtypeStruct((n_out, d // 128, 128), src_rows.dtype),
        scratch_shapes=[pltpu.SemaphoreType.DMA(())],
        compiler_params=_params(("arbitrary",)),
        name="moe_gather",
    )(idx.reshape(steps, 1, rows), src_rows.reshape(n_src, d // 128, 128))
    return tiles.reshape(n_out, d)


def _expert_kernel(te_ref, nu_ref, x_ref, w1_ref, w3_ref, w2_ref, o_ref, acc_scr):
    i = pl.program_id(0)
    kf = pl.program_id(1)
    last = kf == pl.num_programs(1) - 1
    used = i < nu_ref[0]

    @pl.when(jnp.logical_and(used, kf == 0))
    def _():
        acc_scr[...] = jnp.zeros_like(acc_scr)

    @pl.when(used)
    def _():
        acc_scr[...] += _swiglu_step(x_ref[...].astype(BF16), w1_ref[0], w3_ref[0], w2_ref[0])

    @pl.when(jnp.logical_and(used, last))
    def _():
        o_ref[...] = acc_scr[...]

    @pl.when(jnp.logical_and(jnp.logical_not(used), last))
    def _():
        o_ref[...] = jnp.zeros_like(o_ref)


def _expert_ffn(xs, w1, w3, w2, tile_expert, n_used):
    n_sorted, d = xs.shape
    dff = w1.shape[2]
    tf = _pick(dff, 896, 128)
    grid_spec = pltpu.PrefetchScalarGridSpec(
        num_scalar_prefetch=2,
        grid=(n_sorted // MOE_TILE, dff // tf),
        in_specs=[pl.BlockSpec((MOE_TILE, d), lambda i, f, te, nu: (i, 0)),
                  pl.BlockSpec((1, d, tf), lambda i, f, te, nu: (te[i], 0, f)),
                  pl.BlockSpec((1, d, tf), lambda i, f, te, nu: (te[i], 0, f)),
                  pl.BlockSpec((1, tf, d), lambda i, f, te, nu: (te[i], f, 0))],
        out_specs=pl.BlockSpec((MOE_TILE, d), lambda i, f, te, nu: (i, 0)),
        scratch_shapes=[pltpu.VMEM((MOE_TILE, d), F32)])
    return pl.pallas_call(
        _expert_kernel,
        grid_spec=grid_spec,
        out_shape=jax.ShapeDtypeStruct((n_sorted, d), F32),
        compiler_params=_params(("arbitrary", "arbitrary")),
        name="moe_experts",
    )(tile_expert, n_used, xs, w1, w3, w2)


def _combine_kernel(rw_ref, y1_ref, y2_ref, x_ref, gc_ref, gx_ref, o_ref, *, tr, n_ctx):
    rw = rw_ref[0]
    f = rw[:, 2:3] * y1_ref[0, 0] + rw[:, 3:4] * y2_ref[0, 0]
    g2 = jnp.where(_row_is_ctx(pl.program_id(1), tr, n_ctx), gc_ref[...], gx_ref[0])
    o_ref[0] = x_ref[0] + g2 * f


def _combine(route, y_tok, x, g_c, g_x, n_ctx):
    b, t, d = x.shape
    tr = _pick(t, COMBINE_ROWS, 16)
    return pl.pallas_call(
        functools.partial(_combine_kernel, tr=tr, n_ctx=n_ctx),
        grid=(b, t // tr),
        in_specs=[pl.BlockSpec((1, tr, 128), lambda i, j: (i, j, 0)),
                  pl.BlockSpec((1, 1, tr, d), lambda i, j: (0, i, j, 0)),
                  pl.BlockSpec((1, 1, tr, d), lambda i, j: (1, i, j, 0)),
                  pl.BlockSpec((1, tr, d), lambda i, j: (i, j, 0)),
                  pl.BlockSpec((1, d), lambda i, j: (0, 0)),
                  pl.BlockSpec((1, 1, d), lambda i, j: (i, 0, 0))],
        out_specs=pl.BlockSpec((1, tr, d), lambda i, j: (i, j, 0)),
        out_shape=jax.ShapeDtypeStruct((b, t, d), F32),
        input_output_aliases={3: 0},
        compiler_params=_params(("arbitrary", "arbitrary")),
        name="moe_combine",
    )(route, y_tok, y_tok, x, g_c, g_x)


def _moe_plan(e1, e2):
    n = e1.shape[0]
    e = jnp.concatenate([e1, e2])
    onehot = (e[:, None] == jnp.arange(N_EXPERTS, dtype=jnp.int32)[None, :]).astype(jnp.int32)
    counts = jnp.sum(onehot, axis=0)
    rank = jnp.sum((jnp.cumsum(onehot, axis=0) - onehot) * onehot, axis=1)
    padded = ((counts + MOE_TILE - 1) // MOE_TILE) * MOE_TILE
    ends = jnp.cumsum(padded)
    slot = jnp.sum((ends - padded)[None, :] * onehot, axis=1) + rank
    n_tiles = (2 * n + N_EXPERTS * (MOE_TILE - 1)) // MOE_TILE
    token = jnp.tile(jnp.arange(n, dtype=jnp.int32), 2)
    tokens_sorted = jnp.zeros((n_tiles * MOE_TILE,), jnp.int32).at[slot].set(token)
    tile_start = jnp.arange(n_tiles, dtype=jnp.int32) * MOE_TILE
    tile_expert = jnp.minimum(jnp.sum((ends[None, :] <= tile_start[:, None]).astype(jnp.int32), axis=1),
                              N_EXPERTS - 1)
    n_used = (ends[-1] // MOE_TILE).reshape(1)
    return slot[:n], slot[n:], tokens_sorted, tile_expert, n_used


def _moe(h_f32, route, w1, w3, w2, x, g_c, g_x, n_ctx):
    b, t, d = x.shape
    e1 = route[:, :, 0].reshape(-1).astype(jnp.int32)
    e2 = route[:, :, 1].reshape(-1).astype(jnp.int32)
    pos1, pos2, tokens_sorted, tile_expert, n_used = _moe_plan(e1, e2)
    xs = _gather_rows(h_f32.reshape(b * t, d), tokens_sorted)
    ys = _expert_ffn(xs, w1, w3, w2, tile_expert, n_used)
    y_tok = _gather_rows(ys, jnp.concatenate([pos1, pos2])).reshape(2, b, t, d)
    return _combine(route, y_tok, x, g_c, g_x, n_ctx)


def _router_kernel(h_ref, r_ref, o_ref):
    logits = jnp.dot(h_ref[0], r_ref[...], preferred_element_type=F32,
                     precision=lax.Precision.HIGHEST)
    lane = lax.broadcasted_iota(jnp.int32, logits.shape, 1).astype(F32)
    neg = -jnp.inf
    logits = jnp.where(lane < N_EXPERTS, logits, neg)
    m1 = jnp.max(logits, axis=-1, keepdims=True)
    i1 = jnp.min(jnp.where(logits == m1, lane, 128.0), axis=-1, keepdims=True)
    rest = jnp.where(lane == i1, neg, logits)
    m2 = jnp.max(rest, axis=-1, keepdims=True)
    i2 = jnp.min(jnp.where(rest == m2, lane, 128.0), axis=-1, keepdims=True)
    e2 = jnp.exp(m2 - m1)
    w1 = 1.0 / (1.0 + e2)
    w2 = e2 / (1.0 + e2)
    o_ref[0] = (jnp.where(lane == 0.0, i1, 0.0) + jnp.where(lane == 1.0, i2, 0.0)
                + jnp.where(lane == 2.0, w1, 0.0) + jnp.where(lane == 3.0, w2, 0.0))


def _router(h_f32, router_pad):
    b, t, d = h_f32.shape
    tr = _pick(t, 1088, 16)
    return pl.pallas_call(
        _router_kernel,
        grid=(b, t // tr),
        in_specs=[pl.BlockSpec((1, tr, d), lambda i, j: (i, j, 0)),
                  pl.BlockSpec((d, 128), lambda i, j: (0, 0))],
        out_specs=pl.BlockSpec((1, tr, 128), lambda i, j: (i, j, 0)),
        out_shape=jax.ShapeDtypeStruct((b, t, 128), F32),
        compiler_params=_params(("arbitrary", "arbitrary")),
        name="router",
    )(h_f32, router_pad)


def _rope_tables(seq, n_ctx):
    rows = seq // GRID_W
    row = jnp.repeat(jnp.arange(rows, dtype=F32), GRID_W)
    col = (jnp.arange(seq) % GRID_W).astype(F32)
    axis_dim = HEAD_DIM // 2
    inv_freq = ROPE_THETA ** (-jnp.arange(0, axis_dim, 2, dtype=F32) / axis_dim)
    ang_r = row[:, None] * inv_freq
    ang_c = col[:, None] * inv_freq
    cos = jnp.concatenate([jnp.cos(ang_r), jnp.cos(ang_r), jnp.cos(ang_c), jnp.cos(ang_c)], axis=-1)
    sin = jnp.concatenate([-jnp.sin(ang_r), jnp.sin(ang_r), -jnp.sin(ang_c), jnp.sin(ang_c)], axis=-1)
    cos = jnp.concatenate([jnp.ones((n_ctx, HEAD_DIM), F32), cos], axis=0)
    sin = jnp.concatenate([jnp.zeros((n_ctx, HEAD_DIM), F32), sin], axis=0)
    return jnp.tile(cos, (1, 2)), jnp.tile(sin, (1, 2))


def _block_ones(n, blk, value=1.0, dtype=BF16):
    i = np.arange(n) // blk
    return jnp.asarray((i[:, None] == i[None, :]).astype(np.float32) * value, dtype)


def _chunk_tri(n, chunk, rev):
    i = np.arange(n)
    same = (i[:, None] // chunk) == (i[None, :] // chunk)
    tri = (i[None, :] >= i[:, None]) if rev else (i[None, :] <= i[:, None])
    return jnp.asarray((same & tri).astype(np.float32), BF16)


def kernel(x, c, ctx, c_ctx, ada_w, ada_b, norm1_g, norm2_g, w_in, qk_norm_g, diff_lambda,
           diff_subln_g, conv_w, conv_b, conv_ln_g, conv_ln_b, hgrn_lb_logits, hgrn_norm_g,
           w_branch, w_out, ffn_w1, ffn_w3, ffn_w2, moe_router, moe_w1, moe_w3, moe_w2):
    b, seq, d = x.shape
    n_ctx = ctx.shape[1]
    t = n_ctx + seq
    depth = ada_w.shape[0]

    cos, sin = _rope_tables(seq, n_ctx)
    j_head_mean = _block_ones(128, HEAD_DIM, 1.0 / HEAD_DIM)
    j_hg_sum = _block_ones(HG_W // 2, HGRN_DK)
    j_hg_mean = _block_ones(HG_W, HGRN_DK, 1.0 / HGRN_DK)
    head_mask = _block_ones(128, HGRN_DK, 1.0, F32)
    tri_f = _chunk_tri(HG_TILE, HG_SUB, False)
    tri_b = _chunk_tri(HG_TILE, HG_SUB, True)

    lb = jnp.cumsum(jax.nn.softmax(hgrn_lb_logits.astype(F32), axis=0), axis=0)
    lb = lb - lb[:1]
    log_lb = jnp.log(lb)
    log_1m_lb = jnp.log1p(-lb)

    rows = ((b + 1 + 7) // 8) * 8
    cvec = jnp.zeros((rows, d), F32).at[:b].set(c).at[b].set(c_ctx)
    mod = _ada(cvec, ada_w, ada_b).reshape(depth, rows, 6, d)

    xc = jnp.concatenate([ctx, x], axis=1)
    q_scale = LOG2E * HEAD_DIM ** -0.5

    for l in range(depth):
        mx = mod[l, :b]
        mc = mod[l, b]
        vx = lambda i: mx[:, i:i + 1, :]
        vc = lambda i: mc[i:i + 1, :]
        w_l = w_in[l].astype(BF16)

        h = _normmod(xc, norm1_g[l], vc(0), vc(1), vx(0), vx(1), n_ctx).reshape(b * t, d)
        proj = lambda cols, dt: _matmul(h, w_l[:, cols[0]:cols[1]], dt).reshape(b, t, -1)
        p_attn = proj(COL_ATTN, BF16)
        p_conv = proj(COL_CONV, BF16)
        p_hq = proj(COL_HQ, BF16)
        p_hf = proj(COL_HF, F32)
        p_hig = proj(COL_HIG, BF16)
        p_gate = proj(COL_GATE, BF16)

        g = qk_norm_g[l].astype(F32)
        gains = jnp.tile(jnp.stack([g[0] * q_scale, g[1], g[2] * q_scale, g[3]]), (1, 2))
        lam_init = 0.8 - 0.6 * math.exp(-0.3 * l)
        lp = diff_lambda[l].astype(F32)
        lam = (jnp.exp(jnp.sum(lp[0] * lp[1])) - jnp.exp(jnp.sum(lp[2] * lp[3]))
               + lam_init).reshape(1)
        qg, kg, vg, qd, kd, vd = _attn_prep(p_attn, cos, sin, gains, j_head_mean)
        o_gqa = _gqa_attention(qg, kg, vg, n_ctx)
        o_diff = _diff_attention(lam, qd, kd, vd, n_ctx)

        conv_o = _conv_branch(p_conv, conv_w[l], conv_b[l], conv_ln_g[l], conv_ln_b[l], n_ctx)

        hg_f = _hgrn_dir(p_hq, p_hf, p_hig, log_lb[l], log_1m_lb[l], tri_f, j_hg_sum, head_mask,
                         n_ctx, False)
        hg_b = _hgrn_dir(p_hq, p_hf, p_hig, log_lb[l], log_1m_lb[l], tri_b, j_hg_sum, head_mask,
                         n_ctx, True)

        xc = _merge(o_gqa, o_diff, conv_o, hg_f, hg_b, p_hig, p_gate, xc,
                    w_branch[l].astype(BF16), w_out[l].astype(BF16),
                    diff_subln_g[l].reshape(1, -1), jnp.tile(hgrn_norm_g[l], HGRN_HEADS).reshape(1, -1),
                    j_hg_mean, vc(2), vx(2), n_ctx, lam_init)

        if l % 2 == 0:
            h2 = _normmod(xc, norm2_g[l], vc(3), vc(4), vx(3), vx(4), n_ctx)
            i = l // 2
            xc = _ffn(h2, ffn_w1[i].astype(BF16), ffn_w3[i].astype(BF16), ffn_w2[i].astype(BF16),
                      xc, vc(5), vx(5), n_ctx)
        else:
            h2f = _normmod(xc, norm2_g[l], vc(3), vc(4), vx(3), vx(4), n_ctx, out_dtype=F32)
            i = l // 2
            router_pad = jnp.zeros((d, 128), F32).at[:, :N_EXPERTS].set(moe_router[i])
            route = _router(h2f, router_pad)
            xc = _moe(h2f, route, moe_w1[i].astype(BF16), moe_w3[i].astype(BF16),
                      moe_w2[i].astype(BF16), xc, vc(5), vx(5), n_ctx)
    return xc[:, n_ctx:]
```

```python
import functools
import math

import jax
import jax.numpy as jnp
import numpy as np
from jax import lax
from jax.experimental import pallas as pl
from jax.experimental.pallas import tpu as pltpu

F32 = jnp.float32
BF16 = jnp.bfloat16

GRID_W = 64
HEAD_DIM = 64
ROPE_THETA = 10000.0
EPS = 1e-6
GQA_HEADS = 8
GQA_KV_HEADS = 2
DIFF_HEADS = 4
CONV_CH = 512
CONV_WIDTH = 31
HGRN_HEADS = 8
HGRN_DK = 64
HGRN_CHUNK = 64
BRANCH_W = 512
N_EXPERTS = 8
LOG2E = 1.4426950408889634

COL_ATTN = (0, 2304)
COL_CONV = (2304, 3328)
COL_HQ = (3328, 3840)
COL_HF = (3840, 4864)
COL_HIG = (4864, 5888)
COL_GATE = (5888, 9984)

VMEM_LIMIT_BYTES = 50 * 1024 * 1024


def _params(sem):
    return pltpu.CompilerParams(dimension_semantics=sem, vmem_limit_bytes=VMEM_LIMIT_BYTES)


def _pick(n, target, mult):
    best = None
    for d in range(mult, min(n, target) + 1, mult):
        if n % d == 0:
            best = d
    assert best is not None, (n, target, mult)
    return best


def _silu(x):
    return x * jax.nn.sigmoid(x)


def _row_is_ctx(tile_idx, tr, n_ctx):
    row = tile_idx * tr + lax.broadcasted_iota(jnp.int32, (tr, 1), 0)
    return row < n_ctx


def _ada_kernel(c_ref, w_ref, b_ref, o_ref):
    s = _silu(c_ref[...])
    o_ref[0] = jnp.dot(s.astype(BF16), w_ref[0].astype(BF16),
                       preferred_element_type=F32) + b_ref[0]


def _ada(cvec, ada_w, ada_b):
    depth, d, n = ada_w.shape
    rows = cvec.shape[0]
    tn = _pick(n, 1536, 128)
    return pl.pallas_call(
        _ada_kernel,
        grid=(depth, n // tn),
        in_specs=[pl.BlockSpec((rows, d), lambda l, j: (0, 0)),
                  pl.BlockSpec((1, d, tn), lambda l, j: (l, 0, j)),
                  pl.BlockSpec((1, 1, tn), lambda l, j: (l, 0, j))],
        out_specs=pl.BlockSpec((1, rows, tn), lambda l, j: (l, 0, j)),
        out_shape=jax.ShapeDtypeStruct((depth, rows, n), F32),
        compiler_params=_params(("arbitrary", "arbitrary")),
        name="ada",
    )(cvec, ada_w, ada_b.reshape(depth, 1, n))


def _normmod_kernel(x_ref, g_ref, shc_ref, scc_ref, shx_ref, scx_ref, o_ref, *, tr, n_ctx):
    x = x_ref[0]
    ms = jnp.mean(x * x, axis=-1, keepdims=True)
    y = x * lax.rsqrt(ms + EPS) * g_ref[...]
    is_ctx = _row_is_ctx(pl.program_id(1), tr, n_ctx)
    sh = jnp.where(is_ctx, shc_ref[...], shx_ref[0])
    sc = jnp.where(is_ctx, scc_ref[...], scx_ref[0])
    o_ref[0] = (y * (1.0 + sc) + sh).astype(o_ref.dtype)


def _normmod(x, g, sh_c, sc_c, sh_x, sc_x, n_ctx, out_dtype=BF16):
    b, t, d = x.shape
    tr = _pick(t, 1088, 16)
    vec_c = pl.BlockSpec((1, d), lambda i, j: (0, 0))
    vec_x = pl.BlockSpec((1, 1, d), lambda i, j: (i, 0, 0))
    return pl.pallas_call(
        functools.partial(_normmod_kernel, tr=tr, n_ctx=n_ctx),
        grid=(b, t // tr),
        in_specs=[pl.BlockSpec((1, tr, d), lambda i, j: (i, j, 0)),
                  vec_c, vec_c, vec_c, vec_x, vec_x],
        out_specs=pl.BlockSpec((1, tr, d), lambda i, j: (i, j, 0)),
        out_shape=jax.ShapeDtypeStruct((b, t, d), out_dtype),
        compiler_params=_params(("arbitrary", "arbitrary")),
        name="normmod",
    )(x, g.reshape(1, d), sh_c, sc_c, sh_x, sc_x)


def _mm_kernel(a_ref, w_ref, o_ref):
    o_ref[...] = jnp.dot(a_ref[...], w_ref[...],
                         preferred_element_type=F32).astype(o_ref.dtype)


def _matmul(a, w, out_dtype):
    m, k = a.shape
    n = w.shape[1]
    tm = _pick(m, 2048, 256)
    tn = _pick(n, 1024, 128)
    return pl.pallas_call(
        _mm_kernel,
        grid=(m // tm, n // tn),
        in_specs=[pl.BlockSpec((tm, k), lambda i, j: (i, 0)),
                  pl.BlockSpec((k, tn), lambda i, j: (0, j))],
        out_specs=pl.BlockSpec((tm, tn), lambda i, j: (i, j)),
        out_shape=jax.ShapeDtypeStruct((m, n), out_dtype),
        compiler_params=_params(("arbitrary", "arbitrary")),
        name="proj",
    )(a, w)


def _prep_kernel(p_ref, cos_ref, sin_ref, g_ref, j_ref, qg_ref, kg_ref, vg_ref,
                 qd_ref, kd_ref, vd_ref):
    cos = cos_ref[...]
    sin = sin_ref[...]
    lane = lax.broadcasted_iota(jnp.int32, (1, 128), 1)
    first_half = (lane % 32) < 16
    low_head = lane < HEAD_DIM

    def norm_rope(col, gi):
        x = p_ref[0, :, col:col + 128].astype(F32)
        ms = jnp.dot((x * x).astype(BF16), j_ref[...], preferred_element_type=F32)
        xn = x * lax.rsqrt(ms + EPS) * g_ref[gi:gi + 1, :]
        swapped = jnp.where(first_half, pltpu.roll(xn, 112, 1), pltpu.roll(xn, 16, 1))
        return xn * cos + swapped * sin

    def put_heads(ref, first, y):
        ref[0, first] = y[:, :HEAD_DIM].astype(ref.dtype)
        ref[0, first + 1] = y[:, HEAD_DIM:].astype(ref.dtype)

    def put_heads_t(ref, first, y):
        yt = y.T
        ref[0, first] = yt[:HEAD_DIM, :].astype(ref.dtype)
        ref[0, first + 1] = yt[HEAD_DIM:, :].astype(ref.dtype)

    tr = cos.shape[0]
    for g in range(4):
        put_heads_t(qg_ref, 2 * g, norm_rope(128 * g, 0))
    put_heads(kg_ref, 0, norm_rope(512, 1))
    for g in range(4):
        put_heads_t(qd_ref, 2 * g, norm_rope(768 + 128 * g, 2))
    for g in range(4):
        put_heads(kd_ref, 2 * g, norm_rope(1280 + 128 * g, 3))
    avt = p_ref[0, :, 640:768].astype(F32).T
    def ones_first_row(n):
        row = lax.broadcasted_iota(jnp.int32, (n, tr), 0)
        return jnp.where(row == 0, 1.0, 0.0).astype(vg_ref.dtype)

    for h in range(GQA_KV_HEADS):
        vg_ref[0, h, 0:HEAD_DIM, :] = avt[h * HEAD_DIM:(h + 1) * HEAD_DIM, :].astype(vg_ref.dtype)
        vg_ref[0, h, HEAD_DIM:2 * HEAD_DIM, :] = ones_first_row(HEAD_DIM)
    for h in range(DIFF_HEADS):
        bvt = p_ref[0, :, 1792 + 128 * h:1792 + 128 * (h + 1)].astype(F32).T
        vd_ref[0, h, 0:128, :] = bvt.astype(vd_ref.dtype)
        vd_ref[0, h, 128:128 + ATT_VPAD, :] = ones_first_row(ATT_VPAD)


def _attn_prep(p_attn, cos, sin, gains, jmat):
    b, t, w = p_attn.shape
    tr = _pick(t, 256, 128)
    heads = lambda n, width: pl.BlockSpec((1, n, tr, width), lambda i, j: (i, 0, j, 0))
    shape = lambda n, width: jax.ShapeDtypeStruct((b, n, t, width), BF16)
    heads_t = lambda n, rows: pl.BlockSpec((1, n, rows, tr), lambda i, j: (i, 0, 0, j))
    shape_t = lambda n, rows: jax.ShapeDtypeStruct((b, n, rows, t), BF16)
    return pl.pallas_call(
        _prep_kernel,
        grid=(b, t // tr),
        in_specs=[pl.BlockSpec((1, tr, w), lambda i, j: (i, j, 0)),
                  pl.BlockSpec((tr, 128), lambda i, j: (j, 0)),
                  pl.BlockSpec((tr, 128), lambda i, j: (j, 0)),
                  pl.BlockSpec((4, 128), lambda i, j: (0, 0)),
                  pl.BlockSpec((128, 128), lambda i, j: (0, 0))],
        out_specs=[heads_t(GQA_HEADS, HEAD_DIM), heads(GQA_KV_HEADS, HEAD_DIM),
                   heads_t(GQA_KV_HEADS, 128),
                   heads_t(2 * DIFF_HEADS, HEAD_DIM), heads(2 * DIFF_HEADS, HEAD_DIM),
                   heads_t(DIFF_HEADS, 128 + ATT_VPAD)],
        out_shape=[shape_t(GQA_HEADS, HEAD_DIM), shape(GQA_KV_HEADS, HEAD_DIM),
                   shape_t(GQA_KV_HEADS, 128),
                   shape_t(2 * DIFF_HEADS, HEAD_DIM), shape(2 * DIFF_HEADS, HEAD_DIM),
                   shape_t(DIFF_HEADS, 128 + ATT_VPAD)],
        compiler_params=_params(("arbitrary", "arbitrary")),
        name="attn_prep",
    )(p_attn, cos, sin, gains, jmat)


ATT_TQ = 256
ATT_TK = 256
ATT_VPAD = 16


def _scores(q_ref, k_ref, k_idx, s_scr, m_scr, slot, nk):
    tq, tk = ATT_TQ, ATT_TK
    qt = q_ref[0, slot]
    m_part = jnp.full((8, tq), -jnp.inf, F32)
    for c in range(nk // tk):
        s = jnp.dot(k_ref[0, k_idx, c * tk:(c + 1) * tk, :], qt, preferred_element_type=F32)
        s_scr[slot, c * tk:(c + 1) * tk, :] = s
        for u in range(tk // 8):
            m_part = jnp.maximum(m_part, s[8 * u:8 * (u + 1), :])
    m_scr[slot] = jnp.broadcast_to(jnp.max(m_part, axis=0, keepdims=True), (8, tq))


def _softmax_pv(v_ref, v_idx, s_scr, m_scr, slot, nk):
    tq, tk = ATT_TQ, ATT_TK
    m = m_scr[slot, 0:1, :]
    acc = jnp.zeros((v_ref.shape[2], tq), F32)
    for c in range(nk // tk):
        p = jnp.exp2((s_scr[slot, c * tk:(c + 1) * tk, :] - m).astype(BF16))
        acc = acc + jnp.dot(v_ref[0, v_idx, :, c * tk:(c + 1) * tk], p,
                            preferred_element_type=F32)
    return acc


def _for_each(n, fn):
    def body(i, carry):
        fn(i)
        return carry
    lax.fori_loop(0, n, body, 0)


def _ctx_or_full(n_ctx, t, fn):
    is_ctx_tile = pl.program_id(2) * ATT_TQ < n_ctx

    @pl.when(is_ctx_tile)
    def _():
        fn(n_ctx)

    @pl.when(jnp.logical_not(is_ctx_tile))
    def _():
        fn(t)


ATT_MAPS = 4


def _gqa_kernel(q_ref, k_ref, v_ref, o_ref, s_scr, m_scr, *, n_ctx, t):
    def attend(nk):
        def scores(i):
            for j in range(2):
                _scores(q_ref, k_ref, 0, s_scr, m_scr, 2 * i + j, nk)
        _for_each(ATT_MAPS // 2, scores)

        def finish(i):
            for j in range(2):
                acc = _softmax_pv(v_ref, 0, s_scr, m_scr, 2 * i + j, nk)
                ot = acc / acc[HEAD_DIM:HEAD_DIM + 1, :]
                o_ref[0, 2 * i + j] = ot.T.astype(o_ref.dtype)
        _for_each(1, lambda i: [finish(j) for j in range(ATT_MAPS // 2)])
    _ctx_or_full(n_ctx, t, attend)


def _diff_kernel(lam_ref, q_ref, k_ref, v_ref, o_ref, s_scr, m_scr, *, n_ctx, t):
    def attend(nk):
        def scores(h):
            for j in range(2):
                _scores(q_ref, k_ref, 2 * h + j, s_scr, m_scr, 2 * h + j, nk)
        _for_each(ATT_MAPS // 2, scores)

        def finish(h):
            a1 = _softmax_pv(v_ref, h, s_scr, m_scr, 2 * h, nk)
            a2 = _softmax_pv(v_ref, h, s_scr, m_scr, 2 * h + 1, nk)
            ot = a1[0:128, :] / a1[128:129, :] - lam_ref[0] * (a2[0:128, :] / a2[128:129, :])
            o_ref[0, h] = ot.T.astype(o_ref.dtype)
        _for_each(1, lambda i: [finish(j) for j in range(ATT_MAPS // 2)])
    _ctx_or_full(n_ctx, t, attend)


def _attention_call(body, name, q, k, v, n_out, k_block, v_block, extra_in, extra_args, n_ctx):
    b, n_q, _, t = q.shape
    assert n_ctx % ATT_TQ == 0 and t % ATT_TQ == 0 and n_ctx % ATT_TK == 0 and t % ATT_TK == 0
    v_rows = v.shape[2]
    return pl.pallas_call(
        functools.partial(body, n_ctx=n_ctx, t=t),
        grid=(b, n_q // ATT_MAPS, t // ATT_TQ),
        in_specs=extra_in + [
            pl.BlockSpec((1, ATT_MAPS, HEAD_DIM, ATT_TQ), lambda i, u, j: (i, u, 0, j)),
            pl.BlockSpec((1, k_block, t, HEAD_DIM), lambda i, u, j: (i, u, 0, 0)),
            pl.BlockSpec((1, v_block, v_rows, t), lambda i, u, j: (i, u, 0, 0))],
        out_specs=pl.BlockSpec((1, n_out, ATT_TQ, 128), lambda i, u, j: (i, u, j, 0)),
        out_shape=jax.ShapeDtypeStruct((b, n_q // ATT_MAPS * n_out, t, 128), BF16),
        scratch_shapes=[pltpu.VMEM((ATT_MAPS, t, ATT_TQ), F32),
                        pltpu.VMEM((ATT_MAPS, 8, ATT_TQ), F32)],
        compiler_params=_params(("arbitrary", "arbitrary", "arbitrary")),
        name=name,
    )(*extra_args, q, k, v)


def _gqa_attention(q, k, v, n_ctx):
    assert GQA_HEADS // GQA_KV_HEADS == ATT_MAPS
    return _attention_call(_gqa_kernel, "attn_gqa", q, k, v, ATT_MAPS, 1, 1, [], [], n_ctx)


def _diff_attention(lam, q, k, v, n_ctx):
    return _attention_call(_diff_kernel, "attn_diff", q, k, v, ATT_MAPS // 2, ATT_MAPS,
                           ATT_MAPS // 2, [pl.BlockSpec(memory_space=pltpu.SMEM)], [lam], n_ctx)


CONV_TILE = 256
CONV_HALO = 16
CONV_ROWS = 32


def _conv_kernel(main_ref, left_ref, right_ref, w_ref, b_ref, lng_ref, lnb_ref, o_ref,
                 u_scr, *, n_ctx, t):
    j = pl.program_id(1)
    row0 = j * CONV_TILE
    has_left = jnp.logical_and(row0 != 0, row0 != n_ctx)
    has_right = jnp.logical_and(row0 + CONV_TILE != n_ctx, row0 + CONV_TILE != t)

    def glu(ref):
        x = ref[0].astype(F32)
        return x[:, :CONV_CH] * jax.nn.sigmoid(x[:, CONV_CH:])

    left = jnp.where(has_left, glu(left_ref), 0.0)
    right = jnp.where(has_right, glu(right_ref), 0.0)
    upad = jnp.concatenate([left, glu(main_ref), right], axis=0)
    n_pad = CONV_TILE + 2 * CONV_HALO
    for r in range(8):
        u_scr[r] = upad if r == 0 else pltpu.roll(upad, n_pad - r, 0)

    pad = CONV_WIDTH // 2

    def rows_body(i, carry):
        base = pl.multiple_of(i * CONV_ROWS, CONV_ROWS)
        acc = jnp.zeros((CONV_ROWS, CONV_CH), F32)
        for tap in range(CONV_WIDTH):
            off = tap + CONV_HALO - pad
            a, r = off // 8, off % 8
            acc = acc + w_ref[tap:tap + 1, :] * u_scr[r, pl.ds(base + 8 * a, CONV_ROWS), :]
        y = acc + b_ref[...]
        mu = jnp.mean(y, axis=-1, keepdims=True)
        yc = y - mu
        var = jnp.mean(yc * yc, axis=-1, keepdims=True)
        z = yc * lax.rsqrt(var + EPS) * lng_ref[...] + lnb_ref[...]
        o_ref[0, pl.ds(base, CONV_ROWS), :] = _silu(z).astype(o_ref.dtype)
        return carry

    lax.fori_loop(0, CONV_TILE // CONV_ROWS, rows_body, 0)


def _conv_branch(p_conv, w, bias, ln_g, ln_b, n_ctx):
    b, t, wd = p_conv.shape
    assert n_ctx % CONV_TILE == 0 and t % CONV_TILE == 0
    hb = CONV_TILE // CONV_HALO
    n_halo_blocks = t // CONV_HALO
    vec = pl.BlockSpec((1, CONV_CH), lambda i, j: (0, 0))
    return pl.pallas_call(
        functools.partial(_conv_kernel, n_ctx=n_ctx, t=t),
        grid=(b, t // CONV_TILE),
        in_specs=[pl.BlockSpec((1, CONV_TILE, wd), lambda i, j: (i, j, 0)),
                  pl.BlockSpec((1, CONV_HALO, wd),
                               lambda i, j: (i, jnp.maximum(j * hb - 1, 0), 0)),
                  pl.BlockSpec((1, CONV_HALO, wd),
                               lambda i, j: (i, jnp.minimum((j + 1) * hb, n_halo_blocks - 1), 0)),
                  pl.BlockSpec((CONV_WIDTH, CONV_CH), lambda i, j: (0, 0)),
                  vec, vec, vec],
        out_specs=pl.BlockSpec((1, CONV_TILE, CONV_CH), lambda i, j: (i, j, 0)),
        out_shape=jax.ShapeDtypeStruct((b, t, CONV_CH), BF16),
        scratch_shapes=[pltpu.VMEM((8, CONV_TILE + 2 * CONV_HALO, CONV_CH), F32)],
        compiler_params=_params(("arbitrary", "arbitrary")),
        name="conv",
    )(p_conv, p_conv, p_conv, w, bias.reshape(1, -1), ln_g.reshape(1, -1), ln_b.reshape(1, -1))


HG_TILE = 256
HG_SUB = 16
HG_ROWS = 64
HG_PAIRS = HGRN_HEADS // 2
HG_PAD = 8
HG_W = HGRN_HEADS * HGRN_DK


def _dot3(a_bf16, x):
    x1 = x.astype(BF16)
    r1 = x - x1.astype(F32)
    x2 = r1.astype(BF16)
    x3 = (r1 - x2.astype(F32)).astype(BF16)
    out = jnp.dot(a_bf16, x1, preferred_element_type=F32)
    out = out + jnp.dot(a_bf16, x2, preferred_element_type=F32)
    return out + jnp.dot(a_bf16, x3, preferred_element_type=F32)


def _hgrn_kernel(q_ref, z_ref, v_ref, la_ref, lb0_ref, tri_ref, j_ref, hm_ref, o_ref,
                 st_scr, q_scr, k_scr, c_scr, v_scr, kp_scr, cp_scr, vp_scr,
                 ks_scr, cs_scr, vs_scr, acc_scr, *, rev):
    c_sz = HG_SUB
    n_chunk = HG_TILE // c_sz

    @pl.when(pl.program_id(1) == 0)
    def _():
        st_scr[...] = jnp.zeros_like(st_scr)

    q = _silu(q_ref[0].astype(F32)) * (HGRN_DK ** -0.5)
    z = z_ref[0]
    v = v_ref[0].astype(F32)
    log_sig = jnp.minimum(z, 0.0) - jnp.log(1.0 + jnp.exp(-jnp.abs(z)))
    t1 = la_ref[...]
    t2 = lb0_ref[...] + log_sig
    logf = jnp.maximum(t1, t2) + jnp.log(1.0 + jnp.exp(-jnp.abs(t1 - t2)))
    k = 1.0 - jnp.exp(logf)
    c2 = _dot3(tri_ref[...], logf * LOG2E)
    q_scr[...] = q
    k_scr[...] = k
    c_scr[...] = c2
    v_scr[...] = v
    for ref, val in ((kp_scr, k), (cp_scr, c2), (vp_scr, v)):
        ref[0:HG_PAD, :] = jnp.zeros((HG_PAD, HG_W), F32)
        ref[HG_PAD:HG_PAD + HG_TILE, :] = val
        ref[HG_PAD + HG_TILE:HG_TILE + 2 * HG_PAD, :] = jnp.zeros((HG_PAD, HG_W), F32)

    order = range(n_chunk - 1, -1, -1) if rev else range(n_chunk)
    for ci in order:
        lo = ci * c_sz
        cj = c_scr[lo:lo + c_sz, :]
        c_end = cj[0:1] if rev else cj[c_sz - 1:c_sz]
        qd = (q_scr[lo:lo + c_sz, :] * jnp.exp2(cj)).astype(BF16)
        kd = (k_scr[lo:lo + c_sz, :] * jnp.exp2(c_end - cj)).astype(BF16)
        vb = v_scr[lo:lo + c_sz, :].astype(BF16)
        dec = jnp.exp2(c_end)
        outs = []
        for p in range(HG_PAIRS):
            sl = slice(128 * p, 128 * (p + 1))
            st = st_scr[p]
            outs.append(lax.dot_general(qd[:, sl], st.astype(BF16), (((1,), (1,)), ((), ())),
                                        preferred_element_type=F32))
            upd = lax.dot_general(vb[:, sl], kd[:, sl], (((0,), (0,)), ((), ())),
                                  preferred_element_type=F32)
            st_scr[p] = st * dec[:, sl] + upd * hm_ref[...]
        acc_scr[lo:lo + c_sz, :] = jnp.concatenate(outs, axis=1)

    row_in_chunk = lax.broadcasted_iota(jnp.int32, (HG_TILE, 1), 0) % c_sz

    def shifted(ref, r, mask):
        start = HG_PAD + r if rev else HG_PAD - r
        y = ref[start:start + HG_TILE, :]
        if not mask or r == 0:
            return y
        keep = (row_in_chunk < c_sz - r) if rev else (row_in_chunk >= r)
        return jnp.where(keep, y, 0.0)

    for r in range(8):
        ks_scr[...] = shifted(kp_scr, r, True)
        cs_scr[...] = shifted(cp_scr, r, True)
        vs_scr[...] = shifted(vp_scr, r, False)
        half = HG_W // 2
        for blk in range(HG_TILE // HG_ROWS):
            base = blk * HG_ROWS
            for a in range(c_sz // 8):
                n = c_sz - 8 * a
                d_lo = 0 if rev else 8 * a
                s_lo = 8 * a if rev else 0
                starts = [base + ci * c_sz for ci in range(HG_ROWS // c_sz)]

                def rows(ref, lo_in_chunk):
                    if n == c_sz:
                        return ref[base:base + HG_ROWS, :]
                    return jnp.concatenate(
                        [ref[s + lo_in_chunk:s + lo_in_chunk + n, :] for s in starts], axis=0)

                qa = rows(q_scr, d_lo)
                ca = rows(c_scr, d_lo)
                kk = rows(ks_scr, s_lo)
                cc = rows(cs_scr, s_lo)
                vv = rows(vs_scr, s_lo)
                pw = (qa * kk * jnp.exp2(ca - cc)).astype(BF16)
                m = pw.shape[0]
                att2 = jnp.dot(jnp.concatenate([pw[:, :half], pw[:, half:]], axis=0), j_ref[...],
                               preferred_element_type=F32)
                contrib = jnp.concatenate([att2[:m], att2[m:]], axis=1) * vv
                if n == c_sz:
                    acc_scr[base:base + HG_ROWS, :] = acc_scr[base:base + HG_ROWS, :] + contrib
                else:
                    for i, s in enumerate(starts):
                        dst = s + d_lo
                        acc_scr[dst:dst + n, :] = acc_scr[dst:dst + n, :] + contrib[i * n:(i + 1) * n]

    o_ref[0] = acc_scr[...]


def _hgrn_dir(p_hq, p_hf, p_hig, la, lb0, tri, jmat, head_mask, n_ctx, rev):
    b, t, _ = p_hq.shape
    assert n_ctx % HG_TILE == 0 and t % HG_TILE == 0
    n_tiles = t // HG_TILE
    n_ctx_tiles = n_ctx // HG_TILE
    col = 1 if rev else 0

    def tile(j):
        if not rev:
            return j
        return jnp.where(j < n_ctx_tiles, n_ctx_tiles - 1 - j, n_tiles - 1 - (j - n_ctx_tiles))

    const = lambda shape: pl.BlockSpec(shape, lambda i, j: (0,) * len(shape))
    return pl.pallas_call(
        functools.partial(_hgrn_kernel, rev=rev),
        grid=(b, n_tiles),
        in_specs=[pl.BlockSpec((1, HG_TILE, HG_W), lambda i, j: (i, tile(j), 0)),
                  pl.BlockSpec((1, HG_TILE, HG_W), lambda i, j: (i, tile(j), col)),
                  pl.BlockSpec((1, HG_TILE, HG_W), lambda i, j: (i, tile(j), 0)),
                  const((1, HG_W)),
                  const((1, HG_W)),
                  const((HG_TILE, HG_TILE)), const((HG_W // 2, HG_W // 2)), const((128, 128))],
        out_specs=pl.BlockSpec((1, HG_TILE, HG_W), lambda i, j: (i, tile(j), 0)),
        out_shape=jax.ShapeDtypeStruct((b, t, HG_W), F32),
        scratch_shapes=[pltpu.VMEM((HG_PAIRS, 128, 128), F32)]
                       + [pltpu.VMEM((HG_TILE, HG_W), F32) for _ in range(4)]
                       + [pltpu.VMEM((HG_TILE + 2 * HG_PAD, HG_W), F32) for _ in range(3)]
                       + [pltpu.VMEM((HG_TILE, HG_W), F32) for _ in range(4)],
        compiler_params=_params(("arbitrary", "arbitrary")),
        name="hgrn_bwd" if rev else "hgrn_fwd",
    )(p_hq, p_hf, p_hig, la[col:col + 1], lb0[col:col + 1], tri, jmat, head_mask)


def _merge_kernel(og_ref, od_ref, cv_ref, hf_ref, hb_ref, hg_ref, gt_ref, x_ref, wb_ref, wo_ref,
                  sub_ref, hng_ref, j_ref, gc_ref, gx_ref, o_ref, *, tr, n_ctx, lam_init):
    gates = gt_ref[0]

    def gate(i):
        return jax.nn.sigmoid(gates[:, i * 1024:(i + 1) * 1024].astype(F32))

    pa = jnp.zeros((tr, 1024), F32)
    for h in range(GQA_HEADS):
        pa = pa + jnp.dot(og_ref[0, h][:, :HEAD_DIM],
                          wb_ref[0, h * HEAD_DIM:(h + 1) * HEAD_DIM, :],
                          preferred_element_type=F32)
    merged = gate(0) * pa
    pb = jnp.zeros((tr, 1024), F32)
    for h in range(DIFF_HEADS):
        od = od_ref[0, h].astype(F32)
        ms = jnp.mean(od * od, axis=-1, keepdims=True)
        on = od * lax.rsqrt(ms + EPS) * sub_ref[...] * (1.0 - lam_init)
        pb = pb + jnp.dot(on.astype(BF16), wb_ref[1, h * 128:(h + 1) * 128, :],
                          preferred_element_type=F32)
    merged = merged + gate(1) * pb
    merged = merged + gate(2) * jnp.dot(cv_ref[0], wb_ref[2], preferred_element_type=F32)
    od = hf_ref[0] + hb_ref[0]
    ms = jnp.dot((od * od).astype(BF16), j_ref[...], preferred_element_type=F32)
    on = od * lax.rsqrt(ms + EPS) * hng_ref[...] * _silu(hg_ref[0].astype(F32))
    merged = merged + gate(3) * jnp.dot(on.astype(BF16), wb_ref[3], preferred_element_type=F32)

    mix = jnp.dot(merged.astype(BF16), wo_ref[...], preferred_element_type=F32)
    g1 = jnp.where(_row_is_ctx(pl.program_id(1), tr, n_ctx), gc_ref[...], gx_ref[0])
    o_ref[0] = x_ref[0] + g1 * mix


def _merge(o_gqa, o_diff, conv_o, hg_f, hg_b, p_hig, p_gate, x, wb, wo, subln_g, hgrn_g, jmat64,
           g_c, g_x, n_ctx, lam_init):
    b, t, d = x.shape
    tr = _pick(t, 272, 16)
    const = lambda shape: pl.BlockSpec(shape, lambda i, j: (0,) * len(shape))
    return pl.pallas_call(
        functools.partial(_merge_kernel, tr=tr, n_ctx=n_ctx, lam_init=lam_init),
        grid=(b, t // tr),
        in_specs=[pl.BlockSpec((1, GQA_HEADS, tr, 128), lambda i, j: (i, 0, j, 0)),
                  pl.BlockSpec((1, DIFF_HEADS, tr, 128), lambda i, j: (i, 0, j, 0)),
                  pl.BlockSpec((1, tr, CONV_CH), lambda i, j: (i, j, 0)),
                  pl.BlockSpec((1, tr, HG_W), lambda i, j: (i, j, 0)),
                  pl.BlockSpec((1, tr, HG_W), lambda i, j: (i, j, 0)),
                  pl.BlockSpec((1, tr, HG_W), lambda i, j: (i, j, 1)),
                  pl.BlockSpec((1, tr, 4 * d), lambda i, j: (i, j, 0)),
                  pl.BlockSpec((1, tr, d), lambda i, j: (i, j, 0)),
                  const((4, BRANCH_W, d)), const((d, d)),
                  const((1, 128)), const((1, HG_W)), const((HG_W, HG_W)),
                  const((1, d)),
                  pl.BlockSpec((1, 1, d), lambda i, j: (i, 0, 0))],
        out_specs=pl.BlockSpec((1, tr, d), lambda i, j: (i, j, 0)),
        out_shape=jax.ShapeDtypeStruct((b, t, d), F32),
        compiler_params=_params(("arbitrary", "arbitrary")),
        name="merge",
    )(o_gqa, o_diff, conv_o, hg_f, hg_b, p_hig, p_gate, x, wb, wo, subln_g, hgrn_g, jmat64,
      g_c, g_x)


def _swiglu_step(h, w1, w3, w2):
    a = jnp.dot(h, w1, preferred_element_type=F32)
    g = jnp.dot(h, w3, preferred_element_type=F32)
    return jnp.dot((_silu(a) * g).astype(BF16), w2, preferred_element_type=F32)


def _ffn_kernel(h_ref, w1_ref, w3_ref, w2_ref, x_ref, gc_ref, gx_ref, o_ref, acc_scr, *, tr, n_ctx):
    kf = pl.program_id(2)

    @pl.when(kf == 0)
    def _():
        acc_scr[...] = jnp.zeros_like(acc_scr)

    acc_scr[...] += _swiglu_step(h_ref[0], w1_ref[...], w3_ref[...], w2_ref[...])

    @pl.when(kf == pl.num_programs(2) - 1)
    def _():
        g2 = jnp.where(_row_is_ctx(pl.program_id(1), tr, n_ctx), gc_ref[...], gx_ref[0])
        o_ref[0] = x_ref[0] + g2 * acc_scr[...]


def _ffn(h, w1, w3, w2, x, g_c, g_x, n_ctx):
    b, t, d = x.shape
    dff = w1.shape[1]
    tr = _pick(t, 1088, 16)
    tf = _pick(dff, 512, 128)
    return pl.pallas_call(
        functools.partial(_ffn_kernel, tr=tr, n_ctx=n_ctx),
        grid=(b, t // tr, dff // tf),
        in_specs=[pl.BlockSpec((1, tr, d), lambda i, j, f: (i, j, 0)),
                  pl.BlockSpec((d, tf), lambda i, j, f: (0, f)),
                  pl.BlockSpec((d, tf), lambda i, j, f: (0, f)),
                  pl.BlockSpec((tf, d), lambda i, j, f: (f, 0)),
                  pl.BlockSpec((1, tr, d), lambda i, j, f: (i, j, 0)),
                  pl.BlockSpec((1, d), lambda i, j, f: (0, 0)),
                  pl.BlockSpec((1, 1, d), lambda i, j, f: (i, 0, 0))],
        out_specs=pl.BlockSpec((1, tr, d), lambda i, j, f: (i, j, 0)),
        out_shape=jax.ShapeDtypeStruct((b, t, d), F32),
        scratch_shapes=[pltpu.VMEM((tr, d), F32)],
        input_output_aliases={4: 0},
        compiler_params=_params(("arbitrary", "arbitrary", "arbitrary")),
        name="ffn",
    )(h, w1, w3, w2, x, g_c, g_x)


MOE_TILE = 512
GATHER_ROWS = 512
COMBINE_ROWS = 544


def _gather_kernel(idx_ref, src_ref, o_ref, sem, *, rows):
    def copy(r):
        return pltpu.make_async_copy(src_ref.at[idx_ref[0, 0, r]], o_ref.at[r], sem)

    def start(r2, carry):
        copy(2 * r2).start(priority=0)
        copy(2 * r2 + 1).start(priority=1)
        return carry

    def wait(r, carry):
        copy(r).wait()
        return carry

    lax.fori_loop(0, rows // 2, start, 0)
    lax.fori_loop(0, rows, wait, 0)


def _gather_rows(src_rows, idx):
    n_out = idx.shape[0]
    n_src, d = src_rows.shape
    assert d % 128 == 0
    rows = _pick(n_out, GATHER_ROWS, 16)
    steps = n_out // rows
    tiles = pl.pallas_call(
        functools.partial(_gather_kernel, rows=rows),
        grid=(steps,),
        in_specs=[pl.BlockSpec((1, 1, rows), lambda i: (i, 0, 0), memory_space=pltpu.SMEM),
                  pl.BlockSpec(memory_space=pl.ANY)],
        out_specs=pl.BlockSpec((rows, d // 128, 128), lambda i: (i, 0, 0)),
        out_shape=jax.ShapeDtypeStruct((n_out, d // 128, 128), src_rows.dtype),
        scratch_shapes=[pltpu.SemaphoreType.DMA(())],
        compiler_params=_params(("arbitrary",)),
        name="moe_gather",
    )(idx.reshape(steps, 1, rows), src_rows.reshape(n_src, d // 128, 128))
    return tiles.reshape(n_out, d)


def _expert_kernel(te_ref, nu_ref, x_ref, w1_ref, w3_ref, w2_ref, o_ref, acc_scr):
    i = pl.program_id(0)
    kf = pl.program_id(1)
    last = kf == pl.num_programs(1) - 1
    used = i < nu_ref[0]

    @pl.when(jnp.logical_and(used, kf == 0))
    def _():
        acc_scr[...] = jnp.zeros_like(acc_scr)

    @pl.when(used)
    def _():
        acc_scr[...] += _swiglu_step(x_ref[...].astype(BF16), w1_ref[0], w3_ref[0], w2_ref[0])

    @pl.when(jnp.logical_and(used, last))
    def _():
        o_ref[...] = acc_scr[...]

    @pl.when(jnp.logical_and(jnp.logical_not(used), last))
    def _():
        o_ref[...] = jnp.zeros_like(o_ref)


def _expert_ffn(xs, w1, w3, w2, tile_expert, n_used):
    n_sorted, d = xs.shape
    dff = w1.shape[2]
    tf = _pick(dff, 896, 128)
    grid_spec = pltpu.PrefetchScalarGridSpec(
        num_scalar_prefetch=2,
        grid=(n_sorted // MOE_TILE, dff // tf),
        in_specs=[pl.BlockSpec((MOE_TILE, d), lambda i, f, te, nu: (i, 0)),
                  pl.BlockSpec((1, d, tf), lambda i, f, te, nu: (te[i], 0, f)),
                  pl.BlockSpec((1, d, tf), lambda i, f, te, nu: (te[i], 0, f)),
                  pl.BlockSpec((1, tf, d), lambda i, f, te, nu: (te[i], f, 0))],
        out_specs=pl.BlockSpec((MOE_TILE, d), lambda i, f, te, nu: (i, 0)),
        scratch_shapes=[pltpu.VMEM((MOE_TILE, d), F32)])
    return pl.pallas_call(
        _expert_kernel,
        grid_spec=grid_spec,
        out_shape=jax.ShapeDtypeStruct((n_sorted, d), F32),
        compiler_params=_params(("arbitrary", "arbitrary")),
        name="moe_experts",
    )(tile_expert, n_used, xs, w1, w3, w2)


def _combine_kernel(rw_ref, y1_ref, y2_ref, x_ref, gc_ref, gx_ref, o_ref, *, tr, n_ctx):
    rw = rw_ref[0]
    f = rw[:, 2:3] * y1_ref[0, 0] + rw[:, 3:4] * y2_ref[0, 0]
    g2 = jnp.where(_row_is_ctx(pl.program_id(1), tr, n_ctx), gc_ref[...], gx_ref[0])
    o_ref[0] = x_ref[0] + g2 * f


def _combine(route, y_tok, x, g_c, g_x, n_ctx):
    b, t, d = x.shape
    tr = _pick(t, COMBINE_ROWS, 16)
    return pl.pallas_call(
        functools.partial(_combine_kernel, tr=tr, n_ctx=n_ctx),
        grid=(b, t // tr),
        in_specs=[pl.BlockSpec((1, tr, 128), lambda i, j: (i, j, 0)),
                  pl.BlockSpec((1, 1, tr, d), lambda i, j: (0, i, j, 0)),
                  pl.BlockSpec((1, 1, tr, d), lambda i, j: (1, i, j, 0)),
                  pl.BlockSpec((1, tr, d), lambda i, j: (i, j, 0)),
                  pl.BlockSpec((1, d), lambda i, j: (0, 0)),
                  pl.BlockSpec((1, 1, d), lambda i, j: (i, 0, 0))],
        out_specs=pl.BlockSpec((1, tr, d), lambda i, j: (i, j, 0)),
        out_shape=jax.ShapeDtypeStruct((b, t, d), F32),
        input_output_aliases={3: 0},
        compiler_params=_params(("arbitrary", "arbitrary")),
        name="moe_combine",
    )(route, y_tok, y_tok, x, g_c, g_x)


def _moe_plan(e1, e2):
    n = e1.shape[0]
    e = jnp.concatenate([e1, e2])
    onehot = (e[:, None] == jnp.arange(N_EXPERTS, dtype=jnp.int32)[None, :]).astype(jnp.int32)
    counts = jnp.sum(onehot, axis=0)
    rank = jnp.sum((jnp.cumsum(onehot, axis=0) - onehot) * onehot, axis=1)
    padded = ((counts + MOE_TILE - 1) // MOE_TILE) * MOE_TILE
    ends = jnp.cumsum(padded)
    slot = jnp.sum((ends - padded)[None, :] * onehot, axis=1) + rank
    n_tiles = (2 * n + N_EXPERTS * (MOE_TILE - 1)) // MOE_TILE
    token = jnp.tile(jnp.arange(n, dtype=jnp.int32), 2)
    tokens_sorted = jnp.zeros((n_tiles * MOE_TILE,), jnp.int32).at[slot].set(token)
    tile_start = jnp.arange(n_tiles, dtype=jnp.int32) * MOE_TILE
    tile_expert = jnp.minimum(jnp.sum((ends[None, :] <= tile_start[:, None]).astype(jnp.int32), axis=1),
                              N_EXPERTS - 1)
    n_used = (ends[-1] // MOE_TILE).reshape(1)
    return slot[:n], slot[n:], tokens_sorted, tile_expert, n_used


def _moe(h_f32, route, w1, w3, w2, x, g_c, g_x, n_ctx):
    b, t, d = x.shape
    e1 = route[:, :, 0].reshape(-1).astype(jnp.int32)
    e2 = route[:, :, 1].reshape(-1).astype(jnp.int32)
    pos1, pos2, tokens_sorted, tile_expert, n_used = _moe_plan(e1, e2)
    xs = _gather_rows(h_f32.reshape(b * t, d), tokens_sorted)
    ys = _expert_ffn(xs, w1, w3, w2, tile_expert, n_used)
    y_tok = _gather_rows(ys, jnp.concatenate([pos1, pos2])).reshape(2, b, t, d)
    return _combine(route, y_tok, x, g_c, g_x, n_ctx)


def _router_kernel(h_ref, r_ref, o_ref):
    logits = jnp.dot(h_ref[0], r_ref[...], preferred_element_type=F32,
                     precision=lax.Precision.HIGHEST)
    lane = lax.broadcasted_iota(jnp.int32, logits.shape, 1).astype(F32)
    neg = -jnp.inf
    logits = jnp.where(lane < N_EXPERTS, logits, neg)
    m1 = jnp.max(logits, axis=-1, keepdims=True)
    i1 = jnp.min(jnp.where(logits == m1, lane, 128.0), axis=-1, keepdims=True)
    rest = jnp.where(lane == i1, neg, logits)
    m2 = jnp.max(rest, axis=-1, keepdims=True)
    i2 = jnp.min(jnp.where(rest == m2, lane, 128.0), axis=-1, keepdims=True)
    e2 = jnp.exp(m2 - m1)
    w1 = 1.0 / (1.0 + e2)
    w2 = e2 / (1.0 + e2)
    o_ref[0] = (jnp.where(lane == 0.0, i1, 0.0) + jnp.where(lane == 1.0, i2, 0.0)
                + jnp.where(lane == 2.0, w1, 0.0) + jnp.where(lane == 3.0, w2, 0.0))


def _router(h_f32, router_pad):
    b, t, d = h_f32.shape
    tr = _pick(t, 1088, 16)
    return pl.pallas_call(
        _router_kernel,
        grid=(b, t // tr),
        in_specs=[pl.BlockSpec((1, tr, d), lambda i, j: (i, j, 0)),
                  pl.BlockSpec((d, 128), lambda i, j: (0, 0))],
        out_specs=pl.BlockSpec((1, tr, 128), lambda i, j: (i, j, 0)),
        out_shape=jax.ShapeDtypeStruct((b, t, 128), F32),
        compiler_params=_params(("arbitrary", "arbitrary")),
        name="router",
    )(h_f32, router_pad)


def _rope_tables(seq, n_ctx):
    rows = seq // GRID_W
    row = jnp.repeat(jnp.arange(rows, dtype=F32), GRID_W)
    col = (jnp.arange(seq) % GRID_W).astype(F32)
    axis_dim = HEAD_DIM // 2
    inv_freq = ROPE_THETA ** (-jnp.arange(0, axis_dim, 2, dtype=F32) / axis_dim)
    ang_r = row[:, None] * inv_freq
    ang_c = col[:, None] * inv_freq
    cos = jnp.concatenate([jnp.cos(ang_r), jnp.cos(ang_r), jnp.cos(ang_c), jnp.cos(ang_c)], axis=-1)
    sin = jnp.concatenate([-jnp.sin(ang_r), jnp.sin(ang_r), -jnp.sin(ang_c), jnp.sin(ang_c)], axis=-1)
    cos = jnp.concatenate([jnp.ones((n_ctx, HEAD_DIM), F32), cos], axis=0)
    sin = jnp.concatenate([jnp.zeros((n_ctx, HEAD_DIM), F32), sin], axis=0)
    return jnp.tile(cos, (1, 2)), jnp.tile(sin, (1, 2))


def _block_ones(n, blk, value=1.0, dtype=BF16):
    i = np.arange(n) // blk
    return jnp.asarray((i[:, None] == i[None, :]).astype(np.float32) * value, dtype)


def _chunk_tri(n, chunk, rev):
    i = np.arange(n)
    same = (i[:, None] // chunk) == (i[None, :] // chunk)
    tri = (i[None, :] >= i[:, None]) if rev else (i[None, :] <= i[:, None])
    return jnp.asarray((same & tri).astype(np.float32), BF16)


def kernel(x, c, ctx, c_ctx, ada_w, ada_b, norm1_g, norm2_g, w_in, qk_norm_g, diff_lambda,
           diff_subln_g, conv_w, conv_b, conv_ln_g, conv_ln_b, hgrn_lb_logits, hgrn_norm_g,
           w_branch, w_out, ffn_w1, ffn_w3, ffn_w2, moe_router, moe_w1, moe_w3, moe_w2):
    b, seq, d = x.shape
    n_ctx = ctx.shape[1]
    t = n_ctx + seq
    depth = ada_w.shape[0]

    cos, sin = _rope_tables(seq, n_ctx)
    j_head_mean = _block_ones(128, HEAD_DIM, 1.0 / HEAD_DIM)
    j_hg_sum = _block_ones(HG_W // 2, HGRN_DK)
    j_hg_mean = _block_ones(HG_W, HGRN_DK, 1.0 / HGRN_DK)
    head_mask = _block_ones(128, HGRN_DK, 1.0, F32)
    tri_f = _chunk_tri(HG_TILE, HG_SUB, False)
    tri_b = _chunk_tri(HG_TILE, HG_SUB, True)

    lb = jnp.cumsum(jax.nn.softmax(hgrn_lb_logits.astype(F32), axis=0), axis=0)
    lb = lb - lb[:1]
    log_lb = jnp.log(lb)
    log_1m_lb = jnp.log1p(-lb)

    rows = ((b + 1 + 7) // 8) * 8
    cvec = jnp.zeros((rows, d), F32).at[:b].set(c).at[b].set(c_ctx)
    mod = _ada(cvec, ada_w, ada_b).reshape(depth, rows, 6, d)

    xc = jnp.concatenate([ctx, x], axis=1)
    q_scale = LOG2E * HEAD_DIM ** -0.5

    for l in range(depth):
        mx = mod[l, :b]
        mc = mod[l, b]
        vx = lambda i: mx[:, i:i + 1, :]
        vc = lambda i: mc[i:i + 1, :]
        w_l = w_in[l].astype(BF16)

        h = _normmod(xc, norm1_g[l], vc(0), vc(1), vx(0), vx(1), n_ctx).reshape(b * t, d)
        proj = lambda cols, dt: _matmul(h, w_l[:, cols[0]:cols[1]], dt).reshape(b, t, -1)
        p_attn = proj(COL_ATTN, BF16)
        p_conv = proj(COL_CONV, BF16)
        p_hq = proj(COL_HQ, BF16)
        p_hf = proj(COL_HF, F32)
        p_hig = proj(COL_HIG, BF16)
        p_gate = proj(COL_GATE, BF16)

        g = qk_norm_g[l].astype(F32)
        gains = jnp.tile(jnp.stack([g[0] * q_scale, g[1], g[2] * q_scale, g[3]]), (1, 2))
        lam_init = 0.8 - 0.6 * math.exp(-0.3 * l)
        lp = diff_lambda[l].astype(F32)
        lam = (jnp.exp(jnp.sum(lp[0] * lp[1])) - jnp.exp(jnp.sum(lp[2] * lp[3]))
               + lam_init).reshape(1)
        qg, kg, vg, qd, kd, vd = _attn_prep(p_attn, cos, sin, gains, j_head_mean)
        o_gqa = _gqa_attention(qg, kg, vg, n_ctx)
        o_diff = _diff_attention(lam, qd, kd, vd, n_ctx)

        conv_o = _conv_branch(p_conv, conv_w[l], conv_b[l], conv_ln_g[l], conv_ln_b[l], n_ctx)

        hg_f = _hgrn_dir(p_hq, p_hf, p_hig, log_lb[l], log_1m_lb[l], tri_f, j_hg_sum, head_mask,
                         n_ctx, False)
        hg_b = _hgrn_dir(p_hq, p_hf, p_hig, log_lb[l], log_1m_lb[l], tri_b, j_hg_sum, head_mask,
                         n_ctx, True)

        xc = _merge(o_gqa, o_diff, conv_o, hg_f, hg_b, p_hig, p_gate, xc,
                    w_branch[l].astype(BF16), w_out[l].astype(BF16),
                    diff_subln_g[l].reshape(1, -1), jnp.tile(hgrn_norm_g[l], HGRN_HEADS).reshape(1, -1),
                    j_hg_mean, vc(2), vx(2), n_ctx, lam_init)

        if l % 2 == 0:
            h2 = _normmod(xc, norm2_g[l], vc(3), vc(4), vx(3), vx(4), n_ctx)
            i = l // 2
            xc = _ffn(h2, ffn_w1[i].astype(BF16), ffn_w3[i].astype(BF16), ffn_w2[i].astype(BF16),
                      xc, vc(5), vx(5), n_ctx)
        else:
            h2f = _normmod(xc, norm2_g[l], vc(3), vc(4), vx(3), vx(4), n_ctx, out_dtype=F32)
            i = l // 2
            router_pad = jnp.zeros((d, 128), F32).at[:, :N_EXPERTS].set(moe_router[i])
            route = _router(h2f, router_pad)
            xc = _moe(h2f, route, moe_w1[i].astype(BF16), moe_w3[i].astype(BF16),
                      moe_w2[i].astype(BF16), xc, vc(5), vx(5), n_ctx)
    return xc[:, n_ctx:]
```

```python
import functools
import math

import jax
import jax.numpy as jnp
import numpy as np
from jax import lax
from jax.experimental import pallas as pl
from jax.experimental.pallas import tpu as pltpu

F32 = jnp.float32
BF16 = jnp.bfloat16

GRID_W = 64
HEAD_DIM = 64
ROPE_THETA = 10000.0
EPS = 1e-6
GQA_HEADS = 8
GQA_KV_HEADS = 2
DIFF_HEADS = 4
CONV_CH = 512
CONV_WIDTH = 31
HGRN_HEADS = 8
HGRN_DK = 64
HGRN_CHUNK = 64
BRANCH_W = 512
N_EXPERTS = 8
LOG2E = 1.4426950408889634

COL_ATTN = (0, 2304)
COL_CONV = (2304, 3328)
COL_HQ = (3328, 3840)
COL_HF = (3840, 4864)
COL_HIG = (4864, 5888)
COL_GATE = (5888, 9984)

VMEM_LIMIT_BYTES = 50 * 1024 * 1024


def _params(sem):
    return pltpu.CompilerParams(dimension_semantics=sem, vmem_limit_bytes=VMEM_LIMIT_BYTES)


def _pick(n, target, mult):
    best = None
    for d in range(mult, min(n, target) + 1, mult):
        if n % d == 0:
            best = d
    assert best is not None, (n, target, mult)
    return best


def _silu(x):
    return x * jax.nn.sigmoid(x)


def _row_is_ctx(tile_idx, tr, n_ctx):
    row = tile_idx * tr + lax.broadcasted_iota(jnp.int32, (tr, 1), 0)
    return row < n_ctx


def _ada_kernel(c_ref, w_ref, b_ref, o_ref):
    s = _silu(c_ref[...])
    o_ref[0] = jnp.dot(s.astype(BF16), w_ref[0].astype(BF16),
                       preferred_element_type=F32) + b_ref[0]


def _ada(cvec, ada_w, ada_b):
    depth, d, n = ada_w.shape
    rows = cvec.shape[0]
    tn = _pick(n, 1536, 128)
    return pl.pallas_call(
        _ada_kernel,
        grid=(depth, n // tn),
        in_specs=[pl.BlockSpec((rows, d), lambda l, j: (0, 0)),
                  pl.BlockSpec((1, d, tn), lambda l, j: (l, 0, j)),
                  pl.BlockSpec((1, 1, tn), lambda l, j: (l, 0, j))],
        out_specs=pl.BlockSpec((1, rows, tn), lambda l, j: (l, 0, j)),
        out_shape=jax.ShapeDtypeStruct((depth, rows, n), F32),
        compiler_params=_params(("arbitrary", "arbitrary")),
        name="ada",
    )(cvec, ada_w, ada_b.reshape(depth, 1, n))


def _normmod_kernel(x_ref, g_ref, shc_ref, scc_ref, shx_ref, scx_ref, o_ref, *, tr, n_ctx):
    x = x_ref[0]
    ms = jnp.mean(x * x, axis=-1, keepdims=True)
    y = x * lax.rsqrt(ms + EPS) * g_ref[...]
    is_ctx = _row_is_ctx(pl.program_id(1), tr, n_ctx)
    sh = jnp.where(is_ctx, shc_ref[...], shx_ref[0])
    sc = jnp.where(is_ctx, scc_ref[...], scx_ref[0])
    o_ref[0] = (y * (1.0 + sc) + sh).astype(o_ref.dtype)


def _normmod(x, g, sh_c, sc_c, sh_x, sc_x, n_ctx, out_dtype=BF16):
    b, t, d = x.shape
    tr = _pick(t, 1088, 16)
    vec_c = pl.BlockSpec((1, d), lambda i, j: (0, 0))
    vec_x = pl.BlockSpec((1, 1, d), lambda i, j: (i, 0, 0))
    return pl.pallas_call(
        functools.partial(_normmod_kernel, tr=tr, n_ctx=n_ctx),
        grid=(b, t // tr),
        in_specs=[pl.BlockSpec((1, tr, d), lambda i, j: (i, j, 0)),
                  vec_c, vec_c, vec_c, vec_x, vec_x],
        out_specs=pl.BlockSpec((1, tr, d), lambda i, j: (i, j, 0)),
        out_shape=jax.ShapeDtypeStruct((b, t, d), out_dtype),
        compiler_params=_params(("arbitrary", "arbitrary")),
        name="normmod",
    )(x, g.reshape(1, d), sh_c, sc_c, sh_x, sc_x)


def _mm_kernel(a_ref, w_ref, o_ref):
    o_ref[...] = jnp.dot(a_ref[...], w_ref[...],
                         preferred_element_type=F32).astype(o_ref.dtype)


def _matmul(a, w, out_dtype):
    m, k = a.shape
    n = w.shape[1]
    tm = _pick(m, 2048, 256)
    tn = _pick(n, 1024, 128)
    return pl.pallas_call(
        _mm_kernel,
        grid=(m // tm, n // tn),
        in_specs=[pl.BlockSpec((tm, k), lambda i, j: (i, 0)),
                  pl.BlockSpec((k, tn), lambda i, j: (0, j))],
        out_specs=pl.BlockSpec((tm, tn), lambda i, j: (i, j)),
        out_shape=jax.ShapeDtypeStruct((m, n), out_dtype),
        compiler_params=_params(("arbitrary", "arbitrary")),
        name="proj",
    )(a, w)


def _prep_kernel(p_ref, cos_ref, sin_ref, g_ref, j_ref, qg_ref, kg_ref, vg_ref,
                 qd_ref, kd_ref, vd_ref):
    cos = cos_ref[...]
    sin = sin_ref[...]
    lane = lax.broadcasted_iota(jnp.int32, (1, 128), 1)
    first_half = (lane % 32) < 16
    low_head = lane < HEAD_DIM

    def norm_rope(col, gi):
        x = p_ref[0, :, col:col + 128].astype(F32)
        ms = jnp.dot((x * x).astype(BF16), j_ref[...], preferred_element_type=F32)
        xn = x * lax.rsqrt(ms + EPS) * g_ref[gi:gi + 1, :]
        swapped = jnp.where(first_half, pltpu.roll(xn, 112, 1), pltpu.roll(xn, 16, 1))
        return xn * cos + swapped * sin

    def put_heads(ref, first, y):
        ref[0, first] = y[:, :HEAD_DIM].astype(ref.dtype)
        ref[0, first + 1] = y[:, HEAD_DIM:].astype(ref.dtype)

    def put_heads_t(ref, first, y):
        yt = y.T
        ref[0, first] = yt[:HEAD_DIM, :].astype(ref.dtype)
        ref[0, first + 1] = yt[HEAD_DIM:, :].astype(ref.dtype)

    tr = cos.shape[0]
    for g in range(4):
        put_heads_t(qg_ref, 2 * g, norm_rope(128 * g, 0))
    put_heads(kg_ref, 0, norm_rope(512, 1))
    for g in range(4):
        put_heads_t(qd_ref, 2 * g, norm_rope(768 + 128 * g, 2))
    for g in range(4):
        put_heads(kd_ref, 2 * g, norm_rope(1280 + 128 * g, 3))
    avt = p_ref[0, :, 640:768].astype(F32).T
    def ones_first_row(n):
        row = lax.broadcasted_iota(jnp.int32, (n, tr), 0)
        return jnp.where(row == 0, 1.0, 0.0).astype(vg_ref.dtype)

    for h in range(GQA_KV_HEADS):
        vg_ref[0, h, 0:HEAD_DIM, :] = avt[h * HEAD_DIM:(h + 1) * HEAD_DIM, :].astype(vg_ref.dtype)
        vg_ref[0, h, HEAD_DIM:2 * HEAD_DIM, :] = ones_first_row(HEAD_DIM)
    for h in range(DIFF_HEADS):
        bvt = p_ref[0, :, 1792 + 128 * h:1792 + 128 * (h + 1)].astype(F32).T
        vd_ref[0, h, 0:128, :] = bvt.astype(vd_ref.dtype)
        vd_ref[0, h, 128:128 + ATT_VPAD, :] = ones_first_row(ATT_VPAD)


def _attn_prep(p_attn, cos, sin, gains, jmat):
    b, t, w = p_attn.shape
    tr = _pick(t, 256, 128)
    heads = lambda n, width: pl.BlockSpec((1, n, tr, width), lambda i, j: (i, 0, j, 0))
    shape = lambda n, width: jax.ShapeDtypeStruct((b, n, t, width), BF16)
    heads_t = lambda n, rows: pl.BlockSpec((1, n, rows, tr), lambda i, j: (i, 0, 0, j))
    shape_t = lambda n, rows: jax.ShapeDtypeStruct((b, n, rows, t), BF16)
    return pl.pallas_call(
        _prep_kernel,
        grid=(b, t // tr),
        in_specs=[pl.BlockSpec((1, tr, w), lambda i, j: (i, j, 0)),
                  pl.BlockSpec((tr, 128), lambda i, j: (j, 0)),
                  pl.BlockSpec((tr, 128), lambda i, j: (j, 0)),
                  pl.BlockSpec((4, 128), lambda i, j: (0, 0)),
                  pl.BlockSpec((128, 128), lambda i, j: (0, 0))],
        out_specs=[heads_t(GQA_HEADS, HEAD_DIM), heads(GQA_KV_HEADS, HEAD_DIM),
                   heads_t(GQA_KV_HEADS, 128),
                   heads_t(2 * DIFF_HEADS, HEAD_DIM), heads(2 * DIFF_HEADS, HEAD_DIM),
                   heads_t(DIFF_HEADS, 128 + ATT_VPAD)],
        out_shape=[shape_t(GQA_HEADS, HEAD_DIM), shape(GQA_KV_HEADS, HEAD_DIM),
                   shape_t(GQA_KV_HEADS, 128),
                   shape_t(2 * DIFF_HEADS, HEAD_DIM), shape(2 * DIFF_HEADS, HEAD_DIM),
                   shape_t(DIFF_HEADS, 128 + ATT_VPAD)],
        compiler_params=_params(("arbitrary", "arbitrary")),
        name="attn_prep",
    )(p_attn, cos, sin, gains, jmat)


ATT_TQ = 256
ATT_TK = 256
ATT_VPAD = 16


def _scores(q_ref, k_ref, k_idx, s_scr, m_scr, slot, nk):
    tq, tk = ATT_TQ, ATT_TK
    qt = q_ref[0, slot]
    m_part = jnp.full((8, tq), -jnp.inf, F32)
    for c in range(nk // tk):
        s = jnp.dot(k_ref[0, k_idx, c * tk:(c + 1) * tk, :], qt, preferred_element_type=F32)
        s_scr[slot, c * tk:(c + 1) * tk, :] = s
        for u in range(tk // 8):
            m_part = jnp.maximum(m_part, s[8 * u:8 * (u + 1), :])
    m_scr[slot] = jnp.broadcast_to(jnp.max(m_part, axis=0, keepdims=True), (8, tq))


def _softmax_pv(v_ref, v_idx, s_scr, m_scr, slot, nk):
    tq, tk = ATT_TQ, ATT_TK
    m = m_scr[slot, 0:1, :]
    acc = jnp.zeros((v_ref.shape[2], tq), F32)
    for c in range(nk // tk):
        p = jnp.exp2((s_scr[slot, c * tk:(c + 1) * tk, :] - m).astype(BF16))
        acc = acc + jnp.dot(v_ref[0, v_idx, :, c * tk:(c + 1) * tk], p,
                            preferred_element_type=F32)
    return acc


def _for_each(n, fn):
    def body(i, carry):
        fn(i)
        return carry
    lax.fori_loop(0, n, body, 0)


def _ctx_or_full(n_ctx, t, fn):
    is_ctx_tile = pl.program_id(2) * ATT_TQ < n_ctx

    @pl.when(is_ctx_tile)
    def _():
        fn(n_ctx)

    @pl.when(jnp.logical_not(is_ctx_tile))
    def _():
        fn(t)


ATT_MAPS = 4


def _gqa_kernel(q_ref, k_ref, v_ref, o_ref, s_scr, m_scr, *, n_ctx, t):
    def attend(nk):
        def scores(i):
            for j in range(2):
                _scores(q_ref, k_ref, 0, s_scr, m_scr, 2 * i + j, nk)
        _for_each(ATT_MAPS // 2, scores)

        def finish(i):
            halves = []
            for j in range(2):
                acc = _softmax_pv(v_ref, 0, s_scr, m_scr, 2 * i + j, nk)
                halves.append(acc[0:HEAD_DIM, :] / acc[HEAD_DIM:HEAD_DIM + 1, :])
            o_ref[0, i] = jnp.concatenate(halves, axis=0).T.astype(o_ref.dtype)
        _for_each(1, lambda i: [finish(j) for j in range(ATT_MAPS // 2)])
    _ctx_or_full(n_ctx, t, attend)


def _diff_kernel(lam_ref, q_ref, k_ref, v_ref, o_ref, s_scr, m_scr, *, n_ctx, t):
    def attend(nk):
        def scores(h):
            for j in range(2):
                _scores(q_ref, k_ref, 2 * h + j, s_scr, m_scr, 2 * h + j, nk)
        _for_each(ATT_MAPS // 2, scores)

        def finish(h):
            a1 = _softmax_pv(v_ref, h, s_scr, m_scr, 2 * h, nk)
            a2 = _softmax_pv(v_ref, h, s_scr, m_scr, 2 * h + 1, nk)
            ot = a1[0:128, :] / a1[128:129, :] - lam_ref[0] * (a2[0:128, :] / a2[128:129, :])
            o_ref[0, h] = ot.T.astype(o_ref.dtype)
        _for_each(1, lambda i: [finish(j) for j in range(ATT_MAPS // 2)])
    _ctx_or_full(n_ctx, t, attend)


def _attention_call(body, name, q, k, v, n_out, k_block, v_block, extra_in, extra_args, n_ctx):
    b, n_q, _, t = q.shape
    assert n_ctx % ATT_TQ == 0 and t % ATT_TQ == 0 and n_ctx % ATT_TK == 0 and t % ATT_TK == 0
    v_rows = v.shape[2]
    return pl.pallas_call(
        functools.partial(body, n_ctx=n_ctx, t=t),
        grid=(b, n_q // ATT_MAPS, t // ATT_TQ),
        in_specs=extra_in + [
            pl.BlockSpec((1, ATT_MAPS, HEAD_DIM, ATT_TQ), lambda i, u, j: (i, u, 0, j)),
            pl.BlockSpec((1, k_block, t, HEAD_DIM), lambda i, u, j: (i, u, 0, 0)),
            pl.BlockSpec((1, v_block, v_rows, t), lambda i, u, j: (i, u, 0, 0))],
        out_specs=pl.BlockSpec((1, n_out, ATT_TQ, 128), lambda i, u, j: (i, u, j, 0)),
        out_shape=jax.ShapeDtypeStruct((b, n_q // ATT_MAPS * n_out, t, 128), BF16),
        scratch_shapes=[pltpu.VMEM((ATT_MAPS, t, ATT_TQ), F32),
                        pltpu.VMEM((ATT_MAPS, 8, ATT_TQ), F32)],
        compiler_params=_params(("arbitrary", "arbitrary", "arbitrary")),
        name=name,
    )(*extra_args, q, k, v)


def _gqa_attention(q, k, v, n_ctx):
    assert GQA_HEADS // GQA_KV_HEADS == ATT_MAPS
    return _attention_call(_gqa_kernel, "attn_gqa", q, k, v, ATT_MAPS // 2, 1, 1, [], [], n_ctx)


def _diff_attention(lam, q, k, v, n_ctx):
    return _attention_call(_diff_kernel, "attn_diff", q, k, v, ATT_MAPS // 2, ATT_MAPS,
                           ATT_MAPS // 2, [pl.BlockSpec(memory_space=pltpu.SMEM)], [lam], n_ctx)


CONV_TILE = 256
CONV_HALO = 16
CONV_ROWS = 32


def _conv_kernel(main_ref, left_ref, right_ref, w_ref, b_ref, lng_ref, lnb_ref, o_ref,
                 u_scr, *, n_ctx, t):
    j = pl.program_id(1)
    row0 = j * CONV_TILE
    has_left = jnp.logical_and(row0 != 0, row0 != n_ctx)
    has_right = jnp.logical_and(row0 + CONV_TILE != n_ctx, row0 + CONV_TILE != t)

    def glu(ref):
        x = ref[0].astype(F32)
        return x[:, :CONV_CH] * jax.nn.sigmoid(x[:, CONV_CH:])

    left = jnp.where(has_left, glu(left_ref), 0.0)
    right = jnp.where(has_right, glu(right_ref), 0.0)
    upad = jnp.concatenate([left, glu(main_ref), right], axis=0)
    n_pad = CONV_TILE + 2 * CONV_HALO
    for r in range(8):
        u_scr[r] = upad if r == 0 else pltpu.roll(upad, n_pad - r, 0)

    pad = CONV_WIDTH // 2

    def rows_body(i, carry):
        base = pl.multiple_of(i * CONV_ROWS, CONV_ROWS)
        acc = jnp.zeros((CONV_ROWS, CONV_CH), F32)
        for tap in range(CONV_WIDTH):
            off = tap + CONV_HALO - pad
            a, r = off // 8, off % 8
            acc = acc + w_ref[tap:tap + 1, :] * u_scr[r, pl.ds(base + 8 * a, CONV_ROWS), :]
        y = acc + b_ref[...]
        mu = jnp.mean(y, axis=-1, keepdims=True)
        yc = y - mu
        var = jnp.mean(yc * yc, axis=-1, keepdims=True)
        z = yc * lax.rsqrt(var + EPS) * lng_ref[...] + lnb_ref[...]
        o_ref[0, pl.ds(base, CONV_ROWS), :] = _silu(z).astype(o_ref.dtype)
        return carry

    lax.fori_loop(0, CONV_TILE // CONV_ROWS, rows_body, 0)


def _conv_branch(p_conv, w, bias, ln_g, ln_b, n_ctx):
    b, t, wd = p_conv.shape
    assert n_ctx % CONV_TILE == 0 and t % CONV_TILE == 0
    hb = CONV_TILE // CONV_HALO
    n_halo_blocks = t // CONV_HALO
    vec = pl.BlockSpec((1, CONV_CH), lambda i, j: (0, 0))
    return pl.pallas_call(
        functools.partial(_conv_kernel, n_ctx=n_ctx, t=t),
        grid=(b, t // CONV_TILE),
        in_specs=[pl.BlockSpec((1, CONV_TILE, wd), lambda i, j: (i, j, 0)),
                  pl.BlockSpec((1, CONV_HALO, wd),
                               lambda i, j: (i, jnp.maximum(j * hb - 1, 0), 0)),
                  pl.BlockSpec((1, CONV_HALO, wd),
                               lambda i, j: (i, jnp.minimum((j + 1) * hb, n_halo_blocks - 1), 0)),
                  pl.BlockSpec((CONV_WIDTH, CONV_CH), lambda i, j: (0, 0)),
                  vec, vec, vec],
        out_specs=pl.BlockSpec((1, CONV_TILE, CONV_CH), lambda i, j: (i, j, 0)),
        out_shape=jax.ShapeDtypeStruct((b, t, CONV_CH), BF16),
        scratch_shapes=[pltpu.VMEM((8, CONV_TILE + 2 * CONV_HALO, CONV_CH), F32)],
        compiler_params=_params(("arbitrary", "arbitrary")),
        name="conv",
    )(p_conv, p_conv, p_conv, w, bias.reshape(1, -1), ln_g.reshape(1, -1), ln_b.reshape(1, -1))


HG_TILE = 256
HG_SUB = 16
HG_ROWS = 64
HG_PAIRS = HGRN_HEADS // 2
HG_PAD = 8
HG_W = HGRN_HEADS * HGRN_DK


def _dot3(a_bf16, x):
    x1 = x.astype(BF16)
    r1 = x - x1.astype(F32)
    x2 = r1.astype(BF16)
    x3 = (r1 - x2.astype(F32)).astype(BF16)
    out = jnp.dot(a_bf16, x1, preferred_element_type=F32)
    out = out + jnp.dot(a_bf16, x2, preferred_element_type=F32)
    return out + jnp.dot(a_bf16, x3, preferred_element_type=F32)


def _hgrn_kernel(q_ref, z_ref, v_ref, la_ref, lb0_ref, tri_ref, j_ref, hm_ref, o_ref,
                 st_scr, q_scr, k_scr, c_scr, v_scr, kp_scr, cp_scr, vp_scr,
                 ks_scr, cs_scr, vs_scr, acc_scr, *, rev):
    c_sz = HG_SUB
    n_chunk = HG_TILE // c_sz

    @pl.when(pl.program_id(1) == 0)
    def _():
        st_scr[...] = jnp.zeros_like(st_scr)

    q = _silu(q_ref[0].astype(F32)) * (HGRN_DK ** -0.5)
    z = z_ref[0]
    v = v_ref[0].astype(F32)
    log_sig = jnp.minimum(z, 0.0) - jnp.log(1.0 + jnp.exp(-jnp.abs(z)))
    t1 = la_ref[...]
    t2 = lb0_ref[...] + log_sig
    logf = jnp.maximum(t1, t2) + jnp.log(1.0 + jnp.exp(-jnp.abs(t1 - t2)))
    k = 1.0 - jnp.exp(logf)
    c2 = _dot3(tri_ref[...], logf * LOG2E)
    q_scr[...] = q
    k_scr[...] = k
    c_scr[...] = c2
    v_scr[...] = v
    for ref, val in ((kp_scr, k), (cp_scr, c2), (vp_scr, v)):
        ref[0:HG_PAD, :] = jnp.zeros((HG_PAD, HG_W), F32)
        ref[HG_PAD:HG_PAD + HG_TILE, :] = val
        ref[HG_PAD + HG_TILE:HG_TILE + 2 * HG_PAD, :] = jnp.zeros((HG_PAD, HG_W), F32)

    order = range(n_chunk - 1, -1, -1) if rev else range(n_chunk)
    for ci in order:
        lo = ci * c_sz
        cj = c_scr[lo:lo + c_sz, :]
        c_end = cj[0:1] if rev else cj[c_sz - 1:c_sz]
        qd = (q_scr[lo:lo + c_sz, :] * jnp.exp2(cj)).astype(BF16)
        kd = (k_scr[lo:lo + c_sz, :] * jnp.exp2(c_end - cj)).astype(BF16)
        vb = v_scr[lo:lo + c_sz, :].astype(BF16)
        dec = jnp.exp2(c_end)
        outs = []
        for p in range(HG_PAIRS):
            sl = slice(128 * p, 128 * (p + 1))
            st = st_scr[p]
            outs.append(lax.dot_general(qd[:, sl], st.astype(BF16), (((1,), (1,)), ((), ())),
                                        preferred_element_type=F32))
            upd = lax.dot_general(vb[:, sl], kd[:, sl], (((0,), (0,)), ((), ())),
                                  preferred_element_type=F32)
            st_scr[p] = st * dec[:, sl] + upd * hm_ref[...]
        acc_scr[lo:lo + c_sz, :] = jnp.concatenate(outs, axis=1)

    row_in_chunk = lax.broadcasted_iota(jnp.int32, (HG_TILE, 1), 0) % c_sz

    def shifted(ref, r, mask):
        start = HG_PAD + r if rev else HG_PAD - r
        y = ref[start:start + HG_TILE, :]
        if not mask or r == 0:
            return y
        keep = (row_in_chunk < c_sz - r) if rev else (row_in_chunk >= r)
        return jnp.where(keep, y, 0.0)

    for r in range(8):
        if r == 0:
            k_src, c_src, v_src = k_scr, c_scr, v_scr
        else:
            ks_scr[...] = shifted(kp_scr, r, True)
            cs_scr[...] = shifted(cp_scr, r, True)
            vs_scr[...] = shifted(vp_scr, r, False)
            k_src, c_src, v_src = ks_scr, cs_scr, vs_scr
        half = HG_W // 2
        for blk in range(HG_TILE // HG_ROWS):
            base = blk * HG_ROWS
            for a in range(c_sz // 8):
                n = c_sz - 8 * a
                d_lo = 0 if rev else 8 * a
                s_lo = 8 * a if rev else 0
                starts = [base + ci * c_sz for ci in range(HG_ROWS // c_sz)]

                def rows(ref, lo_in_chunk):
                    if n == c_sz:
                        return ref[base:base + HG_ROWS, :]
                    return jnp.concatenate(
                        [ref[s + lo_in_chunk:s + lo_in_chunk + n, :] for s in starts], axis=0)

                qa = rows(q_scr, d_lo)
                ca = rows(c_scr, d_lo)
                kk = rows(k_src, s_lo)
                cc = rows(c_src, s_lo)
                vv = rows(v_src, s_lo)
                pw = (qa * kk * jnp.exp2(ca - cc)).astype(BF16)
                m = pw.shape[0]
                att2 = jnp.dot(jnp.concatenate([pw[:, :half], pw[:, half:]], axis=0), j_ref[...],
                               preferred_element_type=F32)
                contrib = jnp.concatenate([att2[:m], att2[m:]], axis=1) * vv
                if n == c_sz:
                    acc_scr[base:base + HG_ROWS, :] = acc_scr[base:base + HG_ROWS, :] + contrib
                else:
                    for i, s in enumerate(starts):
                        dst = s + d_lo
                        acc_scr[dst:dst + n, :] = acc_scr[dst:dst + n, :] + contrib[i * n:(i + 1) * n]

    o_ref[0] = acc_scr[...]


def _hgrn_dir(p_hq, p_hf, p_hig, la, lb0, tri, jmat, head_mask, n_ctx, rev):
    b, t, _ = p_hq.shape
    assert n_ctx % HG_TILE == 0 and t % HG_TILE == 0
    n_tiles = t // HG_TILE
    n_ctx_tiles = n_ctx // HG_TILE
    col = 1 if rev else 0

    def tile(j):
        if not rev:
            return j
        return jnp.where(j < n_ctx_tiles, n_ctx_tiles - 1 - j, n_tiles - 1 - (j - n_ctx_tiles))

    const = lambda shape: pl.BlockSpec(shape, lambda i, j: (0,) * len(shape))
    return pl.pallas_call(
        functools.partial(_hgrn_kernel, rev=rev),
        grid=(b, n_tiles),
        in_specs=[pl.BlockSpec((1, HG_TILE, HG_W), lambda i, j: (i, tile(j), 0)),
                  pl.BlockSpec((1, HG_TILE, HG_W), lambda i, j: (i, tile(j), col)),
                  pl.BlockSpec((1, HG_TILE, HG_W), lambda i, j: (i, tile(j), 0)),
                  const((1, HG_W)),
                  const((1, HG_W)),
                  const((HG_TILE, HG_TILE)), const((HG_W // 2, HG_W // 2)), const((128, 128))],
        out_specs=pl.BlockSpec((1, HG_TILE, HG_W), lambda i, j: (i, tile(j), 0)),
        out_shape=jax.ShapeDtypeStruct((b, t, HG_W), F32),
        scratch_shapes=[pltpu.VMEM((HG_PAIRS, 128, 128), F32)]
                       + [pltpu.VMEM((HG_TILE, HG_W), F32) for _ in range(4)]
                       + [pltpu.VMEM((HG_TILE + 2 * HG_PAD, HG_W), F32) for _ in range(3)]
                       + [pltpu.VMEM((HG_TILE, HG_W), F32) for _ in range(4)],
        compiler_params=_params(("arbitrary", "arbitrary")),
        name="hgrn_bwd" if rev else "hgrn_fwd",
    )(p_hq, p_hf, p_hig, la[col:col + 1], lb0[col:col + 1], tri, jmat, head_mask)


def _merge_kernel(og_ref, od_ref, cv_ref, hf_ref, hb_ref, hg_ref, gt_ref, x_ref, wb_ref, wo_ref,
                  sub_ref, hng_ref, j_ref, gc_ref, gx_ref, o_ref, *, tr, n_ctx, lam_init):
    gates = gt_ref[0]

    def gate(i):
        return jax.nn.sigmoid(gates[:, i * 1024:(i + 1) * 1024].astype(F32))

    oa = jnp.concatenate([og_ref[0, i] for i in range(GQA_HEADS // 2)], axis=1)
    merged = gate(0) * jnp.dot(oa, wb_ref[0], preferred_element_type=F32)
    ob = []
    for h in range(DIFF_HEADS):
        od = od_ref[0, h].astype(F32)
        ms = jnp.mean(od * od, axis=-1, keepdims=True)
        ob.append((od * lax.rsqrt(ms + EPS) * sub_ref[...] * (1.0 - lam_init)).astype(BF16))
    merged = merged + gate(1) * jnp.dot(jnp.concatenate(ob, axis=1), wb_ref[1],
                                        preferred_element_type=F32)
    merged = merged + gate(2) * jnp.dot(cv_ref[0], wb_ref[2], preferred_element_type=F32)
    od = hf_ref[0] + hb_ref[0]
    sq = (od * od).astype(BF16)
    half = HG_W // 2
    ms2 = jnp.dot(jnp.concatenate([sq[:, :half], sq[:, half:]], axis=0), j_ref[...],
                  preferred_element_type=F32)
    ms = jnp.concatenate([ms2[:tr], ms2[tr:]], axis=1)
    on = od * lax.rsqrt(ms + EPS) * hng_ref[...] * _silu(hg_ref[0].astype(F32))
    merged = merged + gate(3) * jnp.dot(on.astype(BF16), wb_ref[3], preferred_element_type=F32)

    mix = jnp.dot(merged.astype(BF16), wo_ref[...], preferred_element_type=F32)
    g1 = jnp.where(_row_is_ctx(pl.program_id(1), tr, n_ctx), gc_ref[...], gx_ref[0])
    o_ref[0] = x_ref[0] + g1 * mix


def _merge(o_gqa, o_diff, conv_o, hg_f, hg_b, p_hig, p_gate, x, wb, wo, subln_g, hgrn_g, jmat64,
           g_c, g_x, n_ctx, lam_init):
    b, t, d = x.shape
    tr = _pick(t, 272, 16)
    const = lambda shape: pl.BlockSpec(shape, lambda i, j: (0,) * len(shape))
    return pl.pallas_call(
        functools.partial(_merge_kernel, tr=tr, n_ctx=n_ctx, lam_init=lam_init),
        grid=(b, t // tr),
        in_specs=[pl.BlockSpec((1, GQA_HEADS // 2, tr, 128), lambda i, j: (i, 0, j, 0)),
                  pl.BlockSpec((1, DIFF_HEADS, tr, 128), lambda i, j: (i, 0, j, 0)),
                  pl.BlockSpec((1, tr, CONV_CH), lambda i, j: (i, j, 0)),
                  pl.BlockSpec((1, tr, HG_W), lambda i, j: (i, j, 0)),
                  pl.BlockSpec((1, tr, HG_W), lambda i, j: (i, j, 0)),
                  pl.BlockSpec((1, tr, HG_W), lambda i, j: (i, j, 1)),
                  pl.BlockSpec((1, tr, 4 * d), lambda i, j: (i, j, 0)),
                  pl.BlockSpec((1, tr, d), lambda i, j: (i, j, 0)),
                  const((4, BRANCH_W, d)), const((d, d)),
                  const((1, 128)), const((1, HG_W)), const((HG_W // 2, HG_W // 2)),
                  const((1, d)),
                  pl.BlockSpec((1, 1, d), lambda i, j: (i, 0, 0))],
        out_specs=pl.BlockSpec((1, tr, d), lambda i, j: (i, j, 0)),
        out_shape=jax.ShapeDtypeStruct((b, t, d), F32),
        compiler_params=_params(("arbitrary", "arbitrary")),
        name="merge",
    )(o_gqa, o_diff, conv_o, hg_f, hg_b, p_hig, p_gate, x, wb, wo, subln_g, hgrn_g, jmat64,
      g_c, g_x)


def _swiglu_step(h, w1, w3, w2):
    a = jnp.dot(h, w1, preferred_element_type=F32)
    g = jnp.dot(h, w3, preferred_element_type=F32)
    return jnp.dot((_silu(a) * g).astype(BF16), w2, preferred_element_type=F32)


def _ffn_kernel(h_ref, w1_ref, w3_ref, w2_ref, x_ref, gc_ref, gx_ref, o_ref, acc_scr, *, tr, n_ctx):
    kf = pl.program_id(2)

    @pl.when(kf == 0)
    def _():
        acc_scr[...] = jnp.zeros_like(acc_scr)

    acc_scr[...] += _swiglu_step(h_ref[0], w1_ref[...], w3_ref[...], w2_ref[...])

    @pl.when(kf == pl.num_programs(2) - 1)
    def _():
        g2 = jnp.where(_row_is_ctx(pl.program_id(1), tr, n_ctx), gc_ref[...], gx_ref[0])
        o_ref[0] = x_ref[0] + g2 * acc_scr[...]


def _ffn(h, w1, w3, w2, x, g_c, g_x, n_ctx):
    b, t, d = x.shape
    dff = w1.shape[1]
    tr = _pick(t, 1088, 16)
    tf = _pick(dff, 512, 128)
    return pl.pallas_call(
        functools.partial(_ffn_kernel, tr=tr, n_ctx=n_ctx),
        grid=(b, t // tr, dff // tf),
        in_specs=[pl.BlockSpec((1, tr, d), lambda i, j, f: (i, j, 0)),
                  pl.BlockSpec((d, tf), lambda i, j, f: (0, f)),
                  pl.BlockSpec((d, tf), lambda i, j, f: (0, f)),
                  pl.BlockSpec((tf, d), lambda i, j, f: (f, 0)),
                  pl.BlockSpec((1, tr, d), lambda i, j, f: (i, j, 0)),
                  pl.BlockSpec((1, d), lambda i, j, f: (0, 0)),
                  pl.BlockSpec((1, 1, d), lambda i, j, f: (i, 0, 0))],
        out_specs=pl.BlockSpec((1, tr, d), lambda i, j, f: (i, j, 0)),
        out_shape=jax.ShapeDtypeStruct((b, t, d), F32),
        scratch_shapes=[pltpu.VMEM((tr, d), F32)],
        input_output_aliases={4: 0},
        compiler_params=_params(("arbitrary", "arbitrary", "arbitrary")),
        name="ffn",
    )(h, w1, w3, w2, x, g_c, g_x)


MOE_TILE = 512
GATHER_ROWS = 512
COMBINE_ROWS = 544


def _gather_kernel(idx_ref, src_ref, o_ref, sem, *, rows):
    def copy(r):
        return pltpu.make_async_copy(src_ref.at[idx_ref[0, 0, r]], o_ref.at[r], sem)

    def start(r2, carry):
        copy(2 * r2).start(priority=0)
        copy(2 * r2 + 1).start(priority=1)
        return carry

    def wait(r, carry):
        copy(r).wait()
        return carry

    lax.fori_loop(0, rows // 2, start, 0)
    lax.fori_loop(0, rows, wait, 0)


def _gather_rows(src_rows, idx):
    n_out = idx.shape[0]
    n_src, d = src_rows.shape
    assert d % 128 == 0
    rows = _pick(n_out, GATHER_ROWS, 16)
    steps = n_out // rows
    tiles = pl.pallas_call(
        functools.partial(_gather_kernel, rows=rows),
        grid=(steps,),
        in_specs=[pl.BlockSpec((1, 1, rows), lambda i: (i, 0, 0), memory_space=pltpu.SMEM),
                  pl.BlockSpec(memory_space=pl.ANY)],
        out_specs=pl.BlockSpec((rows, d // 128, 128), lambda i: (i, 0, 0)),
        out_shape=jax.ShapeDtypeStruct((n_out, d // 128, 128), src_rows.dtype),
        scratch_shapes=[pltpu.SemaphoreType.DMA(())],
        compiler_params=_params(("arbitrary",)),
        name="moe_gather",
    )(idx.reshape(steps, 1, rows), src_rows.reshape(n_src, d // 128, 128))
    return tiles.reshape(n_out, d)


def _expert_kernel(te_ref, nu_ref, x_ref, w1_ref, w3_ref, w2_ref, o_ref, acc_scr):
    i = pl.program_id(0)
    kf = pl.program_id(1)
    last = kf == pl.num_programs(1) - 1
    used = i < nu_ref[0]

    @pl.when(jnp.logical_and(used, kf == 0))
    def _():
        acc_scr[...] = jnp.zeros_like(acc_scr)

    @pl.when(used)
    def _():
        acc_scr[...] += _swiglu_step(x_ref[...].astype(BF16), w1_ref[0], w3_ref[0], w2_ref[0])

    @pl.when(jnp.logical_and(used, last))
    def _():
        o_ref[...] = acc_scr[...]

    @pl.when(jnp.logical_and(jnp.logical_not(used), last))
    def _():
        o_ref[...] = jnp.zeros_like(o_ref)


def _expert_ffn(xs, w1, w3, w2, tile_expert, n_used):
    n_sorted, d = xs.shape
    dff = w1.shape[2]
    tf = _pick(dff, 896, 128)
    grid_spec = pltpu.PrefetchScalarGridSpec(
        num_scalar_prefetch=2,
        grid=(n_sorted // MOE_TILE, dff // tf),
        in_specs=[pl.BlockSpec((MOE_TILE, d), lambda i, f, te, nu: (i, 0)),
                  pl.BlockSpec((1, d, tf), lambda i, f, te, nu: (te[i], 0, f)),
                  pl.BlockSpec((1, d, tf), lambda i, f, te, nu: (te[i], 0, f)),
                  pl.BlockSpec((1, tf, d), lambda i, f, te, nu: (te[i], f, 0))],
        out_specs=pl.BlockSpec((MOE_TILE, d), lambda i, f, te, nu: (i, 0)),
        scratch_shapes=[pltpu.VMEM((MOE_TILE, d), F32)])
    return pl.pallas_call(
        _expert_kernel,
        grid_spec=grid_spec,
        out_shape=jax.ShapeDtypeStruct((n_sorted, d), F32),
        compiler_params=_params(("arbitrary", "arbitrary")),
        name="moe_experts",
    )(tile_expert, n_used, xs, w1, w3, w2)


def _combine_kernel(rw_ref, y1_ref, y2_ref, x_ref, gc_ref, gx_ref, o_ref, *, tr, n_ctx):
    rw = rw_ref[0]
    f = rw[:, 2:3] * y1_ref[0, 0] + rw[:, 3:4] * y2_ref[0, 0]
    g2 = jnp.where(_row_is_ctx(pl.program_id(1), tr, n_ctx), gc_ref[...], gx_ref[0])
    o_ref[0] = x_ref[0] + g2 * f


def _combine(route, y_tok, x, g_c, g_x, n_ctx):
    b, t, d = x.shape
    tr = _pick(t, COMBINE_ROWS, 16)
    return pl.pallas_call(
        functools.partial(_combine_kernel, tr=tr, n_ctx=n_ctx),
        grid=(b, t // tr),
        in_specs=[pl.BlockSpec((1, tr, 128), lambda i, j: (i, j, 0)),
                  pl.BlockSpec((1, 1, tr, d), lambda i, j: (0, i, j, 0)),
                  pl.BlockSpec((1, 1, tr, d), lambda i, j: (1, i, j, 0)),
                  pl.BlockSpec((1, tr, d), lambda i, j: (i, j, 0)),
                  pl.BlockSpec((1, d), lambda i, j: (0, 0)),
                  pl.BlockSpec((1, 1, d), lambda i, j: (i, 0, 0))],
        out_specs=pl.BlockSpec((1, tr, d), lambda i, j: (i, j, 0)),
        out_shape=jax.ShapeDtypeStruct((b, t, d), F32),
        input_output_aliases={3: 0},
        compiler_params=_params(("arbitrary", "arbitrary")),
        name="moe_combine",
    )(route, y_tok, y_tok, x, g_c, g_x)


def _moe_plan(e1, e2):
    n = e1.shape[0]
    e = jnp.concatenate([e1, e2])
    onehot = (e[:, None] == jnp.arange(N_EXPERTS, dtype=jnp.int32)[None, :]).astype(jnp.int32)
    counts = jnp.sum(onehot, axis=0)
    rank = jnp.sum((jnp.cumsum(onehot, axis=0) - onehot) * onehot, axis=1)
    padded = ((counts + MOE_TILE - 1) // MOE_TILE) * MOE_TILE
    ends = jnp.cumsum(padded)
    slot = jnp.sum((ends - padded)[None, :] * onehot, axis=1) + rank
    n_tiles = (2 * n + N_EXPERTS * (MOE_TILE - 1)) // MOE_TILE
    token = jnp.tile(jnp.arange(n, dtype=jnp.int32), 2)
    tokens_sorted = jnp.zeros((n_tiles * MOE_TILE,), jnp.int32).at[slot].set(
        token, unique_indices=True)
    tile_start = jnp.arange(n_tiles, dtype=jnp.int32) * MOE_TILE
    tile_expert = jnp.minimum(jnp.sum((ends[None, :] <= tile_start[:, None]).astype(jnp.int32), axis=1),
                              N_EXPERTS - 1)
    n_used = (ends[-1] // MOE_TILE).reshape(1)
    return slot[:n], slot[n:], tokens_sorted, tile_expert, n_used


def _moe(h_f32, route, w1, w3, w2, x, g_c, g_x, n_ctx):
    b, t, d = x.shape
    e1 = route[:, :, 0].reshape(-1).astype(jnp.int32)
    e2 = route[:, :, 1].reshape(-1).astype(jnp.int32)
    pos1, pos2, tokens_sorted, tile_expert, n_used = _moe_plan(e1, e2)
    xs = _gather_rows(h_f32.reshape(b * t, d), tokens_sorted)
    ys = _expert_ffn(xs, w1, w3, w2, tile_expert, n_used)
    y_tok = _gather_rows(ys, jnp.concatenate([pos1, pos2])).reshape(2, b, t, d)
    return _combine(route, y_tok, x, g_c, g_x, n_ctx)


def _router_kernel(h_ref, r_ref, o_ref):
    logits = jnp.dot(h_ref[0], r_ref[...], preferred_element_type=F32,
                     precision=lax.Precision.HIGHEST)
    lane = lax.broadcasted_iota(jnp.int32, logits.shape, 1).astype(F32)
    neg = -jnp.inf
    logits = jnp.where(lane < N_EXPERTS, logits, neg)
    m1 = jnp.max(logits, axis=-1, keepdims=True)
    i1 = jnp.min(jnp.where(logits == m1, lane, 128.0), axis=-1, keepdims=True)
    rest = jnp.where(lane == i1, neg, logits)
    m2 = jnp.max(rest, axis=-1, keepdims=True)
    i2 = jnp.min(jnp.where(rest == m2, lane, 128.0), axis=-1, keepdims=True)
    e2 = jnp.exp(m2 - m1)
    w1 = 1.0 / (1.0 + e2)
    w2 = e2 / (1.0 + e2)
    o_ref[0] = (jnp.where(lane == 0.0, i1, 0.0) + jnp.where(lane == 1.0, i2, 0.0)
                + jnp.where(lane == 2.0, w1, 0.0) + jnp.where(lane == 3.0, w2, 0.0))


def _router(h_f32, router_pad):
    b, t, d = h_f32.shape
    tr = _pick(t, 1088, 16)
    return pl.pallas_call(
        _router_kernel,
        grid=(b, t // tr),
        in_specs=[pl.BlockSpec((1, tr, d), lambda i, j: (i, j, 0)),
                  pl.BlockSpec((d, 128), lambda i, j: (0, 0))],
        out_specs=pl.BlockSpec((1, tr, 128), lambda i, j: (i, j, 0)),
        out_shape=jax.ShapeDtypeStruct((b, t, 128), F32),
        compiler_params=_params(("arbitrary", "arbitrary")),
        name="router",
    )(h_f32, router_pad)


def _rope_tables(seq, n_ctx):
    rows = seq // GRID_W
    row = jnp.repeat(jnp.arange(rows, dtype=F32), GRID_W)
    col = (jnp.arange(seq) % GRID_W).astype(F32)
    axis_dim = HEAD_DIM // 2
    inv_freq = ROPE_THETA ** (-jnp.arange(0, axis_dim, 2, dtype=F32) / axis_dim)
    ang_r = row[:, None] * inv_freq
    ang_c = col[:, None] * inv_freq
    cos = jnp.concatenate([jnp.cos(ang_r), jnp.cos(ang_r), jnp.cos(ang_c), jnp.cos(ang_c)], axis=-1)
    sin = jnp.concatenate([-jnp.sin(ang_r), jnp.sin(ang_r), -jnp.sin(ang_c), jnp.sin(ang_c)], axis=-1)
    cos = jnp.concatenate([jnp.ones((n_ctx, HEAD_DIM), F32), cos], axis=0)
    sin = jnp.concatenate([jnp.zeros((n_ctx, HEAD_DIM), F32), sin], axis=0)
    return jnp.tile(cos, (1, 2)), jnp.tile(sin, (1, 2))


def _block_ones(n, blk, value=1.0, dtype=BF16):
    i = np.arange(n) // blk
    return jnp.asarray((i[:, None] == i[None, :]).astype(np.float32) * value, dtype)


def _chunk_tri(n, chunk, rev):
    i = np.arange(n)
    same = (i[:, None] // chunk) == (i[None, :] // chunk)
    tri = (i[None, :] >= i[:, None]) if rev else (i[None, :] <= i[:, None])
    return jnp.asarray((same & tri).astype(np.float32), BF16)


def kernel(x, c, ctx, c_ctx, ada_w, ada_b, norm1_g, norm2_g, w_in, qk_norm_g, diff_lambda,
           diff_subln_g, conv_w, conv_b, conv_ln_g, conv_ln_b, hgrn_lb_logits, hgrn_norm_g,
           w_branch, w_out, ffn_w1, ffn_w3, ffn_w2, moe_router, moe_w1, moe_w3, moe_w2):
    b, seq, d = x.shape
    n_ctx = ctx.shape[1]
    t = n_ctx + seq
    depth = ada_w.shape[0]

    cos, sin = _rope_tables(seq, n_ctx)
    j_head_mean = _block_ones(128, HEAD_DIM, 1.0 / HEAD_DIM)
    j_hg_sum = _block_ones(HG_W // 2, HGRN_DK)
    j_hg_mean = _block_ones(HG_W // 2, HGRN_DK, 1.0 / HGRN_DK)
    head_mask = _block_ones(128, HGRN_DK, 1.0, F32)
    tri_f = _chunk_tri(HG_TILE, HG_SUB, False)
    tri_b = _chunk_tri(HG_TILE, HG_SUB, True)

    lb = jnp.cumsum(jax.nn.softmax(hgrn_lb_logits.astype(F32), axis=0), axis=0)
    lb = lb - lb[:1]
    log_lb = jnp.log(lb)
    log_1m_lb = jnp.log1p(-lb)

    rows = ((b + 1 + 7) // 8) * 8
    cvec = jnp.zeros((rows, d), F32).at[:b].set(c).at[b].set(c_ctx)
    mod = _ada(cvec, ada_w, ada_b).reshape(depth, rows, 6, d)

    xc = jnp.concatenate([ctx, x], axis=1)
    q_scale = LOG2E * HEAD_DIM ** -0.5

    for l in range(depth):
        mx = mod[l, :b]
        mc = mod[l, b]
        vx = lambda i: mx[:, i:i + 1, :]
        vc = lambda i: mc[i:i + 1, :]
        w_l = w_in[l].astype(BF16)

        h = _normmod(xc, norm1_g[l], vc(0), vc(1), vx(0), vx(1), n_ctx).reshape(b * t, d)
        proj = lambda cols, dt: _matmul(h, w_l[:, cols[0]:cols[1]], dt).reshape(b, t, -1)
        p_attn = proj(COL_ATTN, BF16)
        p_conv = proj(COL_CONV, BF16)
        p_hq = proj(COL_HQ, BF16)
        p_hf = proj(COL_HF, F32)
        p_hig = proj(COL_HIG, BF16)
        p_gate = proj(COL_GATE, BF16)

        g = qk_norm_g[l].astype(F32)
        gains = jnp.tile(jnp.stack([g[0] * q_scale, g[1], g[2] * q_scale, g[3]]), (1, 2))
        lam_init = 0.8 - 0.6 * math.exp(-0.3 * l)
        lp = diff_lambda[l].astype(F32)
        lam = (jnp.exp(jnp.sum(lp[0] * lp[1])) - jnp.exp(jnp.sum(lp[2] * lp[3]))
               + lam_init).reshape(1)
        qg, kg, vg, qd, kd, vd = _attn_prep(p_attn, cos, sin, gains, j_head_mean)
        o_gqa = _gqa_attention(qg, kg, vg, n_ctx)
        o_diff = _diff_attention(lam, qd, kd, vd, n_ctx)

        conv_o = _conv_branch(p_conv, conv_w[l], conv_b[l], conv_ln_g[l], conv_ln_b[l], n_ctx)

        hg_f = _hgrn_dir(p_hq, p_hf, p_hig, log_lb[l], log_1m_lb[l], tri_f, j_hg_sum, head_mask,
                         n_ctx, False)
        hg_b = _hgrn_dir(p_hq, p_hf, p_hig, log_lb[l], log_1m_lb[l], tri_b, j_hg_sum, head_mask,
                         n_ctx, True)

        xc = _merge(o_gqa, o_diff, conv_o, hg_f, hg_b, p_hig, p_gate, xc,
                    w_branch[l].astype(BF16), w_out[l].astype(BF16),
                    diff_subln_g[l].reshape(1, -1), jnp.tile(hgrn_norm_g[l], HGRN_HEADS).reshape(1, -1),
                    j_hg_mean, vc(2), vx(2), n_ctx, lam_init)

        if l % 2 == 0:
            h2 = _normmod(xc, norm2_g[l], vc(3), vc(4), vx(3), vx(4), n_ctx)
            i = l // 2
            xc = _ffn(h2, ffn_w1[i].astype(BF16), ffn_w3[i].astype(BF16), ffn_w2[i].astype(BF16),
                      xc, vc(5), vx(5), n_ctx)
        else:
            h2f = _normmod(xc, norm2_g[l], vc(3), vc(4), vx(3), vx(4), n_ctx, out_dtype=F32)
            i = l // 2
            router_pad = jnp.zeros((d, 128), F32).at[:, :N_EXPERTS].set(moe_router[i])
            route = _router(h2f, router_pad)
            xc = _moe(h2f, route, moe_w1[i].astype(BF16), moe_w3[i].astype(BF16),
                      moe_w2[i].astype(BF16), xc, vc(5), vx(5), n_ctx)
    return xc[:, n_ctx:]
```

```python
import functools
import math

import jax
import jax.numpy as jnp
import numpy as np
from jax import lax
from jax.experimental import pallas as pl
from jax.experimental.pallas import tpu as pltpu

F32 = jnp.float32
BF16 = jnp.bfloat16

GRID_W = 64
HEAD_DIM = 64
ROPE_THETA = 10000.0
EPS = 1e-6
GQA_HEADS = 8
GQA_KV_HEADS = 2
DIFF_HEADS = 4
CONV_CH = 512
CONV_WIDTH = 31
HGRN_HEADS = 8
HGRN_DK = 64
HGRN_CHUNK = 64
BRANCH_W = 512
N_EXPERTS = 8
LOG2E = 1.4426950408889634

COL_ATTN = (0, 2304)
COL_CONV = (2304, 3328)
COL_HQ = (3328, 3840)
COL_HF = (3840, 4864)
COL_HIG = (4864, 5888)
COL_GATE = (5888, 9984)

VMEM_LIMIT_BYTES = 50 * 1024 * 1024


def _params(sem):
    return pltpu.CompilerParams(dimension_semantics=sem, vmem_limit_bytes=VMEM_LIMIT_BYTES)


def _pick(n, target, mult):
    best = None
    for d in range(mult, min(n, target) + 1, mult):
        if n % d == 0:
            best = d
    assert best is not None, (n, target, mult)
    return best


def _silu(x):
    return x * jax.nn.sigmoid(x)


def _row_is_ctx(tile_idx, tr, n_ctx):
    row = tile_idx * tr + lax.broadcasted_iota(jnp.int32, (tr, 1), 0)
    return row < n_ctx


def _ada_kernel(c_ref, w_ref, b_ref, o_ref):
    s = _silu(c_ref[...])
    o_ref[0] = jnp.dot(s.astype(BF16), w_ref[0].astype(BF16),
                       preferred_element_type=F32) + b_ref[0]


def _ada(cvec, ada_w, ada_b):
    depth, d, n = ada_w.shape
    rows = cvec.shape[0]
    tn = _pick(n, 1536, 128)
    return pl.pallas_call(
        _ada_kernel,
        grid=(depth, n // tn),
        in_specs=[pl.BlockSpec((rows, d), lambda l, j: (0, 0)),
                  pl.BlockSpec((1, d, tn), lambda l, j: (l, 0, j)),
                  pl.BlockSpec((1, 1, tn), lambda l, j: (l, 0, j))],
        out_specs=pl.BlockSpec((1, rows, tn), lambda l, j: (l, 0, j)),
        out_shape=jax.ShapeDtypeStruct((depth, rows, n), F32),
        compiler_params=_params(("arbitrary", "arbitrary")),
        name="ada",
    )(cvec, ada_w, ada_b.reshape(depth, 1, n))


def _normmod_kernel(x_ref, g_ref, shc_ref, scc_ref, shx_ref, scx_ref, o_ref, *, tr, n_ctx):
    x = x_ref[0]
    ms = jnp.mean(x * x, axis=-1, keepdims=True)
    y = x * lax.rsqrt(ms + EPS) * g_ref[...]
    is_ctx = _row_is_ctx(pl.program_id(1), tr, n_ctx)
    sh = jnp.where(is_ctx, shc_ref[...], shx_ref[0])
    sc = jnp.where(is_ctx, scc_ref[...], scx_ref[0])
    o_ref[0] = (y * (1.0 + sc) + sh).astype(o_ref.dtype)


def _normmod(x, g, sh_c, sc_c, sh_x, sc_x, n_ctx, out_dtype=BF16):
    b, t, d = x.shape
    tr = _pick(t, 1088, 16)
    vec_c = pl.BlockSpec((1, d), lambda i, j: (0, 0))
    vec_x = pl.BlockSpec((1, 1, d), lambda i, j: (i, 0, 0))
    return pl.pallas_call(
        functools.partial(_normmod_kernel, tr=tr, n_ctx=n_ctx),
        grid=(b, t // tr),
        in_specs=[pl.BlockSpec((1, tr, d), lambda i, j: (i, j, 0)),
                  vec_c, vec_c, vec_c, vec_x, vec_x],
        out_specs=pl.BlockSpec((1, tr, d), lambda i, j: (i, j, 0)),
        out_shape=jax.ShapeDtypeStruct((b, t, d), out_dtype),
        compiler_params=_params(("arbitrary", "arbitrary")),
        name="normmod",
    )(x, g.reshape(1, d), sh_c, sc_c, sh_x, sc_x)


def _mm_kernel(a_ref, w_ref, o_ref):
    o_ref[...] = jnp.dot(a_ref[...], w_ref[...],
                         preferred_element_type=F32).astype(o_ref.dtype)


def _matmul(a, w, out_dtype):
    m, k = a.shape
    n = w.shape[1]
    tm = _pick(m, 2048, 256)
    tn = _pick(n, 1024, 128)
    return pl.pallas_call(
        _mm_kernel,
        grid=(m // tm, n // tn),
        in_specs=[pl.BlockSpec((tm, k), lambda i, j: (i, 0)),
                  pl.BlockSpec((k, tn), lambda i, j: (0, j))],
        out_specs=pl.BlockSpec((tm, tn), lambda i, j: (i, j)),
        out_shape=jax.ShapeDtypeStruct((m, n), out_dtype),
        compiler_params=_params(("arbitrary", "arbitrary")),
        name="proj",
    )(a, w)


def _prep_kernel(p_ref, cos_ref, sin_ref, g_ref, j_ref, qg_ref, kg_ref, vg_ref,
                 qd_ref, kd_ref, vd_ref):
    cos = cos_ref[...]
    sin = sin_ref[...]
    lane = lax.broadcasted_iota(jnp.int32, (1, 128), 1)
    first_half = (lane % 32) < 16
    low_head = lane < HEAD_DIM

    def norm_rope(col, gi):
        x = p_ref[0, :, col:col + 128].astype(F32)
        ms = jnp.dot((x * x).astype(BF16), j_ref[...], preferred_element_type=F32)
        xn = x * lax.rsqrt(ms + EPS) * g_ref[gi:gi + 1, :]
        swapped = jnp.where(first_half, pltpu.roll(xn, 112, 1), pltpu.roll(xn, 16, 1))
        return xn * cos + swapped * sin

    def put_heads(ref, first, y):
        ref[0, first] = y[:, :HEAD_DIM].astype(ref.dtype)
        ref[0, first + 1] = y[:, HEAD_DIM:].astype(ref.dtype)

    def put_heads_t(ref, first, y):
        yt = y.T
        ref[0, first] = yt[:HEAD_DIM, :].astype(ref.dtype)
        ref[0, first + 1] = yt[HEAD_DIM:, :].astype(ref.dtype)

    tr = cos.shape[0]
    for g in range(4):
        put_heads_t(qg_ref, 2 * g, norm_rope(128 * g, 0))
    put_heads(kg_ref, 0, norm_rope(512, 1))
    for g in range(4):
        put_heads_t(qd_ref, 2 * g, norm_rope(768 + 128 * g, 2))
    for g in range(4):
        put_heads(kd_ref, 2 * g, norm_rope(1280 + 128 * g, 3))
    avt = p_ref[0, :, 640:768].astype(F32).T
    def ones_first_row(n):
        row = lax.broadcasted_iota(jnp.int32, (n, tr), 0)
        return jnp.where(row == 0, 1.0, 0.0).astype(vg_ref.dtype)

    for h in range(GQA_KV_HEADS):
        vg_ref[0, h, 0:HEAD_DIM, :] = avt[h * HEAD_DIM:(h + 1) * HEAD_DIM, :].astype(vg_ref.dtype)
        vg_ref[0, h, HEAD_DIM:2 * HEAD_DIM, :] = ones_first_row(HEAD_DIM)
    for h in range(DIFF_HEADS):
        bvt = p_ref[0, :, 1792 + 128 * h:1792 + 128 * (h + 1)].astype(F32).T
        vd_ref[0, h, 0:128, :] = bvt.astype(vd_ref.dtype)
        vd_ref[0, h, 128:128 + ATT_VPAD, :] = ones_first_row(ATT_VPAD)


def _attn_prep(p_attn, cos, sin, gains, jmat):
    b, t, w = p_attn.shape
    tr = _pick(t, 256, 128)
    heads = lambda n, width: pl.BlockSpec((1, n, tr, width), lambda i, j: (i, 0, j, 0))
    shape = lambda n, width: jax.ShapeDtypeStruct((b, n, t, width), BF16)
    heads_t = lambda n, rows: pl.BlockSpec((1, n, rows, tr), lambda i, j: (i, 0, 0, j))
    shape_t = lambda n, rows: jax.ShapeDtypeStruct((b, n, rows, t), BF16)
    return pl.pallas_call(
        _prep_kernel,
        grid=(b, t // tr),
        in_specs=[pl.BlockSpec((1, tr, w), lambda i, j: (i, j, 0)),
                  pl.BlockSpec((tr, 128), lambda i, j: (j, 0)),
                  pl.BlockSpec((tr, 128), lambda i, j: (j, 0)),
                  pl.BlockSpec((4, 128), lambda i, j: (0, 0)),
                  pl.BlockSpec((128, 128), lambda i, j: (0, 0))],
        out_specs=[heads_t(GQA_HEADS, HEAD_DIM), heads(GQA_KV_HEADS, HEAD_DIM),
                   heads_t(GQA_KV_HEADS, 128),
                   heads_t(2 * DIFF_HEADS, HEAD_DIM), heads(2 * DIFF_HEADS, HEAD_DIM),
                   heads_t(DIFF_HEADS, 128 + ATT_VPAD)],
        out_shape=[shape_t(GQA_HEADS, HEAD_DIM), shape(GQA_KV_HEADS, HEAD_DIM),
                   shape_t(GQA_KV_HEADS, 128),
                   shape_t(2 * DIFF_HEADS, HEAD_DIM), shape(2 * DIFF_HEADS, HEAD_DIM),
                   shape_t(DIFF_HEADS, 128 + ATT_VPAD)],
        compiler_params=_params(("arbitrary", "arbitrary")),
        name="attn_prep",
    )(p_attn, cos, sin, gains, jmat)


ATT_TQ = 256
ATT_TK = 256
ATT_VPAD = 16


def _scores(q_ref, k_ref, k_idx, s_scr, m_scr, slot, nk):
    tq, tk = ATT_TQ, ATT_TK
    qt = q_ref[0, slot]
    m_part = jnp.full((8, tq), -jnp.inf, F32)
    for c in range(nk // tk):
        s = jnp.dot(k_ref[0, k_idx, c * tk:(c + 1) * tk, :], qt, preferred_element_type=F32)
        s_scr[slot, c * tk:(c + 1) * tk, :] = s
        for u in range(tk // 8):
            m_part = jnp.maximum(m_part, s[8 * u:8 * (u + 1), :])
    m_scr[slot] = jnp.broadcast_to(jnp.max(m_part, axis=0, keepdims=True), (8, tq))


def _softmax_pv(v_ref, v_idx, s_scr, m_scr, slot, nk):
    tq, tk = ATT_TQ, ATT_TK
    m = m_scr[slot, 0:1, :]
    acc = jnp.zeros((v_ref.shape[2], tq), F32)
    for c in range(nk // tk):
        p = jnp.exp2((s_scr[slot, c * tk:(c + 1) * tk, :] - m).astype(BF16))
        acc = acc + jnp.dot(v_ref[0, v_idx, :, c * tk:(c + 1) * tk], p,
                            preferred_element_type=F32)
    return acc


def _for_each(n, fn):
    def body(i, carry):
        fn(i)
        return carry
    lax.fori_loop(0, n, body, 0)


def _ctx_or_full(n_ctx, t, fn):
    is_ctx_tile = pl.program_id(2) * ATT_TQ < n_ctx

    @pl.when(is_ctx_tile)
    def _():
        fn(n_ctx)

    @pl.when(jnp.logical_not(is_ctx_tile))
    def _():
        fn(t)


ATT_MAPS = 4


def _gqa_kernel(q_ref, k_ref, v_ref, o_ref, s_scr, m_scr, *, n_ctx, t):
    def attend(nk):
        def scores(i):
            for j in range(2):
                _scores(q_ref, k_ref, 0, s_scr, m_scr, 2 * i + j, nk)
        _for_each(ATT_MAPS // 2, scores)

        def finish(i):
            halves = []
            for j in range(2):
                acc = _softmax_pv(v_ref, 0, s_scr, m_scr, 2 * i + j, nk)
                halves.append(acc[0:HEAD_DIM, :] / acc[HEAD_DIM:HEAD_DIM + 1, :])
            o_ref[0, i] = jnp.concatenate(halves, axis=0).T.astype(o_ref.dtype)
        _for_each(1, lambda i: [finish(j) for j in range(ATT_MAPS // 2)])
    _ctx_or_full(n_ctx, t, attend)


def _diff_kernel(lam_ref, q_ref, k_ref, v_ref, o_ref, s_scr, m_scr, *, n_ctx, t):
    def attend(nk):
        def scores(h):
            for j in range(2):
                _scores(q_ref, k_ref, 2 * h + j, s_scr, m_scr, 2 * h + j, nk)
        _for_each(ATT_MAPS // 2, scores)

        def finish(h):
            a1 = _softmax_pv(v_ref, h, s_scr, m_scr, 2 * h, nk)
            a2 = _softmax_pv(v_ref, h, s_scr, m_scr, 2 * h + 1, nk)
            ot = a1[0:128, :] / a1[128:129, :] - lam_ref[0] * (a2[0:128, :] / a2[128:129, :])
            o_ref[0, h] = ot.T.astype(o_ref.dtype)
        _for_each(1, lambda i: [finish(j) for j in range(ATT_MAPS // 2)])
    _ctx_or_full(n_ctx, t, attend)


def _attention_call(body, name, q, k, v, n_out, k_block, v_block, extra_in, extra_args, n_ctx):
    b, n_q, _, t = q.shape
    assert n_ctx % ATT_TQ == 0 and t % ATT_TQ == 0 and n_ctx % ATT_TK == 0 and t % ATT_TK == 0
    v_rows = v.shape[2]
    return pl.pallas_call(
        functools.partial(body, n_ctx=n_ctx, t=t),
        grid=(b, n_q // ATT_MAPS, t // ATT_TQ),
        in_specs=extra_in + [
            pl.BlockSpec((1, ATT_MAPS, HEAD_DIM, ATT_TQ), lambda i, u, j: (i, u, 0, j)),
            pl.BlockSpec((1, k_block, t, HEAD_DIM), lambda i, u, j: (i, u, 0, 0)),
            pl.BlockSpec((1, v_block, v_rows, t), lambda i, u, j: (i, u, 0, 0))],
        out_specs=pl.BlockSpec((1, n_out, ATT_TQ, 128), lambda i, u, j: (i, u, j, 0)),
        out_shape=jax.ShapeDtypeStruct((b, n_q // ATT_MAPS * n_out, t, 128), BF16),
        scratch_shapes=[pltpu.VMEM((ATT_MAPS, t, ATT_TQ), F32),
                        pltpu.VMEM((ATT_MAPS, 8, ATT_TQ), F32)],
        compiler_params=_params(("arbitrary", "arbitrary", "arbitrary")),
        name=name,
    )(*extra_args, q, k, v)


def _gqa_attention(q, k, v, n_ctx):
    assert GQA_HEADS // GQA_KV_HEADS == ATT_MAPS
    return _attention_call(_gqa_kernel, "attn_gqa", q, k, v, ATT_MAPS // 2, 1, 1, [], [], n_ctx)


def _diff_attention(lam, q, k, v, n_ctx):
    return _attention_call(_diff_kernel, "attn_diff", q, k, v, ATT_MAPS // 2, ATT_MAPS,
                           ATT_MAPS // 2, [pl.BlockSpec(memory_space=pltpu.SMEM)], [lam], n_ctx)


CONV_TILE = 256
CONV_HALO = 16
CONV_ROWS = 32


def _conv_kernel(main_ref, left_ref, right_ref, w_ref, b_ref, lng_ref, lnb_ref, o_ref,
                 u_scr, *, n_ctx, t):
    j = pl.program_id(1)
    row0 = j * CONV_TILE
    has_left = jnp.logical_and(row0 != 0, row0 != n_ctx)
    has_right = jnp.logical_and(row0 + CONV_TILE != n_ctx, row0 + CONV_TILE != t)

    def glu(ref):
        x = ref[0].astype(F32)
        return x[:, :CONV_CH] * jax.nn.sigmoid(x[:, CONV_CH:])

    left = jnp.where(has_left, glu(left_ref), 0.0)
    right = jnp.where(has_right, glu(right_ref), 0.0)
    upad = jnp.concatenate([left, glu(main_ref), right], axis=0)
    n_pad = CONV_TILE + 2 * CONV_HALO
    for r in range(8):
        u_scr[r] = upad if r == 0 else pltpu.roll(upad, n_pad - r, 0)

    pad = CONV_WIDTH // 2

    def rows_body(i, carry):
        base = pl.multiple_of(i * CONV_ROWS, CONV_ROWS)
        acc = jnp.zeros((CONV_ROWS, CONV_CH), F32)
        for tap in range(CONV_WIDTH):
            off = tap + CONV_HALO - pad
            a, r = off // 8, off % 8
            acc = acc + w_ref[tap:tap + 1, :] * u_scr[r, pl.ds(base + 8 * a, CONV_ROWS), :]
        y = acc + b_ref[...]
        mu = jnp.mean(y, axis=-1, keepdims=True)
        yc = y - mu
        var = jnp.mean(yc * yc, axis=-1, keepdims=True)
        z = yc * lax.rsqrt(var + EPS) * lng_ref[...] + lnb_ref[...]
        o_ref[0, pl.ds(base, CONV_ROWS), :] = _silu(z).astype(o_ref.dtype)
        return carry

    lax.fori_loop(0, CONV_TILE // CONV_ROWS, rows_body, 0)


def _conv_branch(p_conv, w, bias, ln_g, ln_b, n_ctx):
    b, t, wd = p_conv.shape
    assert n_ctx % CONV_TILE == 0 and t % CONV_TILE == 0
    hb = CONV_TILE // CONV_HALO
    n_halo_blocks = t // CONV_HALO
    vec = pl.BlockSpec((1, CONV_CH), lambda i, j: (0, 0))
    return pl.pallas_call(
        functools.partial(_conv_kernel, n_ctx=n_ctx, t=t),
        grid=(b, t // CONV_TILE),
        in_specs=[pl.BlockSpec((1, CONV_TILE, wd), lambda i, j: (i, j, 0)),
                  pl.BlockSpec((1, CONV_HALO, wd),
                               lambda i, j: (i, jnp.maximum(j * hb - 1, 0), 0)),
                  pl.BlockSpec((1, CONV_HALO, wd),
                               lambda i, j: (i, jnp.minimum((j + 1) * hb, n_halo_blocks - 1), 0)),
                  pl.BlockSpec((CONV_WIDTH, CONV_CH), lambda i, j: (0, 0)),
                  vec, vec, vec],
        out_specs=pl.BlockSpec((1, CONV_TILE, CONV_CH), lambda i, j: (i, j, 0)),
        out_shape=jax.ShapeDtypeStruct((b, t, CONV_CH), BF16),
        scratch_shapes=[pltpu.VMEM((8, CONV_TILE + 2 * CONV_HALO, CONV_CH), F32)],
        compiler_params=_params(("arbitrary", "arbitrary")),
        name="conv",
    )(p_conv, p_conv, p_conv, w, bias.reshape(1, -1), ln_g.reshape(1, -1), ln_b.reshape(1, -1))


HG_TILE = 256
HG_SUB = 16
HG_ROWS = 64
HG_PAIRS = HGRN_HEADS // 2
HG_PAD = 8
HG_W = HGRN_HEADS * HGRN_DK


def _dot3(a_bf16, x):
    x1 = x.astype(BF16)
    r1 = x - x1.astype(F32)
    x2 = r1.astype(BF16)
    x3 = (r1 - x2.astype(F32)).astype(BF16)
    out = jnp.dot(a_bf16, x1, preferred_element_type=F32)
    out = out + jnp.dot(a_bf16, x2, preferred_element_type=F32)
    return out + jnp.dot(a_bf16, x3, preferred_element_type=F32)


def _hgrn_kernel(q_ref, z_ref, v_ref, la_ref, lb0_ref, tri_ref, j_ref, hm_ref, o_ref,
                 st_scr, q_scr, k_scr, c_scr, v_scr, kp_scr, cp_scr, vp_scr,
                 ks_scr, cs_scr, vs_scr, acc_scr, *, rev):
    c_sz = HG_SUB
    n_chunk = HG_TILE // c_sz

    @pl.when(pl.program_id(1) == 0)
    def _():
        st_scr[...] = jnp.zeros_like(st_scr)

    q = _silu(q_ref[0].astype(F32)) * (HGRN_DK ** -0.5)
    z = z_ref[0]
    v = v_ref[0].astype(F32)
    log_sig = jnp.minimum(z, 0.0) - jnp.log(1.0 + jnp.exp(-jnp.abs(z)))
    t1 = la_ref[...]
    t2 = lb0_ref[...] + log_sig
    logf = jnp.maximum(t1, t2) + jnp.log(1.0 + jnp.exp(-jnp.abs(t1 - t2)))
    k = 1.0 - jnp.exp(logf)
    c2 = _dot3(tri_ref[...], logf * LOG2E)
    q_scr[...] = q
    k_scr[...] = k
    c_scr[...] = c2
    v_scr[...] = v
    for ref, val in ((kp_scr, k), (cp_scr, c2), (vp_scr, v)):
        ref[0:HG_PAD, :] = jnp.zeros((HG_PAD, HG_W), F32)
        ref[HG_PAD:HG_PAD + HG_TILE, :] = val
        ref[HG_PAD + HG_TILE:HG_TILE + 2 * HG_PAD, :] = jnp.zeros((HG_PAD, HG_W), F32)

    order = range(n_chunk - 1, -1, -1) if rev else range(n_chunk)
    for ci in order:
        lo = ci * c_sz
        cj = c_scr[lo:lo + c_sz, :]
        c_end = cj[0:1] if rev else cj[c_sz - 1:c_sz]
        qd = (q_scr[lo:lo + c_sz, :] * jnp.exp2(cj)).astype(BF16)
        kd = (k_scr[lo:lo + c_sz, :] * jnp.exp2(c_end - cj)).astype(BF16)
        vb = v_scr[lo:lo + c_sz, :].astype(BF16)
        dec = jnp.exp2(c_end)
        outs = []
        for p in range(HG_PAIRS):
            sl = slice(128 * p, 128 * (p + 1))
            st = st_scr[p]
            outs.append(lax.dot_general(qd[:, sl], st.astype(BF16), (((1,), (1,)), ((), ())),
                                        preferred_element_type=F32))
            upd = lax.dot_general(vb[:, sl], kd[:, sl], (((0,), (0,)), ((), ())),
                                  preferred_element_type=F32)
            st_scr[p] = st * dec[:, sl] + upd * hm_ref[...]
        acc_scr[lo:lo + c_sz, :] = jnp.concatenate(outs, axis=1)

    row_in_chunk = lax.broadcasted_iota(jnp.int32, (HG_TILE, 1), 0) % c_sz

    def shifted(ref, r, mask):
        start = HG_PAD + r if rev else HG_PAD - r
        y = ref[start:start + HG_TILE, :]
        if not mask or r == 0:
            return y
        keep = (row_in_chunk < c_sz - r) if rev else (row_in_chunk >= r)
        return jnp.where(keep, y, 0.0)

    for r in range(8):
        if r == 0:
            k_src, c_src, v_src = k_scr, c_scr, v_scr
        else:
            ks_scr[...] = shifted(kp_scr, r, True)
            cs_scr[...] = shifted(cp_scr, r, True)
            vs_scr[...] = shifted(vp_scr, r, False)
            k_src, c_src, v_src = ks_scr, cs_scr, vs_scr
        half = HG_W // 2
        for blk in range(HG_TILE // HG_ROWS):
            base = blk * HG_ROWS
            for a in range(c_sz // 8):
                n = c_sz - 8 * a
                d_lo = 0 if rev else 8 * a
                s_lo = 8 * a if rev else 0
                starts = [base + ci * c_sz for ci in range(HG_ROWS // c_sz)]

                def rows(ref, lo_in_chunk):
                    if n == c_sz:
                        return ref[base:base + HG_ROWS, :]
                    return jnp.concatenate(
                        [ref[s + lo_in_chunk:s + lo_in_chunk + n, :] for s in starts], axis=0)

                qa = rows(q_scr, d_lo)
                ca = rows(c_scr, d_lo)
                kk = rows(k_src, s_lo)
                cc = rows(c_src, s_lo)
                vv = rows(v_src, s_lo)
                pw = (qa * kk * jnp.exp2(ca - cc)).astype(BF16)
                m = pw.shape[0]
                att2 = jnp.dot(jnp.concatenate([pw[:, :half], pw[:, half:]], axis=0), j_ref[...],
                               preferred_element_type=F32)
                contrib = jnp.concatenate([att2[:m], att2[m:]], axis=1) * vv
                if n == c_sz:
                    acc_scr[base:base + HG_ROWS, :] = acc_scr[base:base + HG_ROWS, :] + contrib
                else:
                    for i, s in enumerate(starts):
                        dst = s + d_lo
                        acc_scr[dst:dst + n, :] = acc_scr[dst:dst + n, :] + contrib[i * n:(i + 1) * n]

    o_ref[0] = acc_scr[...]


def _hgrn_dir(p_hq, p_hf, p_hig, la, lb0, tri, jmat, head_mask, n_ctx, rev):
    b, t, _ = p_hq.shape
    assert n_ctx % HG_TILE == 0 and t % HG_TILE == 0
    n_tiles = t // HG_TILE
    n_ctx_tiles = n_ctx // HG_TILE
    col = 1 if rev else 0

    def tile(j):
        if not rev:
            return j
        return jnp.where(j < n_ctx_tiles, n_ctx_tiles - 1 - j, n_tiles - 1 - (j - n_ctx_tiles))

    const = lambda shape: pl.BlockSpec(shape, lambda i, j: (0,) * len(shape))
    return pl.pallas_call(
        functools.partial(_hgrn_kernel, rev=rev),
        grid=(b, n_tiles),
        in_specs=[pl.BlockSpec((1, HG_TILE, HG_W), lambda i, j: (i, tile(j), 0)),
                  pl.BlockSpec((1, HG_TILE, HG_W), lambda i, j: (i, tile(j), col)),
                  pl.BlockSpec((1, HG_TILE, HG_W), lambda i, j: (i, tile(j), 0)),
                  const((1, HG_W)),
                  const((1, HG_W)),
                  const((HG_TILE, HG_TILE)), const((HG_W // 2, HG_W // 2)), const((128, 128))],
        out_specs=pl.BlockSpec((1, HG_TILE, HG_W), lambda i, j: (i, tile(j), 0)),
        out_shape=jax.ShapeDtypeStruct((b, t, HG_W), F32),
        scratch_shapes=[pltpu.VMEM((HG_PAIRS, 128, 128), F32)]
                       + [pltpu.VMEM((HG_TILE, HG_W), F32) for _ in range(4)]
                       + [pltpu.VMEM((HG_TILE + 2 * HG_PAD, HG_W), F32) for _ in range(3)]
                       + [pltpu.VMEM((HG_TILE, HG_W), F32) for _ in range(4)],
        compiler_params=_params(("arbitrary", "arbitrary")),
        name="hgrn_bwd" if rev else "hgrn_fwd",
    )(p_hq, p_hf, p_hig, la[col:col + 1], lb0[col:col + 1], tri, jmat, head_mask)


def _merge_kernel(og_ref, od_ref, cv_ref, hf_ref, hb_ref, hg_ref, gt_ref, x_ref, wb_ref, wo_ref,
                  sub_ref, hng_ref, j_ref, gc_ref, gx_ref, o_ref, *, tr, n_ctx, lam_init):
    gates = gt_ref[0]

    def gate(i):
        return jax.nn.sigmoid(gates[:, i * 1024:(i + 1) * 1024].astype(F32))

    oa = jnp.concatenate([og_ref[0, i] for i in range(GQA_HEADS // 2)], axis=1)
    merged = gate(0) * jnp.dot(oa, wb_ref[0], preferred_element_type=F32)
    ob = []
    for h in range(DIFF_HEADS):
        od = od_ref[0, h].astype(F32)
        ms = jnp.mean(od * od, axis=-1, keepdims=True)
        ob.append((od * lax.rsqrt(ms + EPS) * sub_ref[...] * (1.0 - lam_init)).astype(BF16))
    merged = merged + gate(1) * jnp.dot(jnp.concatenate(ob, axis=1), wb_ref[1],
                                        preferred_element_type=F32)
    merged = merged + gate(2) * jnp.dot(cv_ref[0], wb_ref[2], preferred_element_type=F32)
    od = hf_ref[0] + hb_ref[0]
    sq = (od * od).astype(BF16)
    half = HG_W // 2
    ms2 = jnp.dot(jnp.concatenate([sq[:, :half], sq[:, half:]], axis=0), j_ref[...],
                  preferred_element_type=F32)
    ms = jnp.concatenate([ms2[:tr], ms2[tr:]], axis=1)
    on = od * lax.rsqrt(ms + EPS) * hng_ref[...] * _silu(hg_ref[0].astype(F32))
    merged = merged + gate(3) * jnp.dot(on.astype(BF16), wb_ref[3], preferred_element_type=F32)

    mix = jnp.dot(merged.astype(BF16), wo_ref[...], preferred_element_type=F32)
    g1 = jnp.where(_row_is_ctx(pl.program_id(1), tr, n_ctx), gc_ref[...], gx_ref[0])
    o_ref[0] = x_ref[0] + g1 * mix


def _merge(o_gqa, o_diff, conv_o, hg_f, hg_b, p_hig, p_gate, x, wb, wo, subln_g, hgrn_g, jmat64,
           g_c, g_x, n_ctx, lam_init):
    b, t, d = x.shape
    tr = _pick(t, 272, 16)
    const = lambda shape: pl.BlockSpec(shape, lambda i, j: (0,) * len(shape))
    return pl.pallas_call(
        functools.partial(_merge_kernel, tr=tr, n_ctx=n_ctx, lam_init=lam_init),
        grid=(b, t // tr),
        in_specs=[pl.BlockSpec((1, GQA_HEADS // 2, tr, 128), lambda i, j: (i, 0, j, 0)),
                  pl.BlockSpec((1, DIFF_HEADS, tr, 128), lambda i, j: (i, 0, j, 0)),
                  pl.BlockSpec((1, tr, CONV_CH), lambda i, j: (i, j, 0)),
                  pl.BlockSpec((1, tr, HG_W), lambda i, j: (i, j, 0)),
                  pl.BlockSpec((1, tr, HG_W), lambda i, j: (i, j, 0)),
                  pl.BlockSpec((1, tr, HG_W), lambda i, j: (i, j, 1)),
                  pl.BlockSpec((1, tr, 4 * d), lambda i, j: (i, j, 0)),
                  pl.BlockSpec((1, tr, d), lambda i, j: (i, j, 0)),
                  const((4, BRANCH_W, d)), const((d, d)),
                  const((1, 128)), const((1, HG_W)), const((HG_W // 2, HG_W // 2)),
                  const((1, d)),
                  pl.BlockSpec((1, 1, d), lambda i, j: (i, 0, 0))],
        out_specs=pl.BlockSpec((1, tr, d), lambda i, j: (i, j, 0)),
        out_shape=jax.ShapeDtypeStruct((b, t, d), F32),
        compiler_params=_params(("arbitrary", "arbitrary")),
        name="merge",
    )(o_gqa, o_diff, conv_o, hg_f, hg_b, p_hig, p_gate, x, wb, wo, subln_g, hgrn_g, jmat64,
      g_c, g_x)


def _swiglu_step(h, w1, w3, w2):
    a = jnp.dot(h, w1, preferred_element_type=F32)
    g = jnp.dot(h, w3, preferred_element_type=F32)
    return jnp.dot((_silu(a) * g).astype(BF16), w2, preferred_element_type=F32)


def _ffn_kernel(h_ref, w1_ref, w3_ref, w2_ref, x_ref, gc_ref, gx_ref, o_ref, acc_scr, *, tr, n_ctx):
    kf = pl.program_id(2)

    @pl.when(kf == 0)
    def _():
        acc_scr[...] = jnp.zeros_like(acc_scr)

    acc_scr[...] += _swiglu_step(h_ref[0], w1_ref[...], w3_ref[...], w2_ref[...])

    @pl.when(kf == pl.num_programs(2) - 1)
    def _():
        g2 = jnp.where(_row_is_ctx(pl.program_id(1), tr, n_ctx), gc_ref[...], gx_ref[0])
        o_ref[0] = x_ref[0] + g2 * acc_scr[...]


def _ffn(h, w1, w3, w2, x, g_c, g_x, n_ctx):
    b, t, d = x.shape
    dff = w1.shape[1]
    tr = _pick(t, 1088, 16)
    tf = _pick(dff, 512, 128)
    return pl.pallas_call(
        functools.partial(_ffn_kernel, tr=tr, n_ctx=n_ctx),
        grid=(b, t // tr, dff // tf),
        in_specs=[pl.BlockSpec((1, tr, d), lambda i, j, f: (i, j, 0)),
                  pl.BlockSpec((d, tf), lambda i, j, f: (0, f)),
                  pl.BlockSpec((d, tf), lambda i, j, f: (0, f)),
                  pl.BlockSpec((tf, d), lambda i, j, f: (f, 0)),
                  pl.BlockSpec((1, tr, d), lambda i, j, f: (i, j, 0)),
                  pl.BlockSpec((1, d), lambda i, j, f: (0, 0)),
                  pl.BlockSpec((1, 1, d), lambda i, j, f: (i, 0, 0))],
        out_specs=pl.BlockSpec((1, tr, d), lambda i, j, f: (i, j, 0)),
        out_shape=jax.ShapeDtypeStruct((b, t, d), F32),
        scratch_shapes=[pltpu.VMEM((tr, d), F32)],
        input_output_aliases={4: 0},
        compiler_params=_params(("arbitrary", "arbitrary", "arbitrary")),
        name="ffn",
    )(h, w1, w3, w2, x, g_c, g_x)


MOE_TILE = 512
GATHER_ROWS = 2048
GATHER_UNROLL = 8
COMBINE_ROWS = 544


def _gather_kernel(idx_ref, src_ref, o_ref, sem, *, rows):
    def copy(r):
        return pltpu.make_async_copy(src_ref.at[idx_ref[0, 0, r]], o_ref.at[r], sem)

    def start(r8, carry):
        for j in range(GATHER_UNROLL):
            copy(GATHER_UNROLL * r8 + j).start(priority=j % 2)
        return carry

    def wait(r8, carry):
        for j in range(GATHER_UNROLL):
            copy(GATHER_UNROLL * r8 + j).wait()
        return carry

    lax.fori_loop(0, rows // GATHER_UNROLL, start, 0)
    lax.fori_loop(0, rows // GATHER_UNROLL, wait, 0)


def _gather_rows(src_rows, idx):
    n_out = idx.shape[0]
    n_src, d = src_rows.shape
    assert d % 128 == 0
    rows = _pick(n_out, GATHER_ROWS, 16)
    steps = n_out // rows
    tiles = pl.pallas_call(
        functools.partial(_gather_kernel, rows=rows),
        grid=(steps,),
        in_specs=[pl.BlockSpec((1, 1, rows), lambda i: (i, 0, 0), memory_space=pltpu.SMEM),
                  pl.BlockSpec(memory_space=pl.ANY)],
        out_specs=pl.BlockSpec((rows, d // 128, 128), lambda i: (i, 0, 0)),
        out_shape=jax.ShapeDtypeStruct((n_out, d // 128, 128), src_rows.dtype),
        scratch_shapes=[pltpu.SemaphoreType.DMA(())],
        compiler_params=_params(("arbitrary",)),
        name="moe_gather",
    )(idx.reshape(steps, 1, rows), src_rows.reshape(n_src, d // 128, 128))
    return tiles.reshape(n_out, d)


def _expert_kernel(te_ref, nu_ref, x_ref, w1_ref, w3_ref, w2_ref, o_ref, acc_scr):
    i = pl.program_id(0)
    kf = pl.program_id(1)
    last = kf == pl.num_programs(1) - 1
    used = i < nu_ref[0]

    @pl.when(jnp.logical_and(used, kf == 0))
    def _():
        acc_scr[...] = jnp.zeros_like(acc_scr)

    @pl.when(used)
    def _():
        acc_scr[...] += _swiglu_step(x_ref[...].astype(BF16), w1_ref[0], w3_ref[0], w2_ref[0])

    @pl.when(jnp.logical_and(used, last))
    def _():
        o_ref[...] = acc_scr[...]

    @pl.when(jnp.logical_and(jnp.logical_not(used), last))
    def _():
        o_ref[...] = jnp.zeros_like(o_ref)


def _expert_ffn(xs, w1, w3, w2, tile_expert, n_used):
    n_sorted, d = xs.shape
    dff = w1.shape[2]
    tf = _pick(dff, 1792, 128)
    grid_spec = pltpu.PrefetchScalarGridSpec(
        num_scalar_prefetch=2,
        grid=(n_sorted // MOE_TILE, dff // tf),
        in_specs=[pl.BlockSpec((MOE_TILE, d), lambda i, f, te, nu: (i, 0)),
                  pl.BlockSpec((1, d, tf), lambda i, f, te, nu: (te[i], 0, f)),
                  pl.BlockSpec((1, d, tf), lambda i, f, te, nu: (te[i], 0, f)),
                  pl.BlockSpec((1, tf, d), lambda i, f, te, nu: (te[i], f, 0))],
        out_specs=pl.BlockSpec((MOE_TILE, d), lambda i, f, te, nu: (i, 0)),
        scratch_shapes=[pltpu.VMEM((MOE_TILE, d), F32)])
    return pl.pallas_call(
        _expert_kernel,
        grid_spec=grid_spec,
        out_shape=jax.ShapeDtypeStruct((n_sorted, d), F32),
        compiler_params=_params(("arbitrary", "arbitrary")),
        name="moe_experts",
    )(tile_expert, n_used, xs, w1, w3, w2)


def _combine_kernel(rw_ref, y1_ref, y2_ref, x_ref, gc_ref, gx_ref, o_ref, *, tr, n_ctx):
    rw = rw_ref[0]
    f = rw[:, 2:3] * y1_ref[0, 0] + rw[:, 3:4] * y2_ref[0, 0]
    g2 = jnp.where(_row_is_ctx(pl.program_id(1), tr, n_ctx), gc_ref[...], gx_ref[0])
    o_ref[0] = x_ref[0] + g2 * f


def _combine(route, y_tok, x, g_c, g_x, n_ctx):
    b, t, d = x.shape
    tr = _pick(t, COMBINE_ROWS, 16)
    return pl.pallas_call(
        functools.partial(_combine_kernel, tr=tr, n_ctx=n_ctx),
        grid=(b, t // tr),
        in_specs=[pl.BlockSpec((1, tr, 128), lambda i, j: (i, j, 0)),
                  pl.BlockSpec((1, 1, tr, d), lambda i, j: (0, i, j, 0)),
                  pl.BlockSpec((1, 1, tr, d), lambda i, j: (1, i, j, 0)),
                  pl.BlockSpec((1, tr, d), lambda i, j: (i, j, 0)),
                  pl.BlockSpec((1, d), lambda i, j: (0, 0)),
                  pl.BlockSpec((1, 1, d), lambda i, j: (i, 0, 0))],
        out_specs=pl.BlockSpec((1, tr, d), lambda i, j: (i, j, 0)),
        out_shape=jax.ShapeDtypeStruct((b, t, d), F32),
        input_output_aliases={3: 0},
        compiler_params=_params(("arbitrary", "arbitrary")),
        name="moe_combine",
    )(route, y_tok, y_tok, x, g_c, g_x)


def _moe_plan(e1, e2):
    n = e1.shape[0]
    e = jnp.concatenate([e1, e2])
    onehot = (e[:, None] == jnp.arange(N_EXPERTS, dtype=jnp.int32)[None, :]).astype(jnp.int32)
    counts = jnp.sum(onehot, axis=0)
    rank = jnp.sum((jnp.cumsum(onehot, axis=0) - onehot) * onehot, axis=1)
    padded = ((counts + MOE_TILE - 1) // MOE_TILE) * MOE_TILE
    ends = jnp.cumsum(padded)
    slot = jnp.sum((ends - padded)[None, :] * onehot, axis=1) + rank
    n_tiles = (2 * n + N_EXPERTS * (MOE_TILE - 1)) // MOE_TILE
    token = jnp.tile(jnp.arange(n, dtype=jnp.int32), 2)
    tokens_sorted = jnp.zeros((n_tiles * MOE_TILE,), jnp.int32).at[slot].set(
        token, unique_indices=True)
    tile_start = jnp.arange(n_tiles, dtype=jnp.int32) * MOE_TILE
    tile_expert = jnp.minimum(jnp.sum((ends[None, :] <= tile_start[:, None]).astype(jnp.int32), axis=1),
                              N_EXPERTS - 1)
    n_used = (ends[-1] // MOE_TILE).reshape(1)
    return slot[:n], slot[n:], tokens_sorted, tile_expert, n_used


def _moe(h_f32, route, w1, w3, w2, x, g_c, g_x, n_ctx):
    b, t, d = x.shape
    e1 = route[:, :, 0].reshape(-1).astype(jnp.int32)
    e2 = route[:, :, 1].reshape(-1).astype(jnp.int32)
    pos1, pos2, tokens_sorted, tile_expert, n_used = _moe_plan(e1, e2)
    xs = _gather_rows(h_f32.reshape(b * t, d), tokens_sorted)
    ys = _expert_ffn(xs, w1, w3, w2, tile_expert, n_used)
    y_tok = _gather_rows(ys, jnp.concatenate([pos1, pos2])).reshape(2, b, t, d)
    return _combine(route, y_tok, x, g_c, g_x, n_ctx)


def _router_kernel(h_ref, r_ref, o_ref):
    logits = jnp.dot(h_ref[0], r_ref[...], preferred_element_type=F32,
                     precision=lax.Precision.HIGHEST)
    lane = lax.broadcasted_iota(jnp.int32, logits.shape, 1).astype(F32)
    neg = -jnp.inf
    logits = jnp.where(lane < N_EXPERTS, logits, neg)
    m1 = jnp.max(logits, axis=-1, keepdims=True)
    i1 = jnp.min(jnp.where(logits == m1, lane, 128.0), axis=-1, keepdims=True)
    rest = jnp.where(lane == i1, neg, logits)
    m2 = jnp.max(rest, axis=-1, keepdims=True)
    i2 = jnp.min(jnp.where(rest == m2, lane, 128.0), axis=-1, keepdims=True)
    e2 = jnp.exp(m2 - m1)
    w1 = 1.0 / (1.0 + e2)
    w2 = e2 / (1.0 + e2)
    o_ref[0] = (jnp.where(lane == 0.0, i1, 0.0) + jnp.where(lane == 1.0, i2, 0.0)
                + jnp.where(lane == 2.0, w1, 0.0) + jnp.where(lane == 3.0, w2, 0.0))


def _router(h_f32, router_pad):
    b, t, d = h_f32.shape
    tr = _pick(t, 1088, 16)
    return pl.pallas_call(
        _router_kernel,
        grid=(b, t // tr),
        in_specs=[pl.BlockSpec((1, tr, d), lambda i, j: (i, j, 0)),
                  pl.BlockSpec((d, 128), lambda i, j: (0, 0))],
        out_specs=pl.BlockSpec((1, tr, 128), lambda i, j: (i, j, 0)),
        out_shape=jax.ShapeDtypeStruct((b, t, 128), F32),
        compiler_params=_params(("arbitrary", "arbitrary")),
        name="router",
    )(h_f32, router_pad)


def _rope_tables(seq, n_ctx):
    rows = seq // GRID_W
    row = jnp.repeat(jnp.arange(rows, dtype=F32), GRID_W)
    col = (jnp.arange(seq) % GRID_W).astype(F32)
    axis_dim = HEAD_DIM // 2
    inv_freq = ROPE_THETA ** (-jnp.arange(0, axis_dim, 2, dtype=F32) / axis_dim)
    ang_r = row[:, None] * inv_freq
    ang_c = col[:, None] * inv_freq
    cos = jnp.concatenate([jnp.cos(ang_r), jnp.cos(ang_r), jnp.cos(ang_c), jnp.cos(ang_c)], axis=-1)
    sin = jnp.concatenate([-jnp.sin(ang_r), jnp.sin(ang_r), -jnp.sin(ang_c), jnp.sin(ang_c)], axis=-1)
    cos = jnp.concatenate([jnp.ones((n_ctx, HEAD_DIM), F32), cos], axis=0)
    sin = jnp.concatenate([jnp.zeros((n_ctx, HEAD_DIM), F32), sin], axis=0)
    return jnp.tile(cos, (1, 2)), jnp.tile(sin, (1, 2))


def _block_ones(n, blk, value=1.0, dtype=BF16):
    i = np.arange(n) // blk
    return jnp.asarray((i[:, None] == i[None, :]).astype(np.float32) * value, dtype)


def _chunk_tri(n, chunk, rev):
    i = np.arange(n)
    same = (i[:, None] // chunk) == (i[None, :] // chunk)
    tri = (i[None, :] >= i[:, None]) if rev else (i[None, :] <= i[:, None])
    return jnp.asarray((same & tri).astype(np.float32), BF16)


def kernel(x, c, ctx, c_ctx, ada_w, ada_b, norm1_g, norm2_g, w_in, qk_norm_g, diff_lambda,
           diff_subln_g, conv_w, conv_b, conv_ln_g, conv_ln_b, hgrn_lb_logits, hgrn_norm_g,
           w_branch, w_out, ffn_w1, ffn_w3, ffn_w2, moe_router, moe_w1, moe_w3, moe_w2):
    b, seq, d = x.shape
    n_ctx = ctx.shape[1]
    t = n_ctx + seq
    depth = ada_w.shape[0]

    cos, sin = _rope_tables(seq, n_ctx)
    j_head_mean = _block_ones(128, HEAD_DIM, 1.0 / HEAD_DIM)
    j_hg_sum = _block_ones(HG_W // 2, HGRN_DK)
    j_hg_mean = _block_ones(HG_W // 2, HGRN_DK, 1.0 / HGRN_DK)
    head_mask = _block_ones(128, HGRN_DK, 1.0, F32)
    tri_f = _chunk_tri(HG_TILE, HG_SUB, False)
    tri_b = _chunk_tri(HG_TILE, HG_SUB, True)

    lb = jnp.cumsum(jax.nn.softmax(hgrn_lb_logits.astype(F32), axis=0), axis=0)
    lb = lb - lb[:1]
    log_lb = jnp.log(lb)
    log_1m_lb = jnp.log1p(-lb)

    rows = ((b + 1 + 7) // 8) * 8
    cvec = jnp.zeros((rows, d), F32).at[:b].set(c).at[b].set(c_ctx)
    mod = _ada(cvec, ada_w, ada_b).reshape(depth, rows, 6, d)

    xc = jnp.concatenate([ctx, x], axis=1)
    q_scale = LOG2E * HEAD_DIM ** -0.5

    for l in range(depth):
        mx = mod[l, :b]
        mc = mod[l, b]
        vx = lambda i: mx[:, i:i + 1, :]
        vc = lambda i: mc[i:i + 1, :]
        w_l = w_in[l].astype(BF16)

        h = _normmod(xc, norm1_g[l], vc(0), vc(1), vx(0), vx(1), n_ctx).reshape(b * t, d)
        proj = lambda cols, dt: _matmul(h, w_l[:, cols[0]:cols[1]], dt).reshape(b, t, -1)
        p_attn = proj(COL_ATTN, BF16)
        p_conv = proj(COL_CONV, BF16)
        p_hq = proj(COL_HQ, BF16)
        p_hf = proj(COL_HF, F32)
        p_hig = proj(COL_HIG, BF16)
        p_gate = proj(COL_GATE, BF16)

        g = qk_norm_g[l].astype(F32)
        gains = jnp.tile(jnp.stack([g[0] * q_scale, g[1], g[2] * q_scale, g[3]]), (1, 2))
        lam_init = 0.8 - 0.6 * math.exp(-0.3 * l)
        lp = diff_lambda[l].astype(F32)
        lam = (jnp.exp(jnp.sum(lp[0] * lp[1])) - jnp.exp(jnp.sum(lp[2] * lp[3]))
               + lam_init).reshape(1)
        qg, kg, vg, qd, kd, vd = _attn_prep(p_attn, cos, sin, gains, j_head_mean)
        o_gqa = _gqa_attention(qg, kg, vg, n_ctx)
        o_diff = _diff_attention(lam, qd, kd, vd, n_ctx)

        conv_o = _conv_branch(p_conv, conv_w[l], conv_b[l], conv_ln_g[l], conv_ln_b[l], n_ctx)

        hg_f = _hgrn_dir(p_hq, p_hf, p_hig, log_lb[l], log_1m_lb[l], tri_f, j_hg_sum, head_mask,
                         n_ctx, False)
        hg_b = _hgrn_dir(p_hq, p_hf, p_hig, log_lb[l], log_1m_lb[l], tri_b, j_hg_sum, head_mask,
                         n_ctx, True)

        xc = _merge(o_gqa, o_diff, conv_o, hg_f, hg_b, p_hig, p_gate, xc,
                    w_branch[l].astype(BF16), w_out[l].astype(BF16),
                    diff_subln_g[l].reshape(1, -1), jnp.tile(hgrn_norm_g[l], HGRN_HEADS).reshape(1, -1),
                    j_hg_mean, vc(2), vx(2), n_ctx, lam_init)

        if l % 2 == 0:
            h2 = _normmod(xc, norm2_g[l], vc(3), vc(4), vx(3), vx(4), n_ctx)
            i = l // 2
            xc = _ffn(h2, ffn_w1[i].astype(BF16), ffn_w3[i].astype(BF16), ffn_w2[i].astype(BF16),
                      xc, vc(5), vx(5), n_ctx)
        else:
            h2f = _normmod(xc, norm2_g[l], vc(3), vc(4), vx(3), vx(4), n_ctx, out_dtype=F32)
            i = l // 2
            router_pad = jnp.zeros((d, 128), F32).at[:, :N_EXPERTS].set(moe_router[i])
            route = _router(h2f, router_pad)
            xc = _moe(h2f, route, moe_w1[i].astype(BF16), moe_w3[i].astype(BF16),
                      moe_w2[i].astype(BF16), xc, vc(5), vx(5), n_ctx)
    return xc[:, n_ctx:]
```

```python
import functools
import math

import jax
import jax.numpy as jnp
import numpy as np
from jax import lax
from jax.experimental import pallas as pl
from jax.experimental.pallas import tpu as pltpu

F32 = jnp.float32
BF16 = jnp.bfloat16

GRID_W = 64
HEAD_DIM = 64
ROPE_THETA = 10000.0
EPS = 1e-6
GQA_HEADS = 8
GQA_KV_HEADS = 2
DIFF_HEADS = 4
CONV_CH = 512
CONV_WIDTH = 31
HGRN_HEADS = 8
HGRN_DK = 64
BRANCH_W = 512
N_EXPERTS = 8
LOG2E = 1.4426950408889634

COL_ATTN = (0, 2304)
COL_CONV = (2304, 3328)
COL_HQ = (3328, 3840)
COL_HF = (3840, 4864)
COL_HIG = (4864, 5888)
COL_GATE = (5888, 9984)

VMEM_LIMIT_BYTES = 50 * 1024 * 1024


def _params(sem):
    return pltpu.CompilerParams(dimension_semantics=sem, vmem_limit_bytes=VMEM_LIMIT_BYTES)


def _pick(n, target, mult):
    best = None
    for d in range(mult, min(n, target) + 1, mult):
        if n % d == 0:
            best = d
    assert best is not None, (n, target, mult)
    return best


def _silu(x):
    return x * jax.nn.sigmoid(x)


def _row_is_ctx(tile_idx, tr, n_ctx):
    row = tile_idx * tr + lax.broadcasted_iota(jnp.int32, (tr, 1), 0)
    return row < n_ctx


def _ada_kernel(c_ref, w_ref, b_ref, o_ref):
    s = _silu(c_ref[...])
    o_ref[0] = jnp.dot(s.astype(BF16), w_ref[0].astype(BF16),
                       preferred_element_type=F32) + b_ref[0]


def _ada(cvec, ada_w, ada_b):
    depth, d, n = ada_w.shape
    rows = cvec.shape[0]
    tn = _pick(n, 1536, 128)
    return pl.pallas_call(
        _ada_kernel,
        grid=(depth, n // tn),
        in_specs=[pl.BlockSpec((rows, d), lambda l, j: (0, 0)),
                  pl.BlockSpec((1, d, tn), lambda l, j: (l, 0, j)),
                  pl.BlockSpec((1, 1, tn), lambda l, j: (l, 0, j))],
        out_specs=pl.BlockSpec((1, rows, tn), lambda l, j: (l, 0, j)),
        out_shape=jax.ShapeDtypeStruct((depth, rows, n), F32),
        compiler_params=_params(("arbitrary", "arbitrary")),
        name="ada",
    )(cvec, ada_w, ada_b.reshape(depth, 1, n))


def _normmod_kernel(x_ref, g_ref, shc_ref, scc_ref, shx_ref, scx_ref, o_ref, *, tr, n_ctx):
    x = x_ref[0]
    ms = jnp.mean(x * x, axis=-1, keepdims=True)
    y = x * lax.rsqrt(ms + EPS) * g_ref[...]
    is_ctx = _row_is_ctx(pl.program_id(1), tr, n_ctx)
    sh = jnp.where(is_ctx, shc_ref[...], shx_ref[0])
    sc = jnp.where(is_ctx, scc_ref[...], scx_ref[0])
    o_ref[0] = (y * (1.0 + sc) + sh).astype(o_ref.dtype)


def _normmod(x, g, sh_c, sc_c, sh_x, sc_x, n_ctx, out_dtype=BF16):
    b, t, d = x.shape
    tr = _pick(t, 1088, 16)
    vec_c = pl.BlockSpec((1, d), lambda i, j: (0, 0))
    vec_x = pl.BlockSpec((1, 1, d), lambda i, j: (i, 0, 0))
    return pl.pallas_call(
        functools.partial(_normmod_kernel, tr=tr, n_ctx=n_ctx),
        grid=(b, t // tr),
        in_specs=[pl.BlockSpec((1, tr, d), lambda i, j: (i, j, 0)),
                  vec_c, vec_c, vec_c, vec_x, vec_x],
        out_specs=pl.BlockSpec((1, tr, d), lambda i, j: (i, j, 0)),
        out_shape=jax.ShapeDtypeStruct((b, t, d), out_dtype),
        compiler_params=_params(("arbitrary", "arbitrary")),
        name="normmod",
    )(x, g.reshape(1, d), sh_c, sc_c, sh_x, sc_x)


def _mm_kernel(a_ref, w_ref, o_ref):
    o_ref[...] = jnp.dot(a_ref[...], w_ref[...],
                         preferred_element_type=F32).astype(o_ref.dtype)


def _matmul(a, w, out_dtype):
    m, k = a.shape
    n = w.shape[1]
    tm = _pick(m, 2048, 256)
    tn = _pick(n, 1024, 128)
    return pl.pallas_call(
        _mm_kernel,
        grid=(m // tm, n // tn),
        in_specs=[pl.BlockSpec((tm, k), lambda i, j: (i, 0)),
                  pl.BlockSpec((k, tn), lambda i, j: (0, j))],
        out_specs=pl.BlockSpec((tm, tn), lambda i, j: (i, j)),
        out_shape=jax.ShapeDtypeStruct((m, n), out_dtype),
        compiler_params=_params(("arbitrary", "arbitrary")),
        name="proj",
    )(a, w)


def _prep_kernel(p_ref, cos_ref, sin_ref, g_ref, j_ref, qg_ref, kg_ref, vg_ref,
                 qd_ref, kd_ref, vd_ref):
    cos = cos_ref[...]
    sin = sin_ref[...]
    lane = lax.broadcasted_iota(jnp.int32, (1, 128), 1)
    first_half = (lane % 32) < 16

    def norm_rope(col, gi):
        x = p_ref[0, :, col:col + 128].astype(F32)
        ms = jnp.dot((x * x).astype(BF16), j_ref[...], preferred_element_type=F32)
        xn = x * lax.rsqrt(ms + EPS) * g_ref[gi:gi + 1, :]
        swapped = jnp.where(first_half, pltpu.roll(xn, 112, 1), pltpu.roll(xn, 16, 1))
        return xn * cos + swapped * sin

    def put_heads(ref, first, y):
        ref[0, first] = y[:, :HEAD_DIM].astype(ref.dtype)
        ref[0, first + 1] = y[:, HEAD_DIM:].astype(ref.dtype)

    def put_heads_t(ref, first, y):
        yt = y.T
        ref[0, first] = yt[:HEAD_DIM, :].astype(ref.dtype)
        ref[0, first + 1] = yt[HEAD_DIM:, :].astype(ref.dtype)

    tr = cos.shape[0]
    for g in range(4):
        put_heads_t(qg_ref, 2 * g, norm_rope(128 * g, 0))
    put_heads(kg_ref, 0, norm_rope(512, 1))
    for g in range(4):
        put_heads_t(qd_ref, 2 * g, norm_rope(768 + 128 * g, 2))
    for g in range(4):
        put_heads(kd_ref, 2 * g, norm_rope(1280 + 128 * g, 3))
    avt = p_ref[0, :, 640:768].astype(F32).T
    def ones_first_row(n):
        row = lax.broadcasted_iota(jnp.int32, (n, tr), 0)
        return jnp.where(row == 0, 1.0, 0.0).astype(vg_ref.dtype)

    for h in range(GQA_KV_HEADS):
        vg_ref[0, h, 0:HEAD_DIM, :] = avt[h * HEAD_DIM:(h + 1) * HEAD_DIM, :].astype(vg_ref.dtype)
        vg_ref[0, h, HEAD_DIM:2 * HEAD_DIM, :] = ones_first_row(HEAD_DIM)
    for h in range(DIFF_HEADS):
        bvt = p_ref[0, :, 1792 + 128 * h:1792 + 128 * (h + 1)].astype(F32).T
        vd_ref[0, h, 0:128, :] = bvt.astype(vd_ref.dtype)
        vd_ref[0, h, 128:128 + ATT_VPAD, :] = ones_first_row(ATT_VPAD)


def _attn_prep(p_attn, cos, sin, gains, jmat):
    b, t, w = p_attn.shape
    tr = _pick(t, 256, 128)
    heads = lambda n, width: pl.BlockSpec((1, n, tr, width), lambda i, j: (i, 0, j, 0))
    shape = lambda n, width: jax.ShapeDtypeStruct((b, n, t, width), BF16)
    heads_t = lambda n, rows: pl.BlockSpec((1, n, rows, tr), lambda i, j: (i, 0, 0, j))
    shape_t = lambda n, rows: jax.ShapeDtypeStruct((b, n, rows, t), BF16)
    return pl.pallas_call(
        _prep_kernel,
        grid=(b, t // tr),
        in_specs=[pl.BlockSpec((1, tr, w), lambda i, j: (i, j, 0)),
                  pl.BlockSpec((tr, 128), lambda i, j: (j, 0)),
                  pl.BlockSpec((tr, 128), lambda i, j: (j, 0)),
                  pl.BlockSpec((4, 128), lambda i, j: (0, 0)),
                  pl.BlockSpec((128, 128), lambda i, j: (0, 0))],
        out_specs=[heads_t(GQA_HEADS, HEAD_DIM), heads(GQA_KV_HEADS, HEAD_DIM),
                   heads_t(GQA_KV_HEADS, 128),
                   heads_t(2 * DIFF_HEADS, HEAD_DIM), heads(2 * DIFF_HEADS, HEAD_DIM),
                   heads_t(DIFF_HEADS, 128 + ATT_VPAD)],
        out_shape=[shape_t(GQA_HEADS, HEAD_DIM), shape(GQA_KV_HEADS, HEAD_DIM),
                   shape_t(GQA_KV_HEADS, 128),
                   shape_t(2 * DIFF_HEADS, HEAD_DIM), shape(2 * DIFF_HEADS, HEAD_DIM),
                   shape_t(DIFF_HEADS, 128 + ATT_VPAD)],
        compiler_params=_params(("arbitrary", "arbitrary")),
        name="attn_prep",
    )(p_attn, cos, sin, gains, jmat)


ATT_TQ = 256
ATT_TK = 256
ATT_VPAD = 16


def _scores(q_ref, k_ref, k_idx, s_scr, m_scr, slot, nk):
    tq, tk = ATT_TQ, ATT_TK
    qt = q_ref[0, slot]
    m_part = jnp.full((8, tq), -jnp.inf, F32)
    for c in range(nk // tk):
        s = jnp.dot(k_ref[0, k_idx, c * tk:(c + 1) * tk, :], qt, preferred_element_type=F32)
        s_scr[slot, c * tk:(c + 1) * tk, :] = s
        for u in range(tk // 8):
            m_part = jnp.maximum(m_part, s[8 * u:8 * (u + 1), :])
    m_scr[slot] = jnp.broadcast_to(jnp.max(m_part, axis=0, keepdims=True), (8, tq))


def _softmax_pv(v_ref, v_idx, s_scr, m_scr, slot, nk):
    tq, tk = ATT_TQ, ATT_TK
    m = m_scr[slot, 0:1, :]
    acc = jnp.zeros((v_ref.shape[2], tq), F32)
    for c in range(nk // tk):
        p = jnp.exp2((s_scr[slot, c * tk:(c + 1) * tk, :] - m).astype(BF16))
        acc = acc + jnp.dot(v_ref[0, v_idx, :, c * tk:(c + 1) * tk], p,
                            preferred_element_type=F32)
    return acc


def _for_each(n, fn):
    def body(i, carry):
        fn(i)
        return carry
    lax.fori_loop(0, n, body, 0)


def _ctx_or_full(n_ctx, t, fn):
    is_ctx_tile = pl.program_id(2) * ATT_TQ < n_ctx

    @pl.when(is_ctx_tile)
    def _():
        fn(n_ctx)

    @pl.when(jnp.logical_not(is_ctx_tile))
    def _():
        fn(t)


ATT_MAPS = 4


def _gqa_kernel(q_ref, k_ref, v_ref, o_ref, s_scr, m_scr, *, n_ctx, t):
    def attend(nk):
        def scores(i):
            for j in range(2):
                _scores(q_ref, k_ref, 0, s_scr, m_scr, 2 * i + j, nk)
        _for_each(ATT_MAPS // 2, scores)

        def finish(i):
            halves = []
            for j in range(2):
                acc = _softmax_pv(v_ref, 0, s_scr, m_scr, 2 * i + j, nk)
                halves.append(acc[0:HEAD_DIM, :] / acc[HEAD_DIM:HEAD_DIM + 1, :])
            o_ref[0, i] = jnp.concatenate(halves, axis=0).T.astype(o_ref.dtype)
        for i in range(ATT_MAPS // 2):
            finish(i)
    _ctx_or_full(n_ctx, t, attend)


def _diff_kernel(lam_ref, q_ref, k_ref, v_ref, o_ref, s_scr, m_scr, *, n_ctx, t):
    def attend(nk):
        def scores(h):
            for j in range(2):
                _scores(q_ref, k_ref, 2 * h + j, s_scr, m_scr, 2 * h + j, nk)
        _for_each(ATT_MAPS // 2, scores)

        def finish(h):
            a1 = _softmax_pv(v_ref, h, s_scr, m_scr, 2 * h, nk)
            a2 = _softmax_pv(v_ref, h, s_scr, m_scr, 2 * h + 1, nk)
            ot = a1[0:128, :] / a1[128:129, :] - lam_ref[0] * (a2[0:128, :] / a2[128:129, :])
            o_ref[0, h] = ot.T.astype(o_ref.dtype)
        for i in range(ATT_MAPS // 2):
            finish(i)
    _ctx_or_full(n_ctx, t, attend)


def _attention_call(body, name, q, k, v, n_out, k_block, v_block, extra_in, extra_args, n_ctx):
    b, n_q, _, t = q.shape
    assert n_ctx % ATT_TQ == 0 and t % ATT_TQ == 0 and n_ctx % ATT_TK == 0 and t % ATT_TK == 0
    v_rows = v.shape[2]
    return pl.pallas_call(
        functools.partial(body, n_ctx=n_ctx, t=t),
        grid=(b, n_q // ATT_MAPS, t // ATT_TQ),
        in_specs=extra_in + [
            pl.BlockSpec((1, ATT_MAPS, HEAD_DIM, ATT_TQ), lambda i, u, j: (i, u, 0, j)),
            pl.BlockSpec((1, k_block, t, HEAD_DIM), lambda i, u, j: (i, u, 0, 0)),
            pl.BlockSpec((1, v_block, v_rows, t), lambda i, u, j: (i, u, 0, 0))],
        out_specs=pl.BlockSpec((1, n_out, ATT_TQ, 128), lambda i, u, j: (i, u, j, 0)),
        out_shape=jax.ShapeDtypeStruct((b, n_q // ATT_MAPS * n_out, t, 128), BF16),
        scratch_shapes=[pltpu.VMEM((ATT_MAPS, t, ATT_TQ), F32),
                        pltpu.VMEM((ATT_MAPS, 8, ATT_TQ), F32)],
        compiler_params=_params(("arbitrary", "arbitrary", "arbitrary")),
        name=name,
    )(*extra_args, q, k, v)


def _gqa_attention(q, k, v, n_ctx):
    assert GQA_HEADS // GQA_KV_HEADS == ATT_MAPS
    return _attention_call(_gqa_kernel, "attn_gqa", q, k, v, ATT_MAPS // 2, 1, 1, [], [], n_ctx)


def _diff_attention(lam, q, k, v, n_ctx):
    return _attention_call(_diff_kernel, "attn_diff", q, k, v, ATT_MAPS // 2, ATT_MAPS,
                           ATT_MAPS // 2, [pl.BlockSpec(memory_space=pltpu.SMEM)], [lam], n_ctx)


CONV_TILE = 256
CONV_HALO = 16
CONV_ROWS = 128


def _conv_kernel(main_ref, left_ref, right_ref, w_ref, b_ref, lng_ref, lnb_ref, o_ref,
                 u_scr, *, n_ctx, t):
    j = pl.program_id(1)
    row0 = j * CONV_TILE
    has_left = jnp.logical_and(row0 != 0, row0 != n_ctx)
    has_right = jnp.logical_and(row0 + CONV_TILE != n_ctx, row0 + CONV_TILE != t)

    def glu(ref):
        x = ref[0].astype(F32)
        return x[:, :CONV_CH] * jax.nn.sigmoid(x[:, CONV_CH:])

    left = jnp.where(has_left, glu(left_ref), 0.0)
    right = jnp.where(has_right, glu(right_ref), 0.0)
    upad = jnp.concatenate([left, glu(main_ref), right], axis=0)
    n_pad = CONV_TILE + 2 * CONV_HALO
    for r in range(8):
        u_scr[r] = upad if r == 0 else pltpu.roll(upad, n_pad - r, 0)

    pad = CONV_WIDTH // 2

    def rows_body(i, carry):
        base = pl.multiple_of(i * CONV_ROWS, CONV_ROWS)
        acc = jnp.zeros((CONV_ROWS, CONV_CH), F32)
        for tap in range(CONV_WIDTH):
            off = tap + CONV_HALO - pad
            a, r = off // 8, off % 8
            acc = acc + w_ref[tap:tap + 1, :] * u_scr[r, pl.ds(base + 8 * a, CONV_ROWS), :]
        y = acc + b_ref[...]
        mu = jnp.mean(y, axis=-1, keepdims=True)
        yc = y - mu
        var = jnp.mean(yc * yc, axis=-1, keepdims=True)
        z = yc * lax.rsqrt(var + EPS) * lng_ref[...] + lnb_ref[...]
        o_ref[0, pl.ds(base, CONV_ROWS), :] = _silu(z).astype(o_ref.dtype)
        return carry

    lax.fori_loop(0, CONV_TILE // CONV_ROWS, rows_body, 0)


def _conv_branch(p_conv, w, bias, ln_g, ln_b, n_ctx):
    b, t, wd = p_conv.shape
    assert n_ctx % CONV_TILE == 0 and t % CONV_TILE == 0
    hb = CONV_TILE // CONV_HALO
    n_halo_blocks = t // CONV_HALO
    vec = pl.BlockSpec((1, CONV_CH), lambda i, j: (0, 0))
    return pl.pallas_call(
        functools.partial(_conv_kernel, n_ctx=n_ctx, t=t),
        grid=(b, t // CONV_TILE),
        in_specs=[pl.BlockSpec((1, CONV_TILE, wd), lambda i, j: (i, j, 0)),
                  pl.BlockSpec((1, CONV_HALO, wd),
                               lambda i, j: (i, jnp.maximum(j * hb - 1, 0), 0)),
                  pl.BlockSpec((1, CONV_HALO, wd),
                               lambda i, j: (i, jnp.minimum((j + 1) * hb, n_halo_blocks - 1), 0)),
                  pl.BlockSpec((CONV_WIDTH, CONV_CH), lambda i, j: (0, 0)),
                  vec, vec, vec],
        out_specs=pl.BlockSpec((1, CONV_TILE, CONV_CH), lambda i, j: (i, j, 0)),
        out_shape=jax.ShapeDtypeStruct((b, t, CONV_CH), BF16),
        scratch_shapes=[pltpu.VMEM((8, CONV_TILE + 2 * CONV_HALO, CONV_CH), F32)],
        compiler_params=_params(("arbitrary", "arbitrary")),
        name="conv",
    )(p_conv, p_conv, p_conv, w, bias.reshape(1, -1), ln_g.reshape(1, -1), ln_b.reshape(1, -1))


HG_TILE = 256
HG_SUB = 16
HG_ROWS = 64
HG_PAIRS = HGRN_HEADS // 2
HG_PAD = 8
HG_W = HGRN_HEADS * HGRN_DK


def _dot3(a_bf16, x):
    x1 = x.astype(BF16)
    r1 = x - x1.astype(F32)
    x2 = r1.astype(BF16)
    x3 = (r1 - x2.astype(F32)).astype(BF16)
    out = jnp.dot(a_bf16, x1, preferred_element_type=F32)
    out = out + jnp.dot(a_bf16, x2, preferred_element_type=F32)
    return out + jnp.dot(a_bf16, x3, preferred_element_type=F32)


def _hgrn_kernel(q_ref, z_ref, v_ref, la_ref, lb0_ref, tri_ref, j_ref, hm_ref, o_ref,
                 st_scr, q_scr, k_scr, c_scr, v_scr, kp_scr, cp_scr, vp_scr,
                 ks_scr, cs_scr, vs_scr, acc_scr, *, rev):
    c_sz = HG_SUB
    n_chunk = HG_TILE // c_sz

    @pl.when(pl.program_id(1) == 0)
    def _():
        st_scr[...] = jnp.zeros_like(st_scr)

    q = _silu(q_ref[0].astype(F32)) * (HGRN_DK ** -0.5)
    z = z_ref[0]
    v = v_ref[0].astype(F32)
    log_sig = jnp.minimum(z, 0.0) - jnp.log(1.0 + jnp.exp(-jnp.abs(z)))
    t1 = la_ref[...]
    t2 = lb0_ref[...] + log_sig
    logf = jnp.maximum(t1, t2) + jnp.log(1.0 + jnp.exp(-jnp.abs(t1 - t2)))
    k = 1.0 - jnp.exp(logf)
    c2 = _dot3(tri_ref[...], logf * LOG2E)
    q_scr[...] = q
    k_scr[...] = k
    c_scr[...] = c2
    v_scr[...] = v
    for ref, val in ((kp_scr, k), (cp_scr, c2), (vp_scr, v)):
        ref[0:HG_PAD, :] = jnp.zeros((HG_PAD, HG_W), F32)
        ref[HG_PAD:HG_PAD + HG_TILE, :] = val
        ref[HG_PAD + HG_TILE:HG_TILE + 2 * HG_PAD, :] = jnp.zeros((HG_PAD, HG_W), F32)

    order = range(n_chunk - 1, -1, -1) if rev else range(n_chunk)
    for ci in order:
        lo = ci * c_sz
        cj = c_scr[lo:lo + c_sz, :]
        c_end = cj[0:1] if rev else cj[c_sz - 1:c_sz]
        qd = (q_scr[lo:lo + c_sz, :] * jnp.exp2(cj)).astype(BF16)
        kd = (k_scr[lo:lo + c_sz, :] * jnp.exp2(c_end - cj)).astype(BF16)
        vb = v_scr[lo:lo + c_sz, :].astype(BF16)
        dec = jnp.exp2(c_end)
        outs = []
        for p in range(HG_PAIRS):
            sl = slice(128 * p, 128 * (p + 1))
            st = st_scr[p]
            outs.append(lax.dot_general(qd[:, sl], st.astype(BF16), (((1,), (1,)), ((), ())),
                                        preferred_element_type=F32))
            upd = lax.dot_general(vb[:, sl], kd[:, sl], (((0,), (0,)), ((), ())),
                                  preferred_element_type=F32)
            st_scr[p] = st * dec[:, sl] + upd * hm_ref[...]
        acc_scr[lo:lo + c_sz, :] = jnp.concatenate(outs, axis=1)

    row_in_chunk = lax.broadcasted_iota(jnp.int32, (HG_TILE, 1), 0) % c_sz

    def shifted(ref, r, mask):
        start = HG_PAD + r if rev else HG_PAD - r
        y = ref[start:start + HG_TILE, :]
        if not mask or r == 0:
            return y
        keep = (row_in_chunk < c_sz - r) if rev else (row_in_chunk >= r)
        return jnp.where(keep, y, 0.0)

    for r in range(8):
        if r == 0:
            k_src, c_src, v_src = k_scr, c_scr, v_scr
        else:
            ks_scr[...] = shifted(kp_scr, r, True)
            cs_scr[...] = shifted(cp_scr, r, True)
            vs_scr[...] = shifted(vp_scr, r, False)
            k_src, c_src, v_src = ks_scr, cs_scr, vs_scr
        half = HG_W // 2
        for blk in range(HG_TILE // HG_ROWS):
            base = blk * HG_ROWS
            for a in range(c_sz // 8):
                n = c_sz - 8 * a
                d_lo = 0 if rev else 8 * a
                s_lo = 8 * a if rev else 0
                starts = [base + ci * c_sz for ci in range(HG_ROWS // c_sz)]

                def rows(ref, lo_in_chunk):
                    if n == c_sz:
                        return ref[base:base + HG_ROWS, :]
                    return jnp.concatenate(
                        [ref[s + lo_in_chunk:s + lo_in_chunk + n, :] for s in starts], axis=0)

                qa = rows(q_scr, d_lo)
                ca = rows(c_scr, d_lo)
                kk = rows(k_src, s_lo)
                cc = rows(c_src, s_lo)
                vv = rows(v_src, s_lo)
                pw = (qa * kk * jnp.exp2(ca - cc)).astype(BF16)
                m = pw.shape[0]
                att2 = jnp.dot(jnp.concatenate([pw[:, :half], pw[:, half:]], axis=0), j_ref[...],
                               preferred_element_type=F32)
                contrib = jnp.concatenate([att2[:m], att2[m:]], axis=1) * vv
                if n == c_sz:
                    acc_scr[base:base + HG_ROWS, :] = acc_scr[base:base + HG_ROWS, :] + contrib
                else:
                    for i, s in enumerate(starts):
                        dst = s + d_lo
                        acc_scr[dst:dst + n, :] = acc_scr[dst:dst + n, :] + contrib[i * n:(i + 1) * n]

    o_ref[0] = acc_scr[...]


def _hgrn_dir(p_hq, p_hf, p_hig, la, lb0, tri, jmat, head_mask, n_ctx, rev):
    b, t, _ = p_hq.shape
    assert n_ctx % HG_TILE == 0 and t % HG_TILE == 0
    n_tiles = t // HG_TILE
    n_ctx_tiles = n_ctx // HG_TILE
    col = 1 if rev else 0

    def tile(j):
        if not rev:
            return j
        return jnp.where(j < n_ctx_tiles, n_ctx_tiles - 1 - j, n_tiles - 1 - (j - n_ctx_tiles))

    const = lambda shape: pl.BlockSpec(shape, lambda i, j: (0,) * len(shape))
    return pl.pallas_call(
        functools.partial(_hgrn_kernel, rev=rev),
        grid=(b, n_tiles),
        in_specs=[pl.BlockSpec((1, HG_TILE, HG_W), lambda i, j: (i, tile(j), 0)),
                  pl.BlockSpec((1, HG_TILE, HG_W), lambda i, j: (i, tile(j), col)),
                  pl.BlockSpec((1, HG_TILE, HG_W), lambda i, j: (i, tile(j), 0)),
                  const((1, HG_W)),
                  const((1, HG_W)),
                  const((HG_TILE, HG_TILE)), const((HG_W // 2, HG_W // 2)), const((128, 128))],
        out_specs=pl.BlockSpec((1, HG_TILE, HG_W), lambda i, j: (i, tile(j), 0)),
        out_shape=jax.ShapeDtypeStruct((b, t, HG_W), F32),
        scratch_shapes=[pltpu.VMEM((HG_PAIRS, 128, 128), F32)]
                       + [pltpu.VMEM((HG_TILE, HG_W), F32) for _ in range(4)]
                       + [pltpu.VMEM((HG_TILE + 2 * HG_PAD, HG_W), F32) for _ in range(3)]
                       + [pltpu.VMEM((HG_TILE, HG_W), F32) for _ in range(4)],
        compiler_params=_params(("arbitrary", "arbitrary")),
        name="hgrn_bwd" if rev else "hgrn_fwd",
    )(p_hq, p_hf, p_hig, la[col:col + 1], lb0[col:col + 1], tri, jmat, head_mask)


def _merge_kernel(og_ref, od_ref, cv_ref, hf_ref, hb_ref, hg_ref, gt_ref, x_ref, wb_ref, wo_ref,
                  sub_ref, hng_ref, j_ref, gc_ref, gx_ref, o_ref, *, tr, n_ctx, lam_init):
    gates = gt_ref[0]

    def gate(i):
        return jax.nn.sigmoid(gates[:, i * 1024:(i + 1) * 1024].astype(F32))

    oa = jnp.concatenate([og_ref[0, i] for i in range(GQA_HEADS // 2)], axis=1)
    merged = gate(0) * jnp.dot(oa, wb_ref[0], preferred_element_type=F32)
    ob = []
    for h in range(DIFF_HEADS):
        od = od_ref[0, h].astype(F32)
        ms = jnp.mean(od * od, axis=-1, keepdims=True)
        ob.append((od * lax.rsqrt(ms + EPS) * sub_ref[...] * (1.0 - lam_init)).astype(BF16))
    merged = merged + gate(1) * jnp.dot(jnp.concatenate(ob, axis=1), wb_ref[1],
                                        preferred_element_type=F32)
    merged = merged + gate(2) * jnp.dot(cv_ref[0], wb_ref[2], preferred_element_type=F32)
    od = hf_ref[0] + hb_ref[0]
    sq = (od * od).astype(BF16)
    half = HG_W // 2
    ms2 = jnp.dot(jnp.concatenate([sq[:, :half], sq[:, half:]], axis=0), j_ref[...],
                  preferred_element_type=F32)
    ms = jnp.concatenate([ms2[:tr], ms2[tr:]], axis=1)
    on = od * lax.rsqrt(ms + EPS) * hng_ref[...] * _silu(hg_ref[0].astype(F32))
    merged = merged + gate(3) * jnp.dot(on.astype(BF16), wb_ref[3], preferred_element_type=F32)

    mix = jnp.dot(merged.astype(BF16), wo_ref[...], preferred_element_type=F32)
    g1 = jnp.where(_row_is_ctx(pl.program_id(1), tr, n_ctx), gc_ref[...], gx_ref[0])
    o_ref[0] = x_ref[0] + g1 * mix


def _merge(o_gqa, o_diff, conv_o, hg_f, hg_b, p_hig, p_gate, x, wb, wo, subln_g, hgrn_g, jmat64,
           g_c, g_x, n_ctx, lam_init):
    b, t, d = x.shape
    tr = _pick(t, 272, 16)
    const = lambda shape: pl.BlockSpec(shape, lambda i, j: (0,) * len(shape))
    return pl.pallas_call(
        functools.partial(_merge_kernel, tr=tr, n_ctx=n_ctx, lam_init=lam_init),
        grid=(b, t // tr),
        in_specs=[pl.BlockSpec((1, GQA_HEADS // 2, tr, 128), lambda i, j: (i, 0, j, 0)),
                  pl.BlockSpec((1, DIFF_HEADS, tr, 128), lambda i, j: (i, 0, j, 0)),
                  pl.BlockSpec((1, tr, CONV_CH), lambda i, j: (i, j, 0)),
                  pl.BlockSpec((1, tr, HG_W), lambda i, j: (i, j, 0)),
                  pl.BlockSpec((1, tr, HG_W), lambda i, j: (i, j, 0)),
                  pl.BlockSpec((1, tr, HG_W), lambda i, j: (i, j, 1)),
                  pl.BlockSpec((1, tr, 4 * d), lambda i, j: (i, j, 0)),
                  pl.BlockSpec((1, tr, d), lambda i, j: (i, j, 0)),
                  const((4, BRANCH_W, d)), const((d, d)),
                  const((1, 128)), const((1, HG_W)), const((HG_W // 2, HG_W // 2)),
                  const((1, d)),
                  pl.BlockSpec((1, 1, d), lambda i, j: (i, 0, 0))],
        out_specs=pl.BlockSpec((1, tr, d), lambda i, j: (i, j, 0)),
        out_shape=jax.ShapeDtypeStruct((b, t, d), F32),
        compiler_params=_params(("arbitrary", "arbitrary")),
        name="merge",
    )(o_gqa, o_diff, conv_o, hg_f, hg_b, p_hig, p_gate, x, wb, wo, subln_g, hgrn_g, jmat64,
      g_c, g_x)


def _swiglu_step(h, w1, w3, w2):
    a = jnp.dot(h, w1, preferred_element_type=F32)
    g = jnp.dot(h, w3, preferred_element_type=F32)
    return jnp.dot((_silu(a) * g).astype(BF16), w2, preferred_element_type=F32)


def _ffn_kernel(h_ref, w1_ref, w3_ref, w2_ref, x_ref, gc_ref, gx_ref, o_ref, acc_scr, *, tr, n_ctx):
    kf = pl.program_id(2)

    @pl.when(kf == 0)
    def _():
        acc_scr[...] = jnp.zeros_like(acc_scr)

    acc_scr[...] += _swiglu_step(h_ref[0], w1_ref[...], w3_ref[...], w2_ref[...])

    @pl.when(kf == pl.num_programs(2) - 1)
    def _():
        g2 = jnp.where(_row_is_ctx(pl.program_id(1), tr, n_ctx), gc_ref[...], gx_ref[0])
        o_ref[0] = x_ref[0] + g2 * acc_scr[...]


def _ffn(h, w1, w3, w2, x, g_c, g_x, n_ctx):
    b, t, d = x.shape
    dff = w1.shape[1]
    tr = _pick(t, 544, 16)
    tf = _pick(dff, 1792, 128)
    return pl.pallas_call(
        functools.partial(_ffn_kernel, tr=tr, n_ctx=n_ctx),
        grid=(b, t // tr, dff // tf),
        in_specs=[pl.BlockSpec((1, tr, d), lambda i, j, f: (i, j, 0)),
                  pl.BlockSpec((d, tf), lambda i, j, f: (0, f)),
                  pl.BlockSpec((d, tf), lambda i, j, f: (0, f)),
                  pl.BlockSpec((tf, d), lambda i, j, f: (f, 0)),
                  pl.BlockSpec((1, tr, d), lambda i, j, f: (i, j, 0)),
                  pl.BlockSpec((1, d), lambda i, j, f: (0, 0)),
                  pl.BlockSpec((1, 1, d), lambda i, j, f: (i, 0, 0))],
        out_specs=pl.BlockSpec((1, tr, d), lambda i, j, f: (i, j, 0)),
        out_shape=jax.ShapeDtypeStruct((b, t, d), F32),
        scratch_shapes=[pltpu.VMEM((tr, d), F32)],
        input_output_aliases={4: 0},
        compiler_params=_params(("arbitrary", "arbitrary", "arbitrary")),
        name="ffn",
    )(h, w1, w3, w2, x, g_c, g_x)


MOE_TILE = 512
GATHER_ROWS = 2048
GATHER_UNROLL = 8
COMBINE_ROWS = 544


def _gather_kernel(idx_ref, src_ref, o_ref, sem, *, rows):
    def copy(r):
        return pltpu.make_async_copy(src_ref.at[idx_ref[0, 0, r]], o_ref.at[r], sem)

    def start(r8, carry):
        for j in range(GATHER_UNROLL):
            copy(GATHER_UNROLL * r8 + j).start(priority=j % 2)
        return carry

    def wait(r8, carry):
        for j in range(GATHER_UNROLL):
            copy(GATHER_UNROLL * r8 + j).wait()
        return carry

    lax.fori_loop(0, rows // GATHER_UNROLL, start, 0)
    lax.fori_loop(0, rows // GATHER_UNROLL, wait, 0)


def _gather_rows(src_rows, idx):
    n_out = idx.shape[0]
    n_src, d = src_rows.shape
    assert d % 128 == 0
    rows = _pick(n_out, GATHER_ROWS, 16)
    steps = n_out // rows
    tiles = pl.pallas_call(
        functools.partial(_gather_kernel, rows=rows),
        grid=(steps,),
        in_specs=[pl.BlockSpec((1, 1, rows), lambda i: (i, 0, 0), memory_space=pltpu.SMEM),
                  pl.BlockSpec(memory_space=pl.ANY)],
        out_specs=pl.BlockSpec((rows, d // 128, 128), lambda i: (i, 0, 0)),
        out_shape=jax.ShapeDtypeStruct((n_out, d // 128, 128), src_rows.dtype),
        scratch_shapes=[pltpu.SemaphoreType.DMA(())],
        compiler_params=_params(("arbitrary",)),
        name="moe_gather",
    )(idx.reshape(steps, 1, rows), src_rows.reshape(n_src, d // 128, 128))
    return tiles.reshape(n_out, d)


def _expert_kernel(te_ref, nu_ref, x_ref, w1_ref, w3_ref, w2_ref, o_ref, acc_scr):
    i = pl.program_id(0)
    kf = pl.program_id(1)
    last = kf == pl.num_programs(1) - 1
    used = i < nu_ref[0]

    @pl.when(jnp.logical_and(used, kf == 0))
    def _():
        acc_scr[...] = jnp.zeros_like(acc_scr)

    @pl.when(used)
    def _():
        acc_scr[...] += _swiglu_step(x_ref[...].astype(BF16), w1_ref[0], w3_ref[0], w2_ref[0])

    @pl.when(jnp.logical_and(used, last))
    def _():
        o_ref[...] = acc_scr[...]

    @pl.when(jnp.logical_and(jnp.logical_not(used), last))
    def _():
        o_ref[...] = jnp.zeros_like(o_ref)


def _expert_ffn(xs, w1, w3, w2, tile_expert, n_used):
    n_sorted, d = xs.shape
    dff = w1.shape[2]
    tf = _pick(dff, 1792, 128)
    grid_spec = pltpu.PrefetchScalarGridSpec(
        num_scalar_prefetch=2,
        grid=(n_sorted // MOE_TILE, dff // tf),
        in_specs=[pl.BlockSpec((MOE_TILE, d), lambda i, f, te, nu: (i, 0)),
                  pl.BlockSpec((1, d, tf), lambda i, f, te, nu: (te[i], 0, f)),
                  pl.BlockSpec((1, d, tf), lambda i, f, te, nu: (te[i], 0, f)),
                  pl.BlockSpec((1, tf, d), lambda i, f, te, nu: (te[i], f, 0))],
        out_specs=pl.BlockSpec((MOE_TILE, d), lambda i, f, te, nu: (i, 0)),
        scratch_shapes=[pltpu.VMEM((MOE_TILE, d), F32)])
    return pl.pallas_call(
        _expert_kernel,
        grid_spec=grid_spec,
        out_shape=jax.ShapeDtypeStruct((n_sorted, d), F32),
        compiler_params=_params(("arbitrary", "arbitrary")),
        name="moe_experts",
    )(tile_expert, n_used, xs, w1, w3, w2)


def _combine_kernel(rw_ref, y1_ref, y2_ref, x_ref, gc_ref, gx_ref, o_ref, *, tr, n_ctx):
    rw = rw_ref[0]
    f = rw[:, 2:3] * y1_ref[0, 0] + rw[:, 3:4] * y2_ref[0, 0]
    g2 = jnp.where(_row_is_ctx(pl.program_id(1), tr, n_ctx), gc_ref[...], gx_ref[0])
    o_ref[0] = x_ref[0] + g2 * f


def _combine(route, y_tok, x, g_c, g_x, n_ctx):
    b, t, d = x.shape
    tr = _pick(t, COMBINE_ROWS, 16)
    return pl.pallas_call(
        functools.partial(_combine_kernel, tr=tr, n_ctx=n_ctx),
        grid=(b, t // tr),
        in_specs=[pl.BlockSpec((1, tr, 128), lambda i, j: (i, j, 0)),
                  pl.BlockSpec((1, 1, tr, d), lambda i, j: (0, i, j, 0)),
                  pl.BlockSpec((1, 1, tr, d), lambda i, j: (1, i, j, 0)),
                  pl.BlockSpec((1, tr, d), lambda i, j: (i, j, 0)),
                  pl.BlockSpec((1, d), lambda i, j: (0, 0)),
                  pl.BlockSpec((1, 1, d), lambda i, j: (i, 0, 0))],
        out_specs=pl.BlockSpec((1, tr, d), lambda i, j: (i, j, 0)),
        out_shape=jax.ShapeDtypeStruct((b, t, d), F32),
        input_output_aliases={3: 0},
        compiler_params=_params(("arbitrary", "arbitrary")),
        name="moe_combine",
    )(route, y_tok, y_tok, x, g_c, g_x)


def _moe_plan(e1, e2):
    n = e1.shape[0]
    e = jnp.concatenate([e1, e2])
    onehot = (e[:, None] == jnp.arange(N_EXPERTS, dtype=jnp.int32)[None, :]).astype(jnp.int32)
    counts = jnp.sum(onehot, axis=0)
    rank = jnp.sum((jnp.cumsum(onehot, axis=0) - onehot) * onehot, axis=1)
    padded = ((counts + MOE_TILE - 1) // MOE_TILE) * MOE_TILE
    ends = jnp.cumsum(padded)
    slot = jnp.sum((ends - padded)[None, :] * onehot, axis=1) + rank
    n_tiles = (2 * n + N_EXPERTS * (MOE_TILE - 1)) // MOE_TILE
    token = jnp.tile(jnp.arange(n, dtype=jnp.int32), 2)
    tokens_sorted = jnp.zeros((n_tiles * MOE_TILE,), jnp.int32).at[slot].set(
        token, unique_indices=True)
    tile_start = jnp.arange(n_tiles, dtype=jnp.int32) * MOE_TILE
    tile_expert = jnp.minimum(jnp.sum((ends[None, :] <= tile_start[:, None]).astype(jnp.int32), axis=1),
                              N_EXPERTS - 1)
    n_used = (ends[-1] // MOE_TILE).reshape(1)
    return slot[:n], slot[n:], tokens_sorted, tile_expert, n_used


def _moe(h_f32, route, w1, w3, w2, x, g_c, g_x, n_ctx):
    b, t, d = x.shape
    e1 = route[:, :, 0].reshape(-1).astype(jnp.int32)
    e2 = route[:, :, 1].reshape(-1).astype(jnp.int32)
    pos1, pos2, tokens_sorted, tile_expert, n_used = _moe_plan(e1, e2)
    xs = _gather_rows(h_f32.reshape(b * t, d), tokens_sorted)
    ys = _expert_ffn(xs, w1, w3, w2, tile_expert, n_used)
    y_tok = _gather_rows(ys, jnp.concatenate([pos1, pos2])).reshape(2, b, t, d)
    return _combine(route, y_tok, x, g_c, g_x, n_ctx)


def _router_kernel(h_ref, r_ref, o_ref):
    logits = jnp.dot(h_ref[0], r_ref[...], preferred_element_type=F32,
                     precision=lax.Precision.HIGHEST)
    lane = lax.broadcasted_iota(jnp.int32, logits.shape, 1).astype(F32)
    neg = -jnp.inf
    logits = jnp.where(lane < N_EXPERTS, logits, neg)
    m1 = jnp.max(logits, axis=-1, keepdims=True)
    i1 = jnp.min(jnp.where(logits == m1, lane, 128.0), axis=-1, keepdims=True)
    rest = jnp.where(lane == i1, neg, logits)
    m2 = jnp.max(rest, axis=-1, keepdims=True)
    i2 = jnp.min(jnp.where(rest == m2, lane, 128.0), axis=-1, keepdims=True)
    e2 = jnp.exp(m2 - m1)
    w1 = 1.0 / (1.0 + e2)
    w2 = e2 / (1.0 + e2)
    o_ref[0] = (jnp.where(lane == 0.0, i1, 0.0) + jnp.where(lane == 1.0, i2, 0.0)
                + jnp.where(lane == 2.0, w1, 0.0) + jnp.where(lane == 3.0, w2, 0.0))


def _router(h_f32, router_pad):
    b, t, d = h_f32.shape
    tr = _pick(t, 1088, 16)
    return pl.pallas_call(
        _router_kernel,
        grid=(b, t // tr),
        in_specs=[pl.BlockSpec((1, tr, d), lambda i, j: (i, j, 0)),
                  pl.BlockSpec((d, 128), lambda i, j: (0, 0))],
        out_specs=pl.BlockSpec((1, tr, 128), lambda i, j: (i, j, 0)),
        out_shape=jax.ShapeDtypeStruct((b, t, 128), F32),
        compiler_params=_params(("arbitrary", "arbitrary")),
        name="router",
    )(h_f32, router_pad)


def _rope_tables(seq, n_ctx):
    rows = seq // GRID_W
    row = jnp.repeat(jnp.arange(rows, dtype=F32), GRID_W)
    col = (jnp.arange(seq) % GRID_W).astype(F32)
    axis_dim = HEAD_DIM // 2
    inv_freq = ROPE_THETA ** (-jnp.arange(0, axis_dim, 2, dtype=F32) / axis_dim)
    ang_r = row[:, None] * inv_freq
    ang_c = col[:, None] * inv_freq
    cos = jnp.concatenate([jnp.cos(ang_r), jnp.cos(ang_r), jnp.cos(ang_c), jnp.cos(ang_c)], axis=-1)
    sin = jnp.concatenate([-jnp.sin(ang_r), jnp.sin(ang_r), -jnp.sin(ang_c), jnp.sin(ang_c)], axis=-1)
    cos = jnp.concatenate([jnp.ones((n_ctx, HEAD_DIM), F32), cos], axis=0)
    sin = jnp.concatenate([jnp.zeros((n_ctx, HEAD_DIM), F32), sin], axis=0)
    return jnp.tile(cos, (1, 2)), jnp.tile(sin, (1, 2))


def _block_ones(n, blk, value=1.0, dtype=BF16):
    i = np.arange(n) // blk
    return jnp.asarray((i[:, None] == i[None, :]).astype(np.float32) * value, dtype)


def _chunk_tri(n, chunk, rev):
    i = np.arange(n)
    same = (i[:, None] // chunk) == (i[None, :] // chunk)
    tri = (i[None, :] >= i[:, None]) if rev else (i[None, :] <= i[:, None])
    return jnp.asarray((same & tri).astype(np.float32), BF16)


def kernel(x, c, ctx, c_ctx, ada_w, ada_b, norm1_g, norm2_g, w_in, qk_norm_g, diff_lambda,
           diff_subln_g, conv_w, conv_b, conv_ln_g, conv_ln_b, hgrn_lb_logits, hgrn_norm_g,
           w_branch, w_out, ffn_w1, ffn_w3, ffn_w2, moe_router, moe_w1, moe_w3, moe_w2):
    b, seq, d = x.shape
    n_ctx = ctx.shape[1]
    t = n_ctx + seq
    depth = ada_w.shape[0]

    cos, sin = _rope_tables(seq, n_ctx)
    j_head_mean = _block_ones(128, HEAD_DIM, 1.0 / HEAD_DIM)
    j_hg_sum = _block_ones(HG_W // 2, HGRN_DK)
    j_hg_mean = _block_ones(HG_W // 2, HGRN_DK, 1.0 / HGRN_DK)
    head_mask = _block_ones(128, HGRN_DK, 1.0, F32)
    tri_f = _chunk_tri(HG_TILE, HG_SUB, False)
    tri_b = _chunk_tri(HG_TILE, HG_SUB, True)

    lb = jnp.cumsum(jax.nn.softmax(hgrn_lb_logits.astype(F32), axis=0), axis=0)
    lb = lb - lb[:1]
    log_lb = jnp.log(lb)
    log_1m_lb = jnp.log1p(-lb)

    rows = ((b + 1 + 7) // 8) * 8
    cvec = jnp.zeros((rows, d), F32).at[:b].set(c).at[b].set(c_ctx)
    mod = _ada(cvec, ada_w, ada_b).reshape(depth, rows, 6, d)

    xc = jnp.concatenate([ctx, x], axis=1)
    q_scale = LOG2E * HEAD_DIM ** -0.5

    for l in range(depth):
        mx = mod[l, :b]
        mc = mod[l, b]
        vx = lambda i: mx[:, i:i + 1, :]
        vc = lambda i: mc[i:i + 1, :]
        w_l = w_in[l].astype(BF16)

        h = _normmod(xc, norm1_g[l], vc(0), vc(1), vx(0), vx(1), n_ctx).reshape(b * t, d)
        proj = lambda cols, dt: _matmul(h, w_l[:, cols[0]:cols[1]], dt).reshape(b, t, -1)
        p_attn = proj(COL_ATTN, BF16)
        p_conv = proj(COL_CONV, BF16)
        p_hq = proj(COL_HQ, BF16)
        p_hf = proj(COL_HF, F32)
        p_hig = proj(COL_HIG, BF16)
        p_gate = proj(COL_GATE, BF16)

        g = qk_norm_g[l].astype(F32)
        gains = jnp.tile(jnp.stack([g[0] * q_scale, g[1], g[2] * q_scale, g[3]]), (1, 2))
        lam_init = 0.8 - 0.6 * math.exp(-0.3 * l)
        lp = diff_lambda[l].astype(F32)
        lam = (jnp.exp(jnp.sum(lp[0] * lp[1])) - jnp.exp(jnp.sum(lp[2] * lp[3]))
               + lam_init).reshape(1)
        qg, kg, vg, qd, kd, vd = _attn_prep(p_attn, cos, sin, gains, j_head_mean)
        o_gqa = _gqa_attention(qg, kg, vg, n_ctx)
        o_diff = _diff_attention(lam, qd, kd, vd, n_ctx)

        conv_o = _conv_branch(p_conv, conv_w[l], conv_b[l], conv_ln_g[l], conv_ln_b[l], n_ctx)

        hg_f = _hgrn_dir(p_hq, p_hf, p_hig, log_lb[l], log_1m_lb[l], tri_f, j_hg_sum, head_mask,
                         n_ctx, False)
        hg_b = _hgrn_dir(p_hq, p_hf, p_hig, log_lb[l], log_1m_lb[l], tri_b, j_hg_sum, head_mask,
                         n_ctx, True)

        xc = _merge(o_gqa, o_diff, conv_o, hg_f, hg_b, p_hig, p_gate, xc,
                    w_branch[l].astype(BF16), w_out[l].astype(BF16),
                    diff_subln_g[l].reshape(1, -1), jnp.tile(hgrn_norm_g[l], HGRN_HEADS).reshape(1, -1),
                    j_hg_mean, vc(2), vx(2), n_ctx, lam_init)

        if l % 2 == 0:
            h2 = _normmod(xc, norm2_g[l], vc(3), vc(4), vx(3), vx(4), n_ctx)
            i = l // 2
            xc = _ffn(h2, ffn_w1[i].astype(BF16), ffn_w3[i].astype(BF16), ffn_w2[i].astype(BF16),
                      xc, vc(5), vx(5), n_ctx)
        else:
            h2f = _normmod(xc, norm2_g[l], vc(3), vc(4), vx(3), vx(4), n_ctx, out_dtype=F32)
            i = l // 2
            router_pad = jnp.zeros((d, 128), F32).at[:, :N_EXPERTS].set(moe_router[i])
            route = _router(h2f, router_pad)
            xc = _moe(h2f, route, moe_w1[i].astype(BF16), moe_w3[i].astype(BF16),
                      moe_w2[i].astype(BF16), xc, vc(5), vx(5), n_ctx)
    return xc[:, n_ctx:]
```
